```python
import math
import jax
import jax.numpy as jnp
from jax import lax
import numpy as np

D_MODEL = 4096
BATCH = 4
SEQ = 2048
DEPTH = 2
DEC_BATCH = 8
DEC_SEQ = 1
PAST_LEN = 16384
PAGE_SIZE = 128

HEAD_DIM = 128
N_EVEN = (DEPTH + 1) // 2
N_ODD = DEPTH // 2
POOL_WIDTH = D_MODEL // 4
POOL_WINDOWS = (2, 4, 8, 16)
POOL_GROUPS = len(POOL_WINDOWS)
POOL_GROUP_DIM = POOL_WIDTH // POOL_GROUPS
POOL_STATE = max(POOL_WINDOWS) - 1
NSA_HEADS = (D_MODEL - POOL_WIDTH) // HEAD_DIM
NSA_KV_HEADS = 2
NSA_GROUP = NSA_HEADS // NSA_KV_HEADS
CMP_BLOCK = 32
CMP_STRIDE = 16
SEL_BLOCK = 64
N_SEL = 16
WINDOW = 512
Q_BLOCK = 128
SC_WIDTH = D_MODEL // 4
SC_CONV = 3
SB_HEADS = (D_MODEL - SC_WIDTH) // HEAD_DIM
N_BUCKETS = 32
MAX_DISTANCE = 128
N_MEM = 256
MEM_HEADS = 4
MEM_HEAD_DIM = 256
MEM_WIDTH = MEM_HEADS * MEM_HEAD_DIM
D_FF = ((8 * D_MODEL // 3 + 255) // 256) * 256
FFN_CONV = 3
ALPHA = (2.0 * DEPTH) ** 0.25
BETA = (8.0 * DEPTH) ** -0.25
LN_EPS = 1e-5
NEG_INF = -1e30
BIG = 1e30

MIX_EVEN = POOL_WIDTH + NSA_HEADS * HEAD_DIM
MIX_ODD = SC_WIDTH + SB_HEADS * HEAD_DIM
KV_W = NSA_KV_HEADS * HEAD_DIM
EVEN_SPLITS = (POOL_WIDTH, NSA_HEADS * HEAD_DIM, KV_W, KV_W, KV_W, KV_W, 2 * KV_W, 3 * NSA_HEADS)
ODD_SPLITS = (SC_WIDTH, SC_WIDTH, SC_WIDTH, SB_HEADS * HEAD_DIM, SB_HEADS * HEAD_DIM, SB_HEADS * HEAD_DIM)

kernel_name = 'hybrid_pool_nsa_shortconv_stickbreak_decode_step'


def split_cols(h, sizes):
    offs = np.cumsum((0,) + tuple(sizes))
    return [h[..., int(offs[i]):int(offs[i + 1])] for i in range(len(sizes))]


def layer_norm(x, g, b):
    xf = x.astype(jnp.float32)
    mu = jnp.mean(xf, axis=-1, keepdims=True)
    var = jnp.mean(jnp.square(xf - mu), axis=-1, keepdims=True)
    return ((xf - mu) * lax.rsqrt(var + LN_EPS) * g + b).astype(x.dtype)


def masked_softmax(logits, mask):
    p = jax.nn.softmax(jnp.where(mask, logits, NEG_INF), axis=-1)
    return jnp.where(mask, p, 0.0)


def rel_bucket(dist):
    n = jnp.maximum(dist, 0)
    max_exact = N_BUCKETS // 2
    nf = jnp.maximum(n, 1).astype(jnp.float32)
    large = max_exact + (jnp.log(nf / max_exact) / math.log(MAX_DISTANCE / max_exact)
                         * (N_BUCKETS - max_exact)).astype(jnp.int32)
    return jnp.where(n < max_exact, n, jnp.minimum(large, N_BUCKETS - 1))


def gather_pages(pool, page_table):
    rows = pool[page_table]
    return rows.reshape((page_table.shape[0], page_table.shape[1] * pool.shape[1]) + pool.shape[2:])


def causal_dwconv(x, prev, w):
    width, t = w.shape[0], x.shape[1]
    full = jnp.concatenate([prev, x], axis=1)
    y = full[:, 0:t] * w[0]
    for j in range(1, width):
        y = y + full[:, j:j + t] * w[j]
    return y, full[:, t:]


def pool_mix(u, prev, pos, w_pool, pool_scale):
    b, t, c = u.shape
    full = jnp.concatenate([prev, u], axis=1)
    ff = full.astype(jnp.float32)
    cs = jnp.concatenate([jnp.zeros((b, 1, c), jnp.float32), jnp.cumsum(ff, axis=1)], axis=1)
    lo = POOL_STATE + 1
    means = []
    for g, w in enumerate(POOL_WINDOWS):
        sl = slice(g * POOL_GROUP_DIM, (g + 1) * POOL_GROUP_DIM)
        win_sum = cs[:, lo:lo + t, sl] - cs[:, lo - w:lo - w + t, sl]
        count = jnp.minimum(w, pos + 1).astype(jnp.float32)[None, :, None]
        means.append(win_sum / count)
    d = (jnp.concatenate(means, axis=-1) - ff[:, POOL_STATE:]).reshape(b, t, POOL_GROUPS, POOL_GROUP_DIM)
    y = jnp.einsum('btgc,gcd->btgd', d, w_pool).reshape(b, t, c) * pool_scale
    return y.astype(u.dtype), full[:, t:]


def compress(rows, w_pos, w_lin):
    length = rows.shape[1]
    n_cmp = (length - CMP_BLOCK) // CMP_STRIDE + 1
    idx = jnp.arange(n_cmp)[:, None] * CMP_STRIDE + jnp.arange(CMP_BLOCK)[None, :]
    blocks = rows[:, idx]
    pooled = jnp.einsum('bnlhd,lhd->bnhd', blocks, w_pos)
    return jnp.einsum('bnhd,hde->bnhe', pooled, w_lin)


def nsa_mixer(q, gates, kc_rows, vc_rows, ks_rows, vs_rows, win_rows, pos_start, n_prev_win,
              w_cmp_pos, w_cmp_lin, rel_bias):
    b, t = q.shape[0], q.shape[1]
    length = kc_rows.shape[1]
    scale = HEAD_DIM ** -0.5
    kc = compress(kc_rows, w_cmp_pos[0], w_cmp_lin[0])
    vc = compress(vc_rows, w_cmp_pos[1], w_cmp_lin[1])
    n_cmp = kc.shape[1]
    c_end = jnp.arange(n_cmp) * CMP_STRIDE + CMP_BLOCK - 1
    c_start = c_end - (CMP_BLOCK - 1)
    n_slc = -(-length // SEL_BLOCK)
    k_top = min(N_SEL, n_slc)
    s_start = jnp.arange(n_slc) * SEL_BLOCK
    overlap = ((c_start[:, None] < s_start[None, :] + SEL_BLOCK)
               & (c_end[:, None] >= s_start[None, :])).astype(jnp.float32)
    ks_t = ks_rows.transpose(0, 2, 1, 3)
    vs_t = vs_rows.transpose(0, 2, 1, 3)
    win = jnp.pad(win_rows, ((0, 0), (WINDOW - n_prev_win, 0), (0, 0), (0, 0), (0, 0)))
    b_idx = jnp.arange(b)[:, None, None, None]
    h_idx = jnp.arange(NSA_KV_HEADS)[None, :, None, None]
    bias_by_head = rel_bias.reshape(N_BUCKETS, NSA_KV_HEADS, NSA_GROUP).transpose(1, 0, 2)
    qb = min(Q_BLOCK, t)
    n_blk = t // qb

    def dense_bias(dist):
        return jnp.moveaxis(bias_by_head[:, rel_bucket(dist)], -1, 1)

    def block(args):
        q_i, g_i, i = args
        start = i * qb
        pos_q = pos_start + start + jnp.arange(qb)
        lc = jnp.einsum('bthgd,bnhd->bhgtn', q_i, kc).astype(jnp.float32) * scale \
            + dense_bias(pos_q[:, None] - c_end[None, :])
        pc = masked_softmax(lc, c_end[None, :] <= pos_q[:, None])
        o_c = jnp.einsum('bhgtn,bnhd->bthgd', pc.astype(vc.dtype), vc)
        imp = jnp.einsum('bhgtn,ns->bhts', pc, overlap)
        blk = jnp.arange(n_slc)[None, :]
        cur = (pos_q // SEL_BLOCK)[:, None]
        forced = (blk == 0) | (blk == cur) | (blk == cur - 1)
        score = jnp.where(blk > cur, NEG_INF, jnp.where(forced, BIG, imp))
        top_val, top_idx = lax.top_k(score, k_top)
        tok5 = top_idx[..., None] * SEL_BLOCK + jnp.arange(SEL_BLOCK)
        ok5 = (top_val[..., None] > 0.5 * NEG_INF) & (tok5 <= pos_q[:, None, None])
        tok = tok5.reshape(b, NSA_KV_HEADS, qb, k_top * SEL_BLOCK)
        ok = ok5.reshape(b, NSA_KV_HEADS, qb, k_top * SEL_BLOCK)
        tok_c = jnp.minimum(tok, length - 1)
        k_sel = ks_t[b_idx, h_idx, tok_c]
        v_sel = vs_t[b_idx, h_idx, tok_c]
        bias_s = jnp.moveaxis(bias_by_head[h_idx, rel_bucket(pos_q[:, None] - tok)], -1, 2)
        ls = jnp.einsum('bthgd,bhtsd->bhgts', q_i, k_sel).astype(jnp.float32) * scale + bias_s
        ps = masked_softmax(ls, ok[:, :, None])
        o_s = jnp.einsum('bhgts,bhtsd->bthgd', ps.astype(v_sel.dtype), v_sel)
        w_i = lax.dynamic_slice_in_dim(win, start, WINDOW + qb, axis=1)
        pos_w = pos_start - WINDOW + start + jnp.arange(WINDOW + qb)
        dist = pos_q[:, None] - pos_w[None, :]
        mask_w = (dist >= 0) & (dist <= WINDOW) & (pos_w[None, :] >= 0)
        lw = jnp.einsum('bthgd,bshd->bhgts', q_i, w_i[:, :, 0]).astype(jnp.float32) * scale + dense_bias(dist)
        pw = masked_softmax(lw, mask_w)
        o_w = jnp.einsum('bhgts,bshd->bthgd', pw.astype(w_i.dtype), w_i[:, :, 1])
        return g_i[..., 0:1] * o_c + g_i[..., 1:2] * o_s + g_i[..., 2:3] * o_w

    q_blocks = jnp.moveaxis(q.reshape(b, n_blk, qb, NSA_KV_HEADS, NSA_GROUP, HEAD_DIM), 1, 0)
    g_blocks = jnp.moveaxis(gates.reshape(b, n_blk, qb, NSA_KV_HEADS, NSA_GROUP, 3), 1, 0)
    out = lax.map(block, (q_blocks, g_blocks, jnp.arange(n_blk)))
    return jnp.moveaxis(out, 0, 1).reshape(b, t, NSA_HEADS * HEAD_DIM)


def stick_breaking(q, k, v, pos_start):
    b, t = q.shape[0], q.shape[1]
    length = k.shape[1]
    pos_k = jnp.arange(length)
    qb = min(Q_BLOCK, t)
    n_blk = t // qb

    def block(args):
        q_i, i = args
        pos_q = pos_start + i * qb + jnp.arange(qb)
        z = jnp.einsum('bthd,bshd->bhts', q_i, k).astype(jnp.float32) * HEAD_DIM ** -0.5
        before = pos_k[None, :] < pos_q[:, None]
        log_keep = jnp.where(before, jax.nn.log_sigmoid(-z), 0.0)
        later = lax.cumsum(log_keep, axis=3, reverse=True) - log_keep
        a = jnp.where(before, jnp.exp(jax.nn.log_sigmoid(z) + later), 0.0)
        return jnp.einsum('bhts,bshd->bthd', a.astype(v.dtype), v)

    q_blocks = jnp.moveaxis(q.reshape(b, n_blk, qb, SB_HEADS, HEAD_DIM), 1, 0)
    out = lax.map(block, (q_blocks, jnp.arange(n_blk)))
    return jnp.moveaxis(out, 0, 1).reshape(b, t, SB_HEADS * HEAD_DIM)


def even_mixer(x, pos_start, pool_prev, cmp_k_past, cmp_v_past, sel_k_past, sel_v_past, win_prev,
               w_in, w_pool, pool_scale, w_cmp_pos, w_cmp_lin, rel_bias, w_out):
    b, t, _ = x.shape
    u, q, kc, vc, ksl, vsl, kvw, g = split_cols(x @ w_in, EVEN_SPLITS)
    pos = pos_start + jnp.arange(t)
    y_pool, pool_new = pool_mix(u, pool_prev, pos, w_pool, pool_scale)
    kv_shape = (b, t, NSA_KV_HEADS, HEAD_DIM)
    kc, vc, ksl, vsl = kc.reshape(kv_shape), vc.reshape(kv_shape), ksl.reshape(kv_shape), vsl.reshape(kv_shape)
    win_rows = jnp.concatenate([win_prev, kvw.reshape(b, t, 2, NSA_KV_HEADS, HEAD_DIM)], axis=1)
    y_nsa = nsa_mixer(q.reshape(b, t, NSA_KV_HEADS, NSA_GROUP, HEAD_DIM),
                      jax.nn.sigmoid(g).reshape(b, t, NSA_KV_HEADS, NSA_GROUP, 3),
                      jnp.concatenate([cmp_k_past, kc], axis=1), jnp.concatenate([cmp_v_past, vc], axis=1),
                      jnp.concatenate([sel_k_past, ksl], axis=1), jnp.concatenate([sel_v_past, vsl], axis=1),
                      win_rows, pos_start, win_prev.shape[1], w_cmp_pos, w_cmp_lin, rel_bias)
    y = jnp.concatenate([y_pool, y_nsa], axis=-1) @ w_out
    return y, pool_new, kc, vc, ksl, vsl, win_rows[:, -WINDOW:]


def odd_mixer(x, pos_start, sc_prev, sb_k_past, sb_v_past, w_in, sc_conv, w_out):
    b, t, _ = x.shape
    xin, b_gate, c_gate, q, k, v = split_cols(x @ w_in, ODD_SPLITS)
    conv_out, sc_new = causal_dwconv(c_gate * xin, sc_prev, sc_conv)
    y_sc = b_gate * conv_out
    hs = (b, t, SB_HEADS, HEAD_DIM)
    k, v = k.reshape(hs), v.reshape(hs)
    y_sb = stick_breaking(q.reshape(hs), jnp.concatenate([sb_k_past, k], axis=1),
                          jnp.concatenate([sb_v_past, v], axis=1), pos_start)
    y = jnp.concatenate([y_sc, y_sb], axis=-1) @ w_out
    return y, sc_new, k, v


def mem_kv_proj(mem, w_ckv):
    return (mem @ w_ckv).reshape(mem.shape[0], N_MEM, 2, MEM_HEADS, MEM_HEAD_DIM)


def cross_attend(x, mem_kv, w_cq, w_co):
    b, t, _ = x.shape
    q = (x @ w_cq).reshape(b, t, MEM_HEADS, MEM_HEAD_DIM)
    s = jnp.einsum('bthd,bmhd->bhtm', q, mem_kv[:, :, 0]).astype(jnp.float32) * MEM_HEAD_DIM ** -0.5
    p = jax.nn.softmax(s, axis=-1).astype(mem_kv.dtype)
    o = jnp.einsum('bhtm,bmhd->bthd', p, mem_kv[:, :, 1]).reshape(b, t, MEM_WIDTH)
    return o @ w_co


def conv_ffn(x, prev, w_up, conv_w, w_down):
    h, new_prev = causal_dwconv(x @ w_up, prev, conv_w)
    return (jax.nn.silu(h[..., :D_FF]) * h[..., D_FF:]) @ w_down, new_prev


def run_group(x, pos_start, st, p):
    new = {'pool': [], 'cmp_k': [], 'cmp_v': [], 'sel_k': [], 'sel_v': [], 'win': [],
           'sc': [], 'sb_k': [], 'sb_v': [], 'ffn': []}
    for l in range(DEPTH):
        e = l // 2
        if l % 2 == 0:
            y, pool_n, ck, cv, sk, sv, win_n = even_mixer(
                x, pos_start, st['pool'][e], st['cmp_k'][e], st['cmp_v'][e], st['sel_k'][e], st['sel_v'][e],
                st['win'][e], p['w_in_even'][e], p['w_pool'][e], p['pool_scale'][e], p['w_cmp_pos'][e],
                p['w_cmp_lin'][e], p['rel_bias'], p['w_out_even'][e])
            new['pool'].append(pool_n)
            new['cmp_k'].append(ck)
            new['cmp_v'].append(cv)
            new['sel_k'].append(sk)
            new['sel_v'].append(sv)
            new['win'].append(win_n)
        else:
            y, sc_n, bk, bv = odd_mixer(x, pos_start, st['sc'][e], st['sb_k'][e], st['sb_v'][e],
                                        p['w_in_odd'][e], p['sc_conv'][e], p['w_out_odd'][e])
            new['sc'].append(sc_n)
            new['sb_k'].append(bk)
            new['sb_v'].append(bv)
        x = layer_norm(ALPHA * x + y, p['ln_g'][l, 0], p['ln_b'][l, 0])
        x = layer_norm(ALPHA * x + cross_attend(x, st['mem'][l], p['w_cq'][l], p['w_co'][l]),
                       p['ln_g'][l, 1], p['ln_b'][l, 1])
        f, ffn_n = conv_ffn(x, st['ffn'][l], p['w_up'][l], p['ffn_conv'][l], p['w_down'][l])
        x = layer_norm(ALPHA * x + f, p['ln_g'][l, 2], p['ln_b'][l, 2])
        new['ffn'].append(ffn_n)
    return x, {name: jnp.stack(v) for name, v in new.items()}


def setup_inputs(seed: int = 0) -> dict:
    key = jax.random.key(seed)
    ks = iter(jax.random.split(key, 40))

    def nrm(shape, scale):
        return jax.random.normal(next(ks), shape, jnp.float32) * scale

    n_pages = PAST_LEN // PAGE_SIZE
    n_pool = (DEC_BATCH * n_pages * 5) // 4
    win_buf = min(WINDOW, PAST_LEN)
    d_in_even = sum(EVEN_SPLITS)
    d_in_odd = sum(ODD_SPLITS)
    return {
        'x_prompt': nrm((BATCH, SEQ, D_MODEL), 1.0),
        'x_sample': nrm((DEC_BATCH, DEC_SEQ, D_MODEL), 1.0),
        'mem_prompt': nrm((BATCH, N_MEM, D_MODEL), 1.0),
        'state_pool': nrm((N_EVEN, DEC_BATCH, POOL_STATE, POOL_WIDTH), 1.0),
        'cache_nsa_cmp_k': nrm((N_EVEN, n_pool, PAGE_SIZE, NSA_KV_HEADS, HEAD_DIM), 1.0),
        'cache_nsa_cmp_v': nrm((N_EVEN, n_pool, PAGE_SIZE, NSA_KV_HEADS, HEAD_DIM), 1.0),
        'cache_nsa_sel_k': nrm((N_EVEN, n_pool, PAGE_SIZE, NSA_KV_HEADS, HEAD_DIM), 1.0),
        'cache_nsa_sel_v': nrm((N_EVEN, n_pool, PAGE_SIZE, NSA_KV_HEADS, HEAD_DIM), 1.0),
        'state_nsa_win': nrm((N_EVEN, DEC_BATCH, win_buf, 2, NSA_KV_HEADS, HEAD_DIM), 1.0),
        'state_sc': nrm((N_ODD, DEC_BATCH, SC_CONV - 1, SC_WIDTH), 1.0),
        'cache_sb_k': nrm((N_ODD, n_pool, PAGE_SIZE, SB_HEADS, HEAD_DIM), 1.0),
        'cache_sb_v': nrm((N_ODD, n_pool, PAGE_SIZE, SB_HEADS, HEAD_DIM), 1.0),
        'state_ffn': nrm((DEPTH, DEC_BATCH, FFN_CONV - 1, 2 * D_FF), 1.0),
        'cache_mem': nrm((DEPTH, DEC_BATCH, N_MEM, 2, MEM_HEADS, MEM_HEAD_DIM), 1.0),
        'page_table': jax.random.permutation(next(ks), n_pool)[:DEC_BATCH * n_pages]
                          .reshape(DEC_BATCH, n_pages).astype(jnp.int32),
        'w_in_even': nrm((N_EVEN, D_MODEL, d_in_even), D_MODEL ** -0.5),
        'w_pool': nrm((N_EVEN, POOL_GROUPS, POOL_GROUP_DIM, POOL_GROUP_DIM), POOL_GROUP_DIM ** -0.5),
        'pool_scale': 1.0 + nrm((N_EVEN, POOL_WIDTH), 0.02),
        'w_cmp_pos': 1.0 / CMP_BLOCK + nrm((N_EVEN, 2, CMP_BLOCK, NSA_KV_HEADS, HEAD_DIM), CMP_BLOCK ** -0.5),
        'w_cmp_lin': nrm((N_EVEN, 2, NSA_KV_HEADS, HEAD_DIM, HEAD_DIM), HEAD_DIM ** -0.5),
        'rel_bias': nrm((N_BUCKETS, NSA_HEADS), 0.5),
        'w_out_even': nrm((N_EVEN, MIX_EVEN, D_MODEL), BETA * MIX_EVEN ** -0.5),
        'w_in_odd': nrm((N_ODD, D_MODEL, d_in_odd), D_MODEL ** -0.5),
        'sc_conv': nrm((N_ODD, SC_CONV, SC_WIDTH), SC_CONV ** -0.5),
        'w_out_odd': nrm((N_ODD, MIX_ODD, D_MODEL), BETA * MIX_ODD ** -0.5),
        'w_cq': nrm((DEPTH, D_MODEL, MEM_WIDTH), D_MODEL ** -0.5),
        'w_ckv': nrm((DEPTH, D_MODEL, 2 * MEM_WIDTH), D_MODEL ** -0.5),
        'w_co': nrm((DEPTH, MEM_WIDTH, D_MODEL), BETA * MEM_WIDTH ** -0.5),
        'w_up': nrm((DEPTH, D_MODEL, 2 * D_FF), D_MODEL ** -0.5),
        'ffn_conv': nrm((DEPTH, FFN_CONV, 2 * D_FF), FFN_CONV ** -0.5),
        'w_down': nrm((DEPTH, D_FF, D_MODEL), BETA * D_FF ** -0.5),
        'ln_g': 1.0 + nrm((DEPTH, 3, D_MODEL), 0.02),
        'ln_b': nrm((DEPTH, 3, D_MODEL), 0.02),
    }


def reference(x_prompt, x_sample, mem_prompt, state_pool, cache_nsa_cmp_k, cache_nsa_cmp_v,
              cache_nsa_sel_k, cache_nsa_sel_v, state_nsa_win, state_sc, cache_sb_k, cache_sb_v,
              state_ffn, cache_mem, page_table, w_in_even, w_pool, pool_scale, w_cmp_pos, w_cmp_lin,
              rel_bias, w_out_even, w_in_odd, sc_conv, w_out_odd, w_cq, w_ckv, w_co, w_up, ffn_conv,
              w_down, ln_g, ln_b):
    p = {'w_in_even': w_in_even, 'w_pool': w_pool, 'pool_scale': pool_scale, 'w_cmp_pos': w_cmp_pos,
         'w_cmp_lin': w_cmp_lin, 'rel_bias': rel_bias, 'w_out_even': w_out_even, 'w_in_odd': w_in_odd,
         'sc_conv': sc_conv, 'w_out_odd': w_out_odd, 'w_cq': w_cq, 'w_co': w_co, 'w_up': w_up,
         'ffn_conv': ffn_conv, 'w_down': w_down, 'ln_g': ln_g, 'ln_b': ln_b}
    bp = x_prompt.shape[0]
    dt = x_prompt.dtype
    past_len = page_table.shape[1] * PAGE_SIZE

    p_mem = jnp.stack([mem_kv_proj(mem_prompt, w_ckv[l]) for l in range(DEPTH)])
    nsa_empty = jnp.zeros((bp, 0, NSA_KV_HEADS, HEAD_DIM), dt)
    sb_empty = jnp.zeros((bp, 0, SB_HEADS, HEAD_DIM), dt)
    st_prompt = {
        'pool': [jnp.zeros((bp, POOL_STATE, POOL_WIDTH), dt)] * N_EVEN,
        'cmp_k': [nsa_empty] * N_EVEN, 'cmp_v': [nsa_empty] * N_EVEN,
        'sel_k': [nsa_empty] * N_EVEN, 'sel_v': [nsa_empty] * N_EVEN,
        'win': [jnp.zeros((bp, 0, 2, NSA_KV_HEADS, HEAD_DIM), dt)] * N_EVEN,
        'sc': [jnp.zeros((bp, SC_CONV - 1, SC_WIDTH), dt)] * N_ODD,
        'sb_k': [sb_empty] * N_ODD, 'sb_v': [sb_empty] * N_ODD,
        'ffn': [jnp.zeros((bp, FFN_CONV - 1, 2 * D_FF), dt)] * DEPTH,
        'mem': p_mem,
    }
    st_sample = {
        'pool': state_pool,
        'cmp_k': [gather_pages(cache_nsa_cmp_k[e], page_table) for e in range(N_EVEN)],
        'cmp_v': [gather_pages(cache_nsa_cmp_v[e], page_table) for e in range(N_EVEN)],
        'sel_k': [gather_pages(cache_nsa_sel_k[e], page_table) for e in range(N_EVEN)],
        'sel_v': [gather_pages(cache_nsa_sel_v[e], page_table) for e in range(N_EVEN)],
        'win': state_nsa_win,
        'sc': state_sc,
        'sb_k': [gather_pages(cache_sb_k[o], page_table) for o in range(N_ODD)],
        'sb_v': [gather_pages(cache_sb_v[o], page_table) for o in range(N_ODD)],
        'ffn': state_ffn,
        'mem': cache_mem,
    }
    y_prompt, new_p = run_group(x_prompt, 0, st_prompt, p)
    y_sample, new_s = run_group(x_sample, past_len, st_sample, p)
    return (y_prompt, y_sample,
            new_p['pool'], new_p['cmp_k'], new_p['cmp_v'], new_p['sel_k'], new_p['sel_v'], new_p['win'],
            new_p['sc'], new_p['sb_k'], new_p['sb_v'], new_p['ffn'], p_mem,
            new_s['pool'], new_s['cmp_k'], new_s['cmp_v'], new_s['sel_k'], new_s['sel_v'], new_s['win'],
            new_s['sc'], new_s['sb_k'], new_s['sb_v'], new_s['ffn'])
```

```python
import functools
import math

import jax
import jax.numpy as jnp
import numpy as np
from jax import lax
from jax.experimental import pallas as pl
from jax.experimental.pallas import tpu as pltpu

F32 = jnp.float32
BF16 = jnp.bfloat16

HEAD_DIM = 128
PAGE_SIZE = 128
POOL_WINDOWS = (2, 4, 8, 16)
POOL_GROUP_DIM = 256
POOL_STATE = 15
NSA_KV_HEADS = 2
NSA_GROUP = 12
NSA_HEADS = 24
CMP_BLOCK = 32
CMP_STRIDE = 16
SEL_BLOCK = 64
SEL_SHIFT = 6
N_SEL = 16
WINDOW = 512
N_BUCKETS = 32
MAX_DISTANCE = 128
MEM_HEADS = 4
MEM_HEAD_DIM = 256
LN_EPS = 1e-5
NEG_INF = -1e30
BIG = 1e30

LANES = 128
VMEM_LIMIT = 56 * 1024 * 1024

E_Q, E_U, E_KC, E_VC, E_KS, E_VS, E_KVW, E_G, E_TOT = 0, 3072, 4096, 4352, 4608, 4864, 5120, 5632, 5760
O_X, O_B, O_C, O_Q, O_K, O_V = 0, 1024, 2048, 3072, 6144, 9216


def _cp(*sem):
    return pltpu.CompilerParams(dimension_semantics=sem, vmem_limit_bytes=VMEM_LIMIT)


def _nt(a, b):
    return lax.dot_general(a, b, (((1,), (1,)), ((), ())), preferred_element_type=F32)


def _split3(x):
    hi = x.astype(BF16)
    r1 = x - hi.astype(F32)
    mid = r1.astype(BF16)
    lo = (r1 - mid.astype(F32)).astype(BF16)
    return hi, mid, lo


def _mm_kernel(x_ref, w_ref, o_ref):
    o_ref[...] = jnp.dot(x_ref[...], w_ref[...], preferred_element_type=F32).astype(o_ref.dtype)


def _mm(x, w, out_dtype, tm, tn):
    m, k = x.shape
    n = w.shape[1]
    tm = min(tm, m)
    return pl.pallas_call(
        _mm_kernel,
        grid=(m // tm, n // tn),
        in_specs=[pl.BlockSpec((tm, k), lambda i, j: (i, 0)),
                  pl.BlockSpec((k, tn), lambda i, j: (0, j))],
        out_specs=pl.BlockSpec((tm, tn), lambda i, j: (i, j)),
        out_shape=jax.ShapeDtypeStruct((m, n), out_dtype),
        compiler_params=_cp("parallel", "parallel"),
    )(x, w)


def _mm_acc_kernel(x_ref, w_ref, o_ref, acc_ref):
    k = pl.program_id(2)

    @pl.when(k == 0)
    def _():
        acc_ref[...] = jnp.zeros_like(acc_ref)

    acc_ref[...] += jnp.dot(x_ref[...], w_ref[...], preferred_element_type=F32)

    @pl.when(k == pl.num_programs(2) - 1)
    def _():
        o_ref[...] = acc_ref[...].astype(o_ref.dtype)


def _mm_acc(x, w, out_dtype, tm, tn, tk):
    m, k = x.shape
    n = w.shape[1]
    tm = min(tm, m)
    return pl.pallas_call(
        _mm_acc_kernel,
        grid=(m // tm, n // tn, k // tk),
        in_specs=[pl.BlockSpec((tm, tk), lambda i, j, kk: (i, kk)),
                  pl.BlockSpec((tk, tn), lambda i, j, kk: (kk, j))],
        out_specs=pl.BlockSpec((tm, tn), lambda i, j, kk: (i, j)),
        out_shape=jax.ShapeDtypeStruct((m, n), out_dtype),
        scratch_shapes=[pltpu.VMEM((tm, tn), F32)],
        compiler_params=_cp("parallel", "parallel", "arbitrary"),
    )(x, w)


def _ln_kernel(x_ref, y_ref, g_ref, b_ref, of_ref, ob_ref, *, alpha):
    z = alpha * x_ref[...] + y_ref[...]
    mu = jnp.mean(z, axis=-1, keepdims=True)
    zc = z - mu
    var = jnp.mean(zc * zc, axis=-1, keepdims=True)
    out = zc * lax.rsqrt(var + LN_EPS) * g_ref[...] + b_ref[...]
    of_ref[...] = out
    ob_ref[...] = out.astype(BF16)


def _add_ln(x, y, g, b, alpha, tm=256):
    m, d = x.shape
    tm = min(tm, m)
    row = pl.BlockSpec((tm, d), lambda i: (i, 0))
    vec = pl.BlockSpec((1, d), lambda i: (0, 0))
    return pl.pallas_call(
        functools.partial(_ln_kernel, alpha=alpha),
        grid=(m // tm,),
        in_specs=[row, row, vec, vec],
        out_specs=[row, row],
        out_shape=[jax.ShapeDtypeStruct((m, d), F32), jax.ShapeDtypeStruct((m, d), BF16)],
        compiler_params=_cp("parallel"),
    )(x, y, g.reshape(1, d), b.reshape(1, d))


def _cross_kernel(q_ref, kv_ref, o_ref):
    width = MEM_HEADS * MEM_HEAD_DIM
    scale = MEM_HEAD_DIM ** -0.5
    for h in range(MEM_HEADS):
        sl = slice(h * MEM_HEAD_DIM, (h + 1) * MEM_HEAD_DIM)
        q = q_ref[:, sl].astype(BF16)
        k = kv_ref[:, sl].astype(BF16)
        v = kv_ref[:, width + h * MEM_HEAD_DIM: width + (h + 1) * MEM_HEAD_DIM].astype(BF16)
        s = _nt(q, k) * scale
        s = s - jnp.max(s, axis=-1, keepdims=True)
        e = jnp.exp(s)
        p = e / jnp.sum(e, axis=-1, keepdims=True)
        o_ref[:, sl] = jnp.dot(p.astype(BF16), v, preferred_element_type=F32).astype(o_ref.dtype)


def _cross(q, kv, tq):
    b, t, w = q.shape
    tq = min(tq, t)
    return pl.pallas_call(
        _cross_kernel,
        grid=(b, t // tq),
        in_specs=[pl.BlockSpec((None, tq, w), lambda i, j: (i, j, 0)),
                  pl.BlockSpec((None, kv.shape[1], kv.shape[2]), lambda i, j: (i, 0, 0))],
        out_specs=pl.BlockSpec((None, tq, w), lambda i, j: (i, j, 0)),
        out_shape=jax.ShapeDtypeStruct((b, t, w), BF16),
        compiler_params=_cp("parallel", "parallel"),
    )(q, kv)


def _shift_rows(h, c_last2, row):
    s1 = jnp.where(row == 0, c_last2[7:8], pltpu.roll(h, 1, 0))
    s2 = jnp.where(row == 0, c_last2[6:7], jnp.where(row == 1, c_last2[7:8], pltpu.roll(h, 2, 0)))
    return s1, s2


def _ffn_up_kernel(x_ref, w1_ref, w2_ref, cw1_ref, cw2_ref, act_ref, st1_ref, st2_ref, carry_ref,
                   *, tiles_per_batch):
    m = pl.program_id(0)
    n = pl.program_id(1)
    tm = x_ref.shape[0]
    tn = w1_ref.shape[1]
    x = x_ref[...]
    h1 = jnp.dot(x, w1_ref[...], preferred_element_type=F32)
    h2 = jnp.dot(x, w2_ref[...], preferred_element_type=F32)

    @pl.when(m % tiles_per_batch == 0)
    def _():
        carry_ref[n] = jnp.zeros((8, 2 * tn), F32)

    prev = carry_ref[n]
    row = lax.broadcasted_iota(jnp.int32, (tm, 1), 0)

    def conv(h, p, cw):
        s1, s2 = _shift_rows(h, p, row)
        return s2 * cw[0:1] + s1 * cw[1:2] + h * cw[2:3]

    c1 = conv(h1, prev[:, :tn], cw1_ref[...])
    c2 = conv(h2, prev[:, tn:], cw2_ref[...])
    act_ref[...] = (c1 * jax.nn.sigmoid(c1) * c2).astype(BF16)
    l1 = h1[tm - 8:]
    l2 = h2[tm - 8:]
    carry_ref[n] = jnp.concatenate([l1, l2], axis=1)
    st1_ref[...] = l1
    st2_ref[...] = l2


def _ffn_up(x, w_up, conv_w, t, tm, tn=256):
    m, d = x.shape
    dff = w_up.shape[1] // 2
    nt = dff // tn
    tm = min(tm, t)
    tpb = t // tm
    bsz = m // t
    kern = functools.partial(_ffn_up_kernel, tiles_per_batch=tpb)
    return pl.pallas_call(
        kern,
        grid=(m // tm, nt),
        in_specs=[pl.BlockSpec((tm, d), lambda i, j: (i, 0)),
                  pl.BlockSpec((d, tn), lambda i, j: (0, j)),
                  pl.BlockSpec((d, tn), lambda i, j: (0, j + nt)),
                  pl.BlockSpec((3, tn), lambda i, j: (0, j)),
                  pl.BlockSpec((3, tn), lambda i, j: (0, j + nt))],
        out_specs=[pl.BlockSpec((tm, tn), lambda i, j: (i, j)),
                   pl.BlockSpec((None, 8, tn), lambda i, j: (i // tpb, 0, j)),
                   pl.BlockSpec((None, 8, tn), lambda i, j: (i // tpb, 0, j))],
        out_shape=[jax.ShapeDtypeStruct((m, dff), BF16),
                   jax.ShapeDtypeStruct((bsz, 8, dff), F32),
                   jax.ShapeDtypeStruct((bsz, 8, dff), F32)],
        scratch_shapes=[pltpu.VMEM((nt, 8, 2 * tn), F32)],
        compiler_params=_cp("arbitrary", "arbitrary"),
    )(x, w_up, w_up, conv_w, conv_w)


def _ffn_up_dec_kernel(x_ref, w1_ref, w2_ref, cw1_ref, cw2_ref, p1_ref, p2_ref, act_ref, n1_ref, n2_ref):
    x = x_ref[...]
    h1 = jnp.dot(x, w1_ref[...], preferred_element_type=F32)
    h2 = jnp.dot(x, w2_ref[...], preferred_element_type=F32)

    def conv(h, p_ref, cw):
        return p_ref[:, 0, :] * cw[0:1] + p_ref[:, 1, :] * cw[1:2] + h * cw[2:3]

    c1 = conv(h1, p1_ref, cw1_ref[...])
    c2 = conv(h2, p2_ref, cw2_ref[...])
    act_ref[...] = (c1 * jax.nn.sigmoid(c1) * c2).astype(BF16)
    n1_ref[:, 0, :] = p1_ref[:, 1, :]
    n1_ref[:, 1, :] = h1
    n2_ref[:, 0, :] = p2_ref[:, 1, :]
    n2_ref[:, 1, :] = h2


def _ffn_up_dec(x, w_up, conv_w, prev, tn=256):
    b, d = x.shape
    dff = w_up.shape[1] // 2
    nt = dff // tn
    pspec1 = pl.BlockSpec((b, 2, tn), lambda j: (0, 0, j))
    pspec2 = pl.BlockSpec((b, 2, tn), lambda j: (0, 0, j + nt))
    act, n1, n2 = pl.pallas_call(
        _ffn_up_dec_kernel,
        grid=(nt,),
        in_specs=[pl.BlockSpec((b, d), lambda j: (0, 0)),
                  pl.BlockSpec((d, tn), lambda j: (0, j)),
                  pl.BlockSpec((d, tn), lambda j: (0, j + nt)),
                  pl.BlockSpec((3, tn), lambda j: (0, j)),
                  pl.BlockSpec((3, tn), lambda j: (0, j + nt)),
                  pspec1, pspec2],
        out_specs=[pl.BlockSpec((b, tn), lambda j: (0, j)),
                   pl.BlockSpec((b, 2, tn), lambda j: (0, 0, j)),
                   pl.BlockSpec((b, 2, tn), lambda j: (0, 0, j))],
        out_shape=[jax.ShapeDtypeStruct((b, dff), BF16),
                   jax.ShapeDtypeStruct((b, 2, dff), F32),
                   jax.ShapeDtypeStruct((b, 2, dff), F32)],
        compiler_params=_cp("parallel"),
    )(x, w_up, w_up, conv_w, conv_w, prev, prev)
    return act, jnp.concatenate([n1, n2], axis=-1)


def _bias_kernel(rb_ref, dist_ref, o_ref):
    head = pl.program_id(0)
    n = jnp.maximum(dist_ref[...], 0)
    max_exact = N_BUCKETS // 2
    nf = jnp.maximum(n, 1).astype(F32)
    large = max_exact + (jnp.log(nf / max_exact) / math.log(MAX_DISTANCE / max_exact)
                         * (N_BUCKETS - max_exact)).astype(jnp.int32)
    bucket = jnp.where(n < max_exact, n, jnp.minimum(large, N_BUCKETS - 1))
    out = jnp.zeros(n.shape, F32)
    for k in range(N_BUCKETS):
        out = jnp.where(bucket == k, rb_ref[k, head], out)
    o_ref[...] = out


def _bias_lookup(rel_bias, dist, tr):
    r, c = dist.shape
    nh = rel_bias.shape[1]
    tr = min(tr, r)
    return pl.pallas_call(
        _bias_kernel,
        grid=(nh, r // tr),
        in_specs=[pl.BlockSpec(memory_space=pltpu.SMEM),
                  pl.BlockSpec((tr, c), lambda h, i: (i, 0))],
        out_specs=pl.BlockSpec((None, tr, c), lambda h, i: (h, i, 0)),
        out_shape=jax.ShapeDtypeStruct((nh, r, c), F32),
        compiler_params=_cp("parallel", "parallel"),
    )(rel_bias, dist)


def _pool_groups(sums, cur, pos, wp_ref, sc_ref, y_ref):
    for g, w in enumerate(POOL_WINDOWS):
        sl = slice(g * POOL_GROUP_DIM, (g + 1) * POOL_GROUP_DIM)
        cnt = jnp.minimum(w, pos + 1).astype(F32)
        d = sums[g] / cnt - cur[:, sl]
        y = jnp.dot(d.astype(BF16), wp_ref[g], preferred_element_type=F32) * sc_ref[:, sl]
        y_ref[:, sl] = y.astype(y_ref.dtype)


def _pool_kernel(prev_ref, cur_ref, wp_ref, sc_ref, y_ref):
    t = pl.program_id(1)
    tt = cur_ref.shape[0]
    cur = cur_ref[...]
    prev = jnp.where(t == 0, 0.0, prev_ref[...])
    ext = jnp.concatenate([prev, cur], axis=0)
    gd = POOL_GROUP_DIM
    s2 = ext + pltpu.roll(ext, 1, 0)
    x4 = s2[:, gd:]
    s4 = x4 + pltpu.roll(x4, 2, 0)
    x8 = s4[:, gd:]
    s8 = x8 + pltpu.roll(x8, 4, 0)
    x16 = s8[:, gd:]
    s16 = x16 + pltpu.roll(x16, 8, 0)
    sums = [s2[16:, :gd], s4[16:, :gd], s8[16:, :gd], s16[16:]]
    pos = t * tt + lax.broadcasted_iota(jnp.int32, (tt, 1), 0)
    _pool_groups(sums, cur, pos, wp_ref, sc_ref, y_ref)


def _pool_prompt(h, col_blk, w_pool, pool_scale, tt=256):
    b, t, _ = h.shape
    wdt = w_pool.shape[0] * POOL_GROUP_DIM
    tt = min(tt, t)
    r = tt // 16
    return pl.pallas_call(
        _pool_kernel,
        grid=(b, t // tt),
        in_specs=[pl.BlockSpec((None, 16, wdt), lambda i, j: (i, jnp.maximum(j * r - 1, 0), col_blk)),
                  pl.BlockSpec((None, tt, wdt), lambda i, j: (i, j, col_blk)),
                  pl.BlockSpec(w_pool.shape, lambda i, j: (0, 0, 0)),
                  pl.BlockSpec((1, wdt), lambda i, j: (0, 0))],
        out_specs=pl.BlockSpec((None, tt, wdt), lambda i, j: (i, j, 0)),
        out_shape=jax.ShapeDtypeStruct((b, t, wdt), BF16),
        compiler_params=_cp("parallel", "parallel"),
    )(h, h, w_pool, pool_scale.reshape(1, wdt))


def _pool_dec_kernel(sp_ref, u_ref, wp_ref, sc_ref, y_ref, ns_ref, *, pos):
    u = u_ref[...]
    gd = POOL_GROUP_DIM
    acc = u
    sums = []
    back = 1
    for g, w in enumerate(POOL_WINDOWS):
        while back < w:
            acc = acc + sp_ref[:, POOL_STATE - back, :]
            back += 1
        sums.append(acc[:, g * gd:(g + 1) * gd])
    posv = jnp.full((u.shape[0], 1), pos, jnp.int32)
    _pool_groups(sums, u, posv, wp_ref, sc_ref, y_ref)
    for j in range(POOL_STATE - 1):
        ns_ref[:, j, :] = sp_ref[:, j + 1, :]
    ns_ref[:, POOL_STATE - 1, :] = u


def _pool_dec(state, u, w_pool, pool_scale, pos):
    b, wdt = u.shape
    return pl.pallas_call(
        functools.partial(_pool_dec_kernel, pos=pos),
        out_shape=[jax.ShapeDtypeStruct((b, wdt), BF16), jax.ShapeDtypeStruct(state.shape, F32)],
        compiler_params=pltpu.CompilerParams(vmem_limit_bytes=VMEM_LIMIT),
    )(state, u, w_pool, pool_scale.reshape(1, wdt))


def _sconv_kernel(px_ref, pc_ref, x_ref, b_ref, c_ref, cw_ref, y_ref, st_ref):
    t = pl.program_id(1)
    tt = x_ref.shape[0]
    v = c_ref[...] * x_ref[...]
    pv = jnp.where(t == 0, 0.0, pc_ref[...] * px_ref[...])
    row = lax.broadcasted_iota(jnp.int32, (tt, 1), 0)
    s1, s2 = _shift_rows(v, pv, row)
    cw = cw_ref[...]
    conv = s2 * cw[0:1] + s1 * cw[1:2] + v * cw[2:3]
    y_ref[...] = (b_ref[...] * conv).astype(y_ref.dtype)
    st_ref[...] = v[tt - 8:]


def _sconv_prompt(h, conv_w, tt=256):
    b, t, _ = h.shape
    wdt = conv_w.shape[1]
    tt = min(tt, t)
    r = tt // 8

    def prev(blk):
        return pl.BlockSpec((None, 8, wdt), lambda i, j: (i, jnp.maximum(j * r - 1, 0), blk))

    def cur(blk):
        return pl.BlockSpec((None, tt, wdt), lambda i, j: (i, j, blk))

    return pl.pallas_call(
        _sconv_kernel,
        grid=(b, t // tt),
        in_specs=[prev(0), prev(2), cur(0), cur(1), cur(2), pl.BlockSpec((3, wdt), lambda i, j: (0, 0))],
        out_specs=[pl.BlockSpec((None, tt, wdt), lambda i, j: (i, j, 0)),
                   pl.BlockSpec((None, 8, wdt), lambda i, j: (i, 0, 0))],
        out_shape=[jax.ShapeDtypeStruct((b, t, wdt), BF16), jax.ShapeDtypeStruct((b, 8, wdt), F32)],
        compiler_params=_cp("parallel", "arbitrary"),
    )(h, h, h, h, h, conv_w)


def _sconv_dec_kernel(st_ref, x_ref, b_ref, c_ref, cw_ref, y_ref, ns_ref):
    v = c_ref[...] * x_ref[...]
    cw = cw_ref[...]
    conv = st_ref[:, 0, :] * cw[0:1] + st_ref[:, 1, :] * cw[1:2] + v * cw[2:3]
    y_ref[...] = (b_ref[...] * conv).astype(y_ref.dtype)
    ns_ref[:, 0, :] = st_ref[:, 1, :]
    ns_ref[:, 1, :] = v


def _sconv_dec(state, h, conv_w):
    b = h.shape[0]
    wdt = conv_w.shape[1]

    def col(blk):
        return pl.BlockSpec((b, wdt), lambda i: (0, blk))

    return pl.pallas_call(
        _sconv_dec_kernel,
        grid=(1,),
        in_specs=[pl.BlockSpec(state.shape, lambda i: (0, 0, 0)), col(0), col(1), col(2),
                  pl.BlockSpec((3, wdt), lambda i: (0, 0))],
        out_specs=[pl.BlockSpec((b, wdt), lambda i: (0, 0)), pl.BlockSpec(state.shape, lambda i: (0, 0, 0))],
        out_shape=[jax.ShapeDtypeStruct((b, wdt), BF16), jax.ShapeDtypeStruct(state.shape, F32)],
        compiler_params=_cp("arbitrary"),
    )(state, h, h, h, conv_w)


def _softplus_parts(z):
    lg = jnp.log1p(jnp.exp(-jnp.abs(z)))
    return jnp.minimum(z, 0.0) - lg, jnp.minimum(-z, 0.0) - lg


def _suffix_sums(lk, upper):
    r = lk.shape[0]
    pieces = jnp.concatenate(_split3(lk), axis=0)
    s = jnp.dot(pieces, upper, preferred_element_type=F32)
    return s[:r] + s[r:2 * r] + s[2 * r:]


def _sb_kernel(q_ref, k_ref, v_ref, o_ref, *, tq):
    i = pl.program_id(2)
    scale = HEAD_DIM ** -0.5
    q = q_ref[...].astype(BF16)
    rj = lax.broadcasted_iota(jnp.int32, (tq, tq), 0)
    cj = lax.broadcasted_iota(jnp.int32, (tq, tq), 1)
    upper = jnp.where(rj > cj, 1.0, 0.0).astype(BF16)

    def chunk(c, r_run, acc, diag):
        off = pl.multiple_of(c * tq, tq)
        k = k_ref[pl.ds(off, tq), :].astype(BF16)
        v = v_ref[pl.ds(off, tq), :].astype(BF16)
        z = _nt(q, k) * scale
        ls_pos, ls_neg = _softplus_parts(z)
        if diag:
            before = cj < rj
            lk = jnp.where(before, ls_neg, 0.0)
        else:
            lk = ls_neg
        later = _suffix_sums(lk, upper) + r_run
        a = jnp.exp(ls_pos + later)
        if diag:
            a = jnp.where(before, a, 0.0)
        acc = acc + jnp.dot(a.astype(BF16), v, preferred_element_type=F32)
        return r_run + jnp.sum(lk, axis=1, keepdims=True), acc

    r0 = jnp.zeros((tq, 1), F32)
    acc0 = jnp.zeros((tq, HEAD_DIM), F32)
    r1, acc1 = chunk(i, r0, acc0, True)

    def body(s, carry):
        return chunk(i - 1 - s, carry[0], carry[1], False)

    _, acc = lax.fori_loop(0, i, body, (r1, acc1))
    o_ref[...] = acc.astype(o_ref.dtype)


def _sb_prompt(h, tq=256):
    b, t, _ = h.shape
    nh = (O_K - O_Q) // HEAD_DIM
    tq = min(tq, t)
    qb, kb, vb = O_Q // HEAD_DIM, O_K // HEAD_DIM, O_V // HEAD_DIM
    return pl.pallas_call(
        functools.partial(_sb_kernel, tq=tq),
        grid=(b, nh, t // tq),
        in_specs=[pl.BlockSpec((None, tq, HEAD_DIM), lambda bi, hi, i: (bi, i, qb + hi)),
                  pl.BlockSpec((None, t, HEAD_DIM), lambda bi, hi, i: (bi, 0, kb + hi)),
                  pl.BlockSpec((None, t, HEAD_DIM), lambda bi, hi, i: (bi, 0, vb + hi))],
        out_specs=pl.BlockSpec((None, tq, HEAD_DIM), lambda bi, hi, i: (bi, i, hi)),
        out_shape=jax.ShapeDtypeStruct((b, t, nh * HEAD_DIM), BF16),
        compiler_params=_cp("parallel", "parallel", "parallel"),
    )(h, h, h)


def _sb_dec_kernel(pt_ref, q_ref, k_ref, v_ref, o_ref, r_sc, acc_sc):
    c = pl.program_id(1)
    scale = HEAD_DIM ** -0.5
    nh = q_ref.shape[0]
    ng = nh // 8

    @pl.when(c == 0)
    def _():
        r_sc[...] = jnp.zeros_like(r_sc)
        acc_sc[...] = jnp.zeros_like(acc_sc)

    rj = lax.broadcasted_iota(jnp.int32, (PAGE_SIZE, PAGE_SIZE), 0)
    cj = lax.broadcasted_iota(jnp.int32, (PAGE_SIZE, PAGE_SIZE), 1)
    upper = jnp.where(rj > cj, 1.0, 0.0).astype(BF16)
    row8 = lax.broadcasted_iota(jnp.int32, (8, HEAD_DIM), 0)

    zs = []
    for g in range(ng):
        qg = q_ref[8 * g:8 * g + 8, :]
        z = jnp.zeros((8, PAGE_SIZE), F32)
        for hh in range(8):
            kh = k_ref[:, 8 * g + hh, :].astype(BF16)
            lhs = jnp.where(row8 == hh, qg, 0.0).astype(BF16)
            z = z + _nt(lhs, kh)
        zs.append(z)
    z = jnp.concatenate(zs, axis=0) * scale
    ls_pos, lk = _softplus_parts(z)
    later = _suffix_sums(lk, upper) + r_sc[...]
    a = jnp.exp(ls_pos + later)
    r_sc[...] += jnp.sum(lk, axis=1, keepdims=True)
    for g in range(ng):
        ag = a[8 * g:8 * g + 8]
        acc = acc_sc[8 * g:8 * g + 8, :]
        for hh in range(8):
            vh = v_ref[:, 8 * g + hh, :].astype(BF16)
            lhs = jnp.where(row8 == hh, ag, 0.0).astype(BF16)
            acc = acc + jnp.dot(lhs, vh, preferred_element_type=F32)
        acc_sc[8 * g:8 * g + 8, :] = acc

    @pl.when(c == pl.num_programs(1) - 1)
    def _():
        o_ref[...] = acc_sc[...].astype(o_ref.dtype)


def _sb_dec(q, k_pool, v_pool, layer, page_table):
    b, nh, _ = q.shape
    n_pages = page_table.shape[1]

    def page(bi, c, pt):
        return (layer, pt[bi, n_pages - 1 - c], 0, 0, 0)

    grid_spec = pltpu.PrefetchScalarGridSpec(
        num_scalar_prefetch=1,
        grid=(b, n_pages),
        in_specs=[pl.BlockSpec((None, nh, HEAD_DIM), lambda bi, c, pt: (bi, 0, 0)),
                  pl.BlockSpec((None, None, PAGE_SIZE, nh, HEAD_DIM), page),
                  pl.BlockSpec((None, None, PAGE_SIZE, nh, HEAD_DIM), page)],
        out_specs=pl.BlockSpec((None, nh, HEAD_DIM), lambda bi, c, pt: (bi, 0, 0)),
        scratch_shapes=[pltpu.VMEM((nh, 1), F32), pltpu.VMEM((nh, HEAD_DIM), F32)],
    )
    return pl.pallas_call(
        _sb_dec_kernel,
        grid_spec=grid_spec,
        out_shape=jax.ShapeDtypeStruct((b, nh, HEAD_DIM), BF16),
        compiler_params=_cp("parallel", "arbitrary"),
    )(page_table, q, k_pool, v_pool)


def _chunk_sums(x, w0, w1):
    x3 = x.reshape(x.shape[0] // CMP_STRIDE, CMP_STRIDE, x.shape[1])
    return jnp.sum(x3 * w0[None], axis=1), jnp.sum(x3 * w1[None], axis=1)


def _compress_kernel(kc_ref, vc_ref, wpos_ref, wlin_ref, ko_ref, vo_ref):
    nchunk = kc_ref.shape[0] // CMP_STRIDE
    for idx, (src, dst) in enumerate(((kc_ref, ko_ref), (vc_ref, vo_ref))):
        a, bsum = _chunk_sums(src[...], wpos_ref[idx, 0:CMP_STRIDE, :], wpos_ref[idx, CMP_STRIDE:, :])
        pooled = a + pltpu.roll(bsum, nchunk - 1, 0)
        for h in range(NSA_KV_HEADS):
            sl = slice(h * HEAD_DIM, (h + 1) * HEAD_DIM)
            dst[:, sl] = jnp.dot(pooled[:, sl].astype(BF16), wlin_ref[idx, h], preferred_element_type=F32)


def _compress_prompt(h, wpos, wlin):
    b, t, _ = h.shape
    kvw = NSA_KV_HEADS * HEAD_DIM
    nchunk = t // CMP_STRIDE
    out = jax.ShapeDtypeStruct((b, nchunk, kvw), F32)
    ospec = pl.BlockSpec((None, nchunk, kvw), lambda i: (i, 0, 0))
    return pl.pallas_call(
        _compress_kernel,
        grid=(b,),
        in_specs=[pl.BlockSpec((None, t, kvw), lambda i: (i, 0, E_KC // kvw)),
                  pl.BlockSpec((None, t, kvw), lambda i: (i, 0, E_VC // kvw)),
                  pl.BlockSpec(wpos.shape, lambda i: (0, 0, 0)),
                  pl.BlockSpec(wlin.shape, lambda i: (0, 0, 0, 0))],
        out_specs=[ospec, ospec],
        out_shape=[out, out],
        compiler_params=_cp("parallel"),
    )(h, h, wpos, wlin)


def _topk_mask(score, n_blocks, k_top):
    lane = lax.broadcasted_iota(jnp.int32, score.shape, 1)
    rank = jnp.zeros(score.shape, jnp.int32)
    for s2 in range(n_blocks):
        col = score[:, s2:s2 + 1]
        beats = jnp.where(col > score, 1, jnp.where(col == score, jnp.where(lane > s2, 1, 0), 0))
        rank = rank + beats
    keep = jnp.where(rank < k_top, jnp.where(score > 0.5 * NEG_INF, jnp.where(lane < n_blocks, 1.0, 0.0), 0.0), 0.0)
    return keep


def _flash_init(m_sc, l_sc, acc_sc):
    m_sc[...] = jnp.full(m_sc.shape, NEG_INF, F32)
    l_sc[...] = jnp.zeros(l_sc.shape, F32)
    acc_sc[...] = jnp.zeros(acc_sc.shape, F32)


def _flash_chunk(q_all, k, v, bias, mask, m_sc, l_sc, acc_sc):
    g, r, _ = m_sc.shape
    c = k.shape[0]
    s = _nt(q_all, k).reshape(g, r, c) * (HEAD_DIM ** -0.5) + bias
    if mask is not None:
        s = jnp.where(mask[None], s, NEG_INF)
    m_prev = m_sc[...]
    m_new = jnp.maximum(m_prev, jnp.max(s, axis=-1, keepdims=True))
    alpha = jnp.exp(m_prev - m_new)
    p = jnp.exp(s - m_new)
    if mask is not None:
        p = jnp.where(mask[None], p, 0.0)
    l_sc[...] = alpha * l_sc[...] + jnp.sum(p, axis=-1, keepdims=True)
    pv = jnp.dot(p.reshape(g * r, c).astype(BF16), v, preferred_element_type=F32).reshape(g, r, HEAD_DIM)
    acc_sc[...] = alpha * acc_sc[...] + pv
    m_sc[...] = m_new


def _nsa_kernel(b31_ref, q_ref, g_ref, kc_ref, vc_ref, ks_ref, vs_ref, kvw_ref, tabc_ref, tabd_ref, o_ref,
                m_sc, l_sc, acc_sc, oc_sc, os_sc, mask_sc):
    i = pl.program_id(1)
    qb = q_ref.shape[0]
    t_len = ks_ref.shape[0]
    n_cmp = (t_len - CMP_BLOCK) // CMP_STRIDE + 1
    n_slc = t_len // SEL_BLOCK
    k_top = min(N_SEL, n_slc)
    gsz = NSA_GROUP
    rowi = lax.broadcasted_iota(jnp.int32, (qb, LANES), 0)
    lane = lax.broadcasted_iota(jnp.int32, (qb, LANES), 1)
    pos_q = i * qb + rowi
    causal = lane <= rowi
    wedge = lane >= rowi
    sig = jax.nn.sigmoid(g_ref[...])
    es = lax.broadcasted_iota(jnp.int32, (LANES, t_len), 0)
    ej = lax.broadcasted_iota(jnp.int32, (LANES, t_len), 1)
    expand = jnp.where((ej >> SEL_SHIFT) == es, 1.0, 0.0).astype(BF16)
    nc = kc_ref.shape[0]
    on = lax.broadcasted_iota(jnp.int32, (nc, LANES), 0)
    osb = lax.broadcasted_iota(jnp.int32, (nc, LANES), 1)
    c_start = on * CMP_STRIDE
    c_end = c_start + CMP_BLOCK - 1
    overlap = jnp.where((c_start < osb * SEL_BLOCK + SEL_BLOCK) & (c_end >= osb * SEL_BLOCK) & (on < n_cmp)
                        & (osb < n_slc), 1.0, 0.0).astype(BF16)
    lane_c = lax.broadcasted_iota(jnp.int32, (qb, nc), 1)
    cmask = ((lane_c * CMP_STRIDE + CMP_BLOCK - 1) <= pos_q[:, :1]) & (lane_c < n_cmp)

    for h in range(NSA_KV_HEADS):
        hs = slice(h * HEAD_DIM, (h + 1) * HEAD_DIM)
        g0 = h * gsz
        q_all = jnp.concatenate(
            [q_ref[:, (g0 + g) * HEAD_DIM:(g0 + g + 1) * HEAD_DIM] for g in range(gsz)], axis=0).astype(BF16)
        far_bias = jnp.stack([jnp.full((1, 1), b31_ref[g0 + g], F32) for g in range(gsz)], axis=0)

        kc = kc_ref[:, hs].astype(BF16)
        vc = vc_ref[:, hs].astype(BF16)
        s = _nt(q_all, kc).reshape(gsz, qb, nc) * (HEAD_DIM ** -0.5) + tabc_ref[g0:g0 + gsz]
        s = jnp.where(cmask[None], s, NEG_INF)
        s = s - jnp.max(s, axis=-1, keepdims=True)
        e = jnp.exp(s)
        pc = jnp.where(cmask[None], e / jnp.sum(e, axis=-1, keepdims=True), 0.0)
        oc_sc[...] = jnp.dot(pc.reshape(gsz * qb, nc).astype(BF16), vc,
                             preferred_element_type=F32).reshape(gsz, qb, HEAD_DIM)
        pcs = jnp.sum(pc, axis=0)
        hi, mid, lo = _split3(pcs)
        imp = (jnp.dot(hi, overlap, preferred_element_type=F32) + jnp.dot(mid, overlap, preferred_element_type=F32)
               + jnp.dot(lo, overlap, preferred_element_type=F32))
        cur = pos_q >> SEL_SHIFT
        forced = (lane == 0) | (lane == cur) | (lane == cur - 1)
        score = jnp.where(lane > cur, NEG_INF, jnp.where(forced, BIG, imp))
        keep = _topk_mask(score, n_slc, k_top)
        mask_sc[...] = jnp.dot(keep.astype(BF16), expand, preferred_element_type=F32)

        def sel_mask(c):
            return mask_sc[:, pl.ds(pl.multiple_of(c * qb, qb), qb)] > 0.5

        def kv_chunk(k_ref, v_ref, c, ksl, vsl):
            off = pl.multiple_of(c * qb, qb)
            return k_ref[pl.ds(off, qb), ksl].astype(BF16), v_ref[pl.ds(off, qb), vsl].astype(BF16)

        tab0 = lambda: tabd_ref[g0:g0 + gsz, 0]
        tab1 = lambda: tabd_ref[g0:g0 + gsz, 1]

        _flash_init(m_sc, l_sc, acc_sc)

        def far_body(c, carry):
            k, v = kv_chunk(ks_ref, vs_ref, c, hs, hs)
            _flash_chunk(q_all, k, v, far_bias, sel_mask(c), m_sc, l_sc, acc_sc)
            return carry

        lax.fori_loop(0, jnp.maximum(i - 1, 0), far_body, 0)

        @pl.when(i >= 1)
        def _():
            k, v = kv_chunk(ks_ref, vs_ref, i - 1, hs, hs)
            _flash_chunk(q_all, k, v, tab1(), sel_mask(i - 1), m_sc, l_sc, acc_sc)

        k, v = kv_chunk(ks_ref, vs_ref, i, hs, hs)
        _flash_chunk(q_all, k, v, tab0(), sel_mask(i) & causal, m_sc, l_sc, acc_sc)
        os_sc[...] = acc_sc[...] / l_sc[...]

        kw_sl = hs
        vw_sl = slice(NSA_KV_HEADS * HEAD_DIM + h * HEAD_DIM, NSA_KV_HEADS * HEAD_DIM + (h + 1) * HEAD_DIM)
        _flash_init(m_sc, l_sc, acc_sc)
        n_back = WINDOW // qb
        for back in range(n_back, 0, -1):
            @pl.when(i >= back)
            def _(back=back):
                k, v = kv_chunk(kvw_ref, kvw_ref, i - back, kw_sl, vw_sl)
                bias = tab1() if back == 1 else far_bias
                _flash_chunk(q_all, k, v, bias, wedge if back == n_back else None, m_sc, l_sc, acc_sc)

        k, v = kv_chunk(kvw_ref, kvw_ref, i, kw_sl, vw_sl)
        _flash_chunk(q_all, k, v, tab0(), causal, m_sc, l_sc, acc_sc)
        ow = acc_sc[...] / l_sc[...]

        def gate(j):
            return jnp.stack([sig[:, (g0 + g) * 3 + j:(g0 + g) * 3 + j + 1] for g in range(gsz)], axis=0)

        y = gate(0) * oc_sc[...] + gate(1) * os_sc[...] + gate(2) * ow
        for g in range(gsz):
            o_ref[:, (g0 + g) * HEAD_DIM:(g0 + g + 1) * HEAD_DIM] = y[g].astype(o_ref.dtype)


def _nsa_prompt(h, kcmp, vcmp, tabc, tabd, b31, qb=128):
    b, t, _ = h.shape
    assert qb == LANES and WINDOW % qb == 0 and t % qb == 0 and t // CMP_STRIDE <= LANES and t // SEL_BLOCK <= LANES
    qw = NSA_HEADS * HEAD_DIM
    kvw = NSA_KV_HEADS * HEAD_DIM
    nchunk = kcmp.shape[1]
    return pl.pallas_call(
        _nsa_kernel,
        grid=(b, t // qb),
        in_specs=[pl.BlockSpec(memory_space=pltpu.SMEM),
                  pl.BlockSpec((None, qb, qw), lambda bi, i: (bi, i, E_Q // qw)),
                  pl.BlockSpec((None, qb, LANES), lambda bi, i: (bi, i, E_G // LANES)),
                  pl.BlockSpec((None, nchunk, kvw), lambda bi, i: (bi, 0, 0)),
                  pl.BlockSpec((None, nchunk, kvw), lambda bi, i: (bi, 0, 0)),
                  pl.BlockSpec((None, t, kvw), lambda bi, i: (bi, 0, E_KS // kvw)),
                  pl.BlockSpec((None, t, kvw), lambda bi, i: (bi, 0, E_VS // kvw)),
                  pl.BlockSpec((None, t, 2 * kvw), lambda bi, i: (bi, 0, E_KVW // (2 * kvw))),
                  pl.BlockSpec((NSA_HEADS, qb, nchunk), lambda bi, i: (0, i, 0)),
                  pl.BlockSpec((NSA_HEADS, 2, qb, LANES), lambda bi, i: (0, 0, 0, 0))],
        out_specs=pl.BlockSpec((None, qb, qw), lambda bi, i: (bi, i, 0)),
        out_shape=jax.ShapeDtypeStruct((b, t, qw), BF16),
        scratch_shapes=[pltpu.VMEM((NSA_GROUP, qb, 1), F32), pltpu.VMEM((NSA_GROUP, qb, 1), F32),
                        pltpu.VMEM((NSA_GROUP, qb, HEAD_DIM), F32), pltpu.VMEM((NSA_GROUP, qb, HEAD_DIM), F32),
                        pltpu.VMEM((NSA_GROUP, qb, HEAD_DIM), F32), pltpu.VMEM((qb, t), F32)],
        compiler_params=_cp("parallel", "parallel"),
    )(b31, h, h, kcmp, vcmp, h, h, h, tabc, tabd)


def _cmp_dec_kernel(pt_ref, *refs, n_pg):
    k_refs = refs[:n_pg]
    v_refs = refs[n_pg:2 * n_pg]
    wpos_ref = refs[2 * n_pg]
    ak_ref, bk_ref, av_ref, bv_ref = refs[2 * n_pg + 1:]
    per = PAGE_SIZE // CMP_STRIDE
    for idx, (srcs, a_ref, b_ref) in enumerate(((k_refs, ak_ref, bk_ref), (v_refs, av_ref, bv_ref))):
        for r in range(n_pg):
            for h in range(NSA_KV_HEADS):
                sl = slice(h * HEAD_DIM, (h + 1) * HEAD_DIM)
                a, bsum = _chunk_sums(srcs[r][:, h, :], wpos_ref[idx, 0:CMP_STRIDE, sl], wpos_ref[idx, CMP_STRIDE:, sl])
                a_ref[r * per:(r + 1) * per, sl] = a
                b_ref[r * per:(r + 1) * per, sl] = bsum


def _cmp_dec(k_pool, v_pool, layer, page_table, wpos, n_pg=8):
    b, n_pages = page_table.shape
    kvw = NSA_KV_HEADS * HEAD_DIM
    per = PAGE_SIZE // CMP_STRIDE

    def page(r):
        return pl.BlockSpec((None, None, PAGE_SIZE, NSA_KV_HEADS, HEAD_DIM),
                            lambda bi, c, pt: (layer, pt[bi, c * n_pg + r], 0, 0, 0))

    ospec = pl.BlockSpec((None, n_pg * per, kvw), lambda bi, c, pt: (bi, c, 0))
    out = jax.ShapeDtypeStruct((b, n_pages * per, kvw), F32)
    grid_spec = pltpu.PrefetchScalarGridSpec(
        num_scalar_prefetch=1,
        grid=(b, n_pages // n_pg),
        in_specs=[page(r) for r in range(n_pg)] * 2 + [pl.BlockSpec(wpos.shape, lambda bi, c, pt: (0, 0, 0))],
        out_specs=[ospec] * 4,
    )
    return pl.pallas_call(
        functools.partial(_cmp_dec_kernel, n_pg=n_pg),
        grid_spec=grid_spec,
        out_shape=[out] * 4,
        compiler_params=_cp("parallel", "parallel"),
    )(page_table, *([k_pool] * n_pg), *([v_pool] * n_pg), wpos)


def _nsa_dec_cmp_kernel(q_ref, ak_ref, bk_ref, av_ref, bv_ref, wlin_ref, bias_ref, oc_ref, idx_ref,
                        *, pos, n_cmp, n_slc):
    nrow = ak_ref.shape[0]
    sw = idx_ref.shape[-1]
    lane_n = lax.broadcasted_iota(jnp.int32, (1, nrow), 1)
    cmask = ((lane_n * CMP_STRIDE + CMP_BLOCK - 1) <= pos) & (lane_n < n_cmp)
    on = lax.broadcasted_iota(jnp.int32, (nrow, sw), 0)
    osb = lax.broadcasted_iota(jnp.int32, (nrow, sw), 1)
    c_start = on * CMP_STRIDE
    overlap = jnp.where((c_start < osb * SEL_BLOCK + SEL_BLOCK) & (c_start + CMP_BLOCK - 1 >= osb * SEL_BLOCK)
                        & (on < n_cmp) & (osb < n_slc), 1.0, 0.0).astype(BF16)
    pk = ak_ref[...] + pltpu.roll(bk_ref[...], nrow - 1, 0)
    pv = av_ref[...] + pltpu.roll(bv_ref[...], nrow - 1, 0)
    lane_s = lax.broadcasted_iota(jnp.int32, (1, sw), 1)
    ri = lax.broadcasted_iota(jnp.int32, (sw, sw), 0)
    ci = lax.broadcasted_iota(jnp.int32, (sw, sw), 1)
    k_top = min(N_SEL, n_slc)
    cur = pos // SEL_BLOCK
    for h in range(NSA_KV_HEADS):
        sl = slice(h * HEAD_DIM, (h + 1) * HEAD_DIM)
        kc = jnp.dot(pk[:, sl].astype(BF16), wlin_ref[0, h], preferred_element_type=F32).astype(BF16)
        vc = jnp.dot(pv[:, sl].astype(BF16), wlin_ref[1, h], preferred_element_type=F32).astype(BF16)
        q = q_ref[h].astype(BF16)
        s = _nt(q, kc) * (HEAD_DIM ** -0.5) + bias_ref[h * NSA_GROUP:(h + 1) * NSA_GROUP]
        s = jnp.where(cmask, s, NEG_INF)
        s = s - jnp.max(s, axis=-1, keepdims=True)
        e = jnp.exp(s)
        pc = jnp.where(cmask, e / jnp.sum(e, axis=-1, keepdims=True), 0.0)
        oc_ref[h] = jnp.dot(pc.astype(BF16), vc, preferred_element_type=F32)
        pcs = jnp.sum(pc, axis=0, keepdims=True)
        hi, mid, lo = _split3(jnp.broadcast_to(pcs, (8, nrow)))
        imp = (jnp.dot(hi, overlap, preferred_element_type=F32) + jnp.dot(mid, overlap, preferred_element_type=F32)
               + jnp.dot(lo, overlap, preferred_element_type=F32))[0:1]
        forced = (lane_s == 0) | (lane_s == cur) | (lane_s == cur - 1)
        score = jnp.where(lane_s > cur, NEG_INF, jnp.where(forced, BIG, imp))
        score = jnp.where(lane_s < n_slc, score, -jnp.inf)
        rowm = jnp.broadcast_to(score, (sw, sw))
        colm = rowm.T
        beats = jnp.where(colm > rowm, 1, jnp.where(colm == rowm, jnp.where(ri < ci, 1, 0), 0))
        beats = jnp.where(ri < n_slc, beats, 0)
        rank = jnp.sum(beats, axis=0, keepdims=True)
        keep = (rank < k_top) & (score > 0.5 * NEG_INF) & (lane_s < n_slc)
        out = jnp.full((1, sw), -1, jnp.int32)
        for r in range(k_top):
            hit = keep & (rank == r)
            val = jnp.sum(jnp.where(hit, lane_s + 1, 0), axis=1, keepdims=True) - 1
            out = jnp.where(lane_s == r, val, out)
        idx_ref[h] = out


def _nsa_dec_cmp(q, ak, bk, av, bv, wlin, bias_c, pos, n_cmp, n_slc):
    b = q.shape[0]
    nrow = ak.shape[1]
    kvw = ak.shape[2]
    sw = -(-n_slc // LANES) * LANES
    part = pl.BlockSpec((None, nrow, kvw), lambda i: (i, 0, 0))
    kern = functools.partial(_nsa_dec_cmp_kernel, pos=pos, n_cmp=n_cmp, n_slc=n_slc)
    return pl.pallas_call(
        kern,
        grid=(b,),
        in_specs=[pl.BlockSpec((None,) + q.shape[1:], lambda i: (i, 0, 0, 0)), part, part, part, part,
                  pl.BlockSpec(wlin.shape, lambda i: (0, 0, 0, 0)),
                  pl.BlockSpec(bias_c.shape, lambda i: (0, 0))],
        out_specs=[pl.BlockSpec((None,) + q.shape[1:], lambda i: (i, 0, 0, 0)),
                   pl.BlockSpec((None, NSA_KV_HEADS, 1, sw), lambda i: (i, 0, 0, 0))],
        out_shape=[jax.ShapeDtypeStruct(q.shape, F32), jax.ShapeDtypeStruct((b, NSA_KV_HEADS, 1, sw), jnp.int32)],
        compiler_params=_cp("parallel"),
    )(q, ak, bk, av, bv, wlin, bias_c)


def _bucket_bias(dist, rb):
    n = jnp.maximum(dist, 0)
    max_exact = N_BUCKETS // 2
    nf = jnp.maximum(n, 1).astype(F32)
    large = max_exact + (jnp.log(nf / max_exact) / math.log(MAX_DISTANCE / max_exact)
                         * (N_BUCKETS - max_exact)).astype(jnp.int32)
    bucket = jnp.where(n < max_exact, n, jnp.minimum(large, N_BUCKETS - 1))
    out = jnp.zeros((rb.shape[0], dist.shape[1]), F32)
    for k in range(N_BUCKETS):
        out = jnp.where(bucket == k, rb[:, k:k + 1], out)
    return out


def _nsa_dec_sel_kernel(pg_ref, hf_ref, blk_ref, q_ref, kn_ref, vn_ref, rb_ref, k0_ref, k1_ref, v0_ref, v1_ref,
                        o_ref, m_sc, l_sc, acc_sc, *, pos):
    bi = pl.program_id(0)
    j = pl.program_id(1)
    scale = HEAD_DIM ** -0.5
    lane = lax.broadcasted_iota(jnp.int32, (1, SEL_BLOCK), 1)

    @pl.when(j == 0)
    def _():
        for h in range(NSA_KV_HEADS):
            q = q_ref[h]
            rb = rb_ref[h]
            s = jnp.sum(q.astype(BF16).astype(F32) * kn_ref[h:h + 1, :].astype(BF16).astype(F32),
                        axis=-1, keepdims=True) * scale + rb[:, 0:1]
            m_sc[h] = s
            l_sc[h] = jnp.ones_like(s)
            acc_sc[h] = jnp.broadcast_to(vn_ref[h:h + 1, :].astype(BF16).astype(F32), (q.shape[0], HEAD_DIM))

    for h, (k_ref, v_ref) in enumerate(((k0_ref, v0_ref), (k1_ref, v1_ref))):
        blk = blk_ref[bi, h * pl.num_programs(1) + j]

        @pl.when(blk >= 0)
        def _(h=h, k_ref=k_ref, v_ref=v_ref, blk=blk):
            q = q_ref[h].astype(BF16)
            k = k_ref[:, h, :].astype(BF16)
            v = v_ref[:, h, :].astype(BF16)
            tok = blk * SEL_BLOCK + lane
            ok = tok <= pos
            s = _nt(q, k) * scale + _bucket_bias(pos - tok, rb_ref[h])
            s = jnp.where(ok, s, NEG_INF)
            m_prev = m_sc[h]
            m_new = jnp.maximum(m_prev, jnp.max(s, axis=-1, keepdims=True))
            alpha = jnp.exp(m_prev - m_new)
            p = jnp.where(ok, jnp.exp(s - m_new), 0.0)
            l_sc[h] = alpha * l_sc[h] + jnp.sum(p, axis=-1, keepdims=True)
            acc_sc[h] = alpha * acc_sc[h] + jnp.dot(p.astype(BF16), v, preferred_element_type=F32)
            m_sc[h] = m_new

    @pl.when(j == pl.num_programs(1) - 1)
    def _():
        for h in range(NSA_KV_HEADS):
            o_ref[h] = acc_sc[h] / l_sc[h]


def _nsa_dec_sel(q, k_new, v_new, rb, k_pool, v_pool, layer, pages, halves, blocks, pos):
    b = q.shape[0]
    k_top = blocks.shape[-1]
    pages, halves, blocks = (a.reshape(b, NSA_KV_HEADS * k_top) for a in (pages, halves, blocks))

    def blkspec(h):
        return pl.BlockSpec((None, None, SEL_BLOCK, NSA_KV_HEADS, HEAD_DIM),
                            lambda bi, j, pg, hf, bl: (layer, pg[bi, h * k_top + j], hf[bi, h * k_top + j], 0, 0))

    qspec = pl.BlockSpec((None,) + q.shape[1:], lambda bi, j, pg, hf, bl: (bi, 0, 0, 0))
    nspec = pl.BlockSpec((None, NSA_KV_HEADS, HEAD_DIM), lambda bi, j, pg, hf, bl: (bi, 0, 0))
    grid_spec = pltpu.PrefetchScalarGridSpec(
        num_scalar_prefetch=3,
        grid=(b, k_top),
        in_specs=[qspec, nspec, nspec, pl.BlockSpec(rb.shape, lambda bi, j, pg, hf, bl: (0, 0, 0)),
                  blkspec(0), blkspec(1), blkspec(0), blkspec(1)],
        out_specs=qspec,
        scratch_shapes=[pltpu.VMEM((NSA_KV_HEADS, NSA_GROUP, 1), F32), pltpu.VMEM((NSA_KV_HEADS, NSA_GROUP, 1), F32),
                        pltpu.VMEM((NSA_KV_HEADS, NSA_GROUP, HEAD_DIM), F32)],
    )
    return pl.pallas_call(
        functools.partial(_nsa_dec_sel_kernel, pos=pos),
        grid_spec=grid_spec,
        out_shape=jax.ShapeDtypeStruct(q.shape, F32),
        compiler_params=_cp("parallel", "arbitrary"),
    )(pages, halves, blocks, q, k_new, v_new, rb, k_pool, k_pool, v_pool, v_pool)


def _nsa_dec_win_kernel(q_ref, g_ref, win_ref, new_ref, bias_ref, oc_ref, os_ref, y_ref, nw_ref):
    scale = HEAD_DIM ** -0.5
    nwin = win_ref.shape[0]
    sig = jax.nn.sigmoid(g_ref[...])
    for h in range(NSA_KV_HEADS):
        q = q_ref[h].astype(BF16)
        k = win_ref[:, 0, h, :].astype(BF16)
        v = win_ref[:, 1, h, :].astype(BF16)
        kn = new_ref[0, h:h + 1, :].astype(BF16)
        vn = new_ref[1, h:h + 1, :].astype(BF16)
        bias = bias_ref[h * NSA_GROUP:(h + 1) * NSA_GROUP]
        s = _nt(q, k) * scale + bias[:, :nwin]
        s_new = jnp.sum(q.astype(F32) * kn.astype(F32), axis=-1, keepdims=True) * scale + bias[:, nwin:nwin + 1]
        m = jnp.maximum(jnp.max(s, axis=-1, keepdims=True), s_new)
        p = jnp.exp(s - m)
        p_new = jnp.exp(s_new - m)
        den = jnp.sum(p, axis=-1, keepdims=True) + p_new
        ow = (jnp.dot(p.astype(BF16), v, preferred_element_type=F32)
              + p_new.astype(BF16).astype(F32) * vn.astype(F32)) / den
        gs = sig[h]
        y_ref[h] = (gs[:, 0:1] * oc_ref[h] + gs[:, 1:2] * os_ref[h] + gs[:, 2:3] * ow).astype(y_ref.dtype)
    nw_ref[pl.ds(0, nwin - 1)] = win_ref[pl.ds(1, nwin - 1)]
    nw_ref[nwin - 1] = new_ref[...]


def _nsa_dec_win(q, gates, win, new_kv, bias_w, o_c, o_s):
    b = q.shape[0]
    qspec = pl.BlockSpec((None,) + q.shape[1:], lambda i: (i, 0, 0, 0))
    wspec = pl.BlockSpec((None,) + win.shape[1:], lambda i: (i, 0, 0, 0, 0))
    bias2 = bias_w[:, 0, :]
    return pl.pallas_call(
        _nsa_dec_win_kernel,
        grid=(b,),
        in_specs=[qspec, pl.BlockSpec((None,) + gates.shape[1:], lambda i: (i, 0, 0, 0)), wspec,
                  pl.BlockSpec((None,) + new_kv.shape[1:], lambda i: (i, 0, 0, 0)),
                  pl.BlockSpec(bias2.shape, lambda i: (0, 0)), qspec, qspec],
        out_specs=[qspec, wspec],
        out_shape=[jax.ShapeDtypeStruct(q.shape, BF16), jax.ShapeDtypeStruct(win.shape, F32)],
        compiler_params=_cp("parallel"),
    )(q, gates, win, new_kv, bias2, o_c, o_s)


def _tail(x, xb_unused, y_mix, mem_kv, ffn_prev, p, l, bsz, t, alpha):
    d = x.shape[1]
    x1, x1b = _add_ln(x, y_mix, p['ln_g'][l, 0], p['ln_b'][l, 0], alpha)
    tm = 1024
    q = _mm(x1b, p['w_cq'][l], F32, tm, 512)
    o = _cross(q.reshape(bsz, t, -1), mem_kv, 512).reshape(bsz * t, -1)
    yc = _mm(o, p['w_co'][l], F32, tm, 512)
    x2, x2b = _add_ln(x1, yc, p['ln_g'][l, 1], p['ln_b'][l, 1], alpha)
    if t > 1:
        act, st1, st2 = _ffn_up(x2b, p['w_up'][l], p['ffn_conv'][l], t, 1024)
        ffn_new = jnp.concatenate([st1[:, 6:], st2[:, 6:]], axis=-1)
    else:
        act, ffn_new = _ffn_up_dec(x2b, p['w_up'][l], p['ffn_conv'][l], ffn_prev)
    dff = act.shape[1]
    f = _mm_acc(act, p['w_down'][l], F32, 1024, 512, dff // 2)
    x3, x3b = _add_ln(x2, f, p['ln_g'][l, 2], p['ln_b'][l, 2], alpha)
    return x3, x3b, ffn_new


def kernel(x_prompt, x_sample, mem_prompt, state_pool, cache_nsa_cmp_k, cache_nsa_cmp_v, cache_nsa_sel_k, cache_nsa_sel_v, state_nsa_win, state_sc, cache_sb_k, cache_sb_v, state_ffn, cache_mem, page_table, w_in_even, w_pool, pool_scale, w_cmp_pos, w_cmp_lin, rel_bias, w_out_even, w_in_odd, sc_conv, w_out_odd, w_cq, w_ckv, w_co, w_up, ffn_conv, w_down, ln_g, ln_b):
    bp, t, d = x_prompt.shape
    bs = x_sample.shape[0]
    depth = w_cq.shape[0]
    n_pages = page_table.shape[1]
    past = n_pages * PAGE_SIZE
    alpha = (2.0 * depth) ** 0.25
    kvw = NSA_KV_HEADS * HEAD_DIM
    assert x_sample.shape[1] == 1 and state_nsa_win.shape[2] == WINDOW

    we = w_in_even
    w_even = jnp.concatenate([we[:, :, 1024:4096], we[:, :, :1024], we[:, :, 4096:],
                              jnp.zeros(we.shape[:2] + (E_TOT - we.shape[2],), we.dtype)], axis=2).astype(BF16)
    p = {'w_cq': w_cq.astype(BF16), 'w_co': w_co.astype(BF16), 'w_up': w_up.astype(BF16),
         'w_down': w_down.astype(BF16), 'ffn_conv': ffn_conv, 'ln_g': ln_g, 'ln_b': ln_b}
    w_odd = w_in_odd.astype(BF16)
    w_oe = w_out_even.astype(BF16)
    w_oo = w_out_odd.astype(BF16)
    w_kv = w_ckv.astype(BF16)
    w_pool_b = w_pool.astype(BF16)
    w_lin_b = w_cmp_lin.astype(BF16)
    w_pos = w_cmp_pos.reshape(w_cmp_pos.shape[0], 2, CMP_BLOCK, kvw)

    qb = 128
    ti = np.arange(qb)[:, None]
    tj = np.arange(qb)[None, :]
    dist_d = np.stack([ti - tj, qb + ti - tj]).reshape(2 * qb, qb).astype(np.int32)
    tabd = _bias_lookup(rel_bias, jnp.asarray(dist_d), 2 * qb).reshape(NSA_HEADS, 2, qb, qb)
    nblk = t // qb
    nchunk = t // CMP_STRIDE
    pos_all = np.arange(t)[:, None]
    dist_c = (pos_all - (np.arange(nchunk)[None, :] * CMP_STRIDE + CMP_BLOCK - 1)).astype(np.int32)
    tabc = _bias_lookup(rel_bias, jnp.asarray(dist_c), 512)
    b31 = rel_bias[N_BUCKETS - 1]

    mem_b = mem_prompt.reshape(bp * mem_prompt.shape[1], d).astype(BF16)
    n_mem = mem_prompt.shape[1]

    x = x_prompt.reshape(bp * t, d)
    xb = x.astype(BF16)
    outs_p = {k: [] for k in ('pool', 'cmp_k', 'cmp_v', 'sel_k', 'sel_v', 'win', 'sc', 'sb_k', 'sb_v', 'ffn', 'mem')}
    for l in range(depth):
        e = l // 2
        memkv = _mm(mem_b, w_kv[l], F32, 1024, 512)
        outs_p['mem'].append(memkv.reshape(bp, n_mem, 2, MEM_HEADS, MEM_HEAD_DIM))
        if l % 2 == 0:
            h = _mm(xb, w_even[e], F32, 1024, 640).reshape(bp, t, E_TOT)
            y_pool = _pool_prompt(h, E_U // 1024, w_pool_b[e], pool_scale[e])
            kcmp, vcmp = _compress_prompt(h, w_pos[e], w_lin_b[e])
            y_nsa = _nsa_prompt(h, kcmp, vcmp, tabc, tabd, b31)
            mix = jnp.concatenate([y_pool, y_nsa], axis=-1).reshape(bp * t, d)
            y = _mm(mix, w_oe[e], F32, 1024, 512)
            outs_p['pool'].append(h[:, t - POOL_STATE:, E_U:E_U + 1024])
            for name, off in (('cmp_k', E_KC), ('cmp_v', E_VC), ('sel_k', E_KS), ('sel_v', E_VS)):
                outs_p[name].append(h[:, :, off:off + kvw].reshape(bp, t, NSA_KV_HEADS, HEAD_DIM))
            nw = min(WINDOW, t)
            outs_p['win'].append(h[:, t - nw:, E_KVW:E_KVW + 2 * kvw].reshape(bp, nw, 2, NSA_KV_HEADS, HEAD_DIM))
        else:
            h = _mm(xb, w_odd[e], F32, 1024, 512).reshape(bp, t, -1)
            y_sc, sc_st = _sconv_prompt(h, sc_conv[e])
            y_sb = _sb_prompt(h)
            mix = jnp.concatenate([y_sc, y_sb], axis=-1).reshape(bp * t, d)
            y = _mm(mix, w_oo[e], F32, 1024, 512)
            outs_p['sc'].append(sc_st[:, 6:])
            nh = (O_K - O_Q) // HEAD_DIM
            outs_p['sb_k'].append(h[:, :, O_K:O_V].reshape(bp, t, nh, HEAD_DIM))
            outs_p['sb_v'].append(h[:, :, O_V:].reshape(bp, t, nh, HEAD_DIM))
        x, xb, ffn_new = _tail(x, xb, y, memkv.reshape(bp, n_mem, -1), None, p, l, bp, t, alpha)
        outs_p['ffn'].append(ffn_new)
    y_prompt = x.reshape(bp, t, d)

    pos = past
    length = past + 1
    n_cmp = (length - CMP_BLOCK) // CMP_STRIDE + 1
    n_slc = -(-length // SEL_BLOCK)
    k_top = min(N_SEL, n_slc)
    ncrow = n_pages * (PAGE_SIZE // CMP_STRIDE)
    dist_cd = (pos - (np.arange(ncrow) * CMP_STRIDE + CMP_BLOCK - 1)).astype(np.int32)
    bias_cd = _bias_lookup(rel_bias, jnp.asarray(np.broadcast_to(dist_cd, (8, ncrow))), 8)[:, 0, :]
    dist_wd = np.maximum(WINDOW - np.arange(WINDOW + LANES), 0).astype(np.int32)
    bias_wd = _bias_lookup(rel_bias, jnp.asarray(np.broadcast_to(dist_wd, (8, WINDOW + LANES))), 8)
    rb_hg = rel_bias.T.reshape(NSA_KV_HEADS, NSA_GROUP, N_BUCKETS)
    mem_s = cache_mem.reshape(depth, bs, cache_mem.shape[2], -1)

    x = x_sample.reshape(bs, d)
    xb = x.astype(BF16)
    outs_s = {k: [] for k in ('pool', 'cmp_k', 'cmp_v', 'sel_k', 'sel_v', 'win', 'sc', 'sb_k', 'sb_v', 'ffn')}
    for l in range(depth):
        e = l // 2
        if l % 2 == 0:
            h = _mm(xb, w_even[e], F32, 8, 640)
            y_pool, pool_new = _pool_dec(state_pool[e], h[:, E_U:E_U + 1024], w_pool_b[e], pool_scale[e], pos)
            q4 = h[:, E_Q:E_Q + NSA_HEADS * HEAD_DIM].reshape(bs, NSA_KV_HEADS, NSA_GROUP, HEAD_DIM)
            gates = h[:, E_G:E_G + NSA_HEADS * 3].reshape(bs, NSA_KV_HEADS, NSA_GROUP, 3)
            ak, bk, av, bv = _cmp_dec(cache_nsa_cmp_k, cache_nsa_cmp_v, e, page_table, w_pos[e])
            o_c, idx = _nsa_dec_cmp(q4, ak, bk, av, bv, w_lin_b[e], bias_cd, pos, n_cmp, n_slc)
            blocks = idx[:, :, 0, :k_top]
            past_blk = jnp.where((blocks >= 0) & (blocks * SEL_BLOCK < past), blocks, -1)
            safe = jnp.maximum(past_blk, 0)
            per_page = PAGE_SIZE // SEL_BLOCK
            pages = jnp.take_along_axis(page_table[:, None, :], safe // per_page, axis=2)
            halves = safe % per_page
            ks_new = h[:, E_KS:E_KS + kvw].reshape(bs, NSA_KV_HEADS, HEAD_DIM)
            vs_new = h[:, E_VS:E_VS + kvw].reshape(bs, NSA_KV_HEADS, HEAD_DIM)
            o_s = _nsa_dec_sel(q4, ks_new, vs_new, rb_hg, cache_nsa_sel_k, cache_nsa_sel_v, e,
                               pages, halves, past_blk, pos)
            new_kv = h[:, E_KVW:E_KVW + 2 * kvw].reshape(bs, 2, NSA_KV_HEADS, HEAD_DIM)
            y_nsa, win_new = _nsa_dec_win(q4, gates, state_nsa_win[e], new_kv, bias_wd, o_c, o_s)
            mix = jnp.concatenate([y_pool, y_nsa.reshape(bs, -1)], axis=-1)
            y = _mm(mix, w_oe[e], F32, 8, 512)
            outs_s['pool'].append(pool_new)
            for name, off in (('cmp_k', E_KC), ('cmp_v', E_VC), ('sel_k', E_KS), ('sel_v', E_VS)):
                outs_s[name].append(h[:, off:off + kvw].reshape(bs, 1, NSA_KV_HEADS, HEAD_DIM))
            outs_s['win'].append(win_new)
        else:
            h = _mm(xb, w_odd[e], F32, 8, 512)
            y_sc, sc_new = _sconv_dec(state_sc[e], h, sc_conv[e])
            nh = (O_K - O_Q) // HEAD_DIM
            q3 = h[:, O_Q:O_K].reshape(bs, nh, HEAD_DIM)
            y_sb = _sb_dec(q3, cache_sb_k, cache_sb_v, e, page_table)
            mix = jnp.concatenate([y_sc, y_sb.reshape(bs, -1)], axis=-1)
            y = _mm(mix, w_oo[e], F32, 8, 512)
            outs_s['sc'].append(sc_new)
            outs_s['sb_k'].append(h[:, O_K:O_V].reshape(bs, 1, nh, HEAD_DIM))
            outs_s['sb_v'].append(h[:, O_V:].reshape(bs, 1, nh, HEAD_DIM))
        x, xb, ffn_new = _tail(x, xb, y, mem_s[l], state_ffn[l], p, l, bs, 1, alpha)
        outs_s['ffn'].append(ffn_new)
    y_sample = x.reshape(bs, 1, d)

    sp = {k: jnp.stack(v) for k, v in outs_p.items()}
    ss = {k: jnp.stack(v) for k, v in outs_s.items()}
    return (y_prompt, y_sample,
            sp['pool'], sp['cmp_k'], sp['cmp_v'], sp['sel_k'], sp['sel_v'], sp['win'],
            sp['sc'], sp['sb_k'], sp['sb_v'], sp['ffn'], sp['mem'],
            ss['pool'], ss['cmp_k'], ss['cmp_v'], ss['sel_k'], ss['sel_v'], ss['win'],
            ss['sc'], ss['sb_k'], ss['sb_v'], ss['ffn'])
```

```python
import functools
import math

import jax
import jax.numpy as jnp
import numpy as np
from jax import lax
from jax.experimental import pallas as pl
from jax.experimental.pallas import tpu as pltpu

F32 = jnp.float32
BF16 = jnp.bfloat16

HEAD_DIM = 128
PAGE_SIZE = 128
POOL_WINDOWS = (2, 4, 8, 16)
POOL_GROUP_DIM = 256
POOL_STATE = 15
NSA_KV_HEADS = 2
NSA_GROUP = 12
NSA_HEADS = 24
CMP_BLOCK = 32
CMP_STRIDE = 16
SEL_BLOCK = 64
SEL_SHIFT = 6
N_SEL = 16
WINDOW = 512
N_BUCKETS = 32
MAX_DISTANCE = 128
MEM_HEADS = 4
MEM_HEAD_DIM = 256
LN_EPS = 1e-5
NEG_INF = -1e30
BIG = 1e30

LANES = 128
VMEM_LIMIT = 56 * 1024 * 1024

E_Q, E_U, E_KC, E_VC, E_KS, E_VS, E_KVW, E_G, E_TOT = 0, 3072, 4096, 4352, 4608, 4864, 5120, 5632, 5760
O_X, O_B, O_C, O_Q, O_K, O_V = 0, 1024, 2048, 3072, 6144, 9216


def _cp(*sem):
    return pltpu.CompilerParams(dimension_semantics=sem, vmem_limit_bytes=VMEM_LIMIT)


def _nt(a, b):
    return lax.dot_general(a, b, (((1,), (1,)), ((), ())), preferred_element_type=F32)


def _split3(x):
    hi = x.astype(BF16)
    r1 = x - hi.astype(F32)
    mid = r1.astype(BF16)
    lo = (r1 - mid.astype(F32)).astype(BF16)
    return hi, mid, lo


def _mm_kernel(x_ref, w_ref, o_ref):
    o_ref[...] = jnp.dot(x_ref[...], w_ref[...], preferred_element_type=F32).astype(o_ref.dtype)


def _mm(x, w, out_dtype, tm, tn, name="mm"):
    m, k = x.shape
    n = w.shape[1]
    tm = min(tm, m)
    return pl.pallas_call(
        _mm_kernel,
        name=name,
        grid=(m // tm, n // tn),
        in_specs=[pl.BlockSpec((tm, k), lambda i, j: (i, 0)),
                  pl.BlockSpec((k, tn), lambda i, j: (0, j))],
        out_specs=pl.BlockSpec((tm, tn), lambda i, j: (i, j)),
        out_shape=jax.ShapeDtypeStruct((m, n), out_dtype),
        compiler_params=_cp("parallel", "parallel"),
    )(x, w)


def _mm_acc_kernel(x_ref, w_ref, o_ref, acc_ref):
    k = pl.program_id(2)

    @pl.when(k == 0)
    def _():
        acc_ref[...] = jnp.zeros_like(acc_ref)

    acc_ref[...] += jnp.dot(x_ref[...], w_ref[...], preferred_element_type=F32)

    @pl.when(k == pl.num_programs(2) - 1)
    def _():
        o_ref[...] = acc_ref[...].astype(o_ref.dtype)


def _mm_acc(x, w, out_dtype, tm, tn, tk):
    m, k = x.shape
    n = w.shape[1]
    tm = min(tm, m)
    return pl.pallas_call(
        _mm_acc_kernel,
        name="mm_down",
        grid=(m // tm, n // tn, k // tk),
        in_specs=[pl.BlockSpec((tm, tk), lambda i, j, kk: (i, kk)),
                  pl.BlockSpec((tk, tn), lambda i, j, kk: (kk, j))],
        out_specs=pl.BlockSpec((tm, tn), lambda i, j, kk: (i, j)),
        out_shape=jax.ShapeDtypeStruct((m, n), out_dtype),
        scratch_shapes=[pltpu.VMEM((tm, tn), F32)],
        compiler_params=_cp("parallel", "parallel", "arbitrary"),
    )(x, w)


def _ln_kernel(x_ref, y_ref, g_ref, b_ref, of_ref, ob_ref, *, alpha):
    z = alpha * x_ref[...] + y_ref[...]
    mu = jnp.mean(z, axis=-1, keepdims=True)
    zc = z - mu
    var = jnp.mean(zc * zc, axis=-1, keepdims=True)
    out = zc * lax.rsqrt(var + LN_EPS) * g_ref[...] + b_ref[...]
    of_ref[...] = out
    ob_ref[...] = out.astype(BF16)


def _add_ln(x, y, g, b, alpha, tm=256):
    m, d = x.shape
    tm = min(tm, m)
    row = pl.BlockSpec((tm, d), lambda i: (i, 0))
    vec = pl.BlockSpec((1, d), lambda i: (0, 0))
    return pl.pallas_call(
        functools.partial(_ln_kernel, alpha=alpha),
        name="add_ln",
        grid=(m // tm,),
        in_specs=[row, row, vec, vec],
        out_specs=[row, row],
        out_shape=[jax.ShapeDtypeStruct((m, d), F32), jax.ShapeDtypeStruct((m, d), BF16)],
        compiler_params=_cp("parallel"),
    )(x, y, g.reshape(1, d), b.reshape(1, d))


def _cross_kernel(q_ref, kv_ref, o_ref):
    width = MEM_HEADS * MEM_HEAD_DIM
    scale = MEM_HEAD_DIM ** -0.5
    for h in range(MEM_HEADS):
        sl = slice(h * MEM_HEAD_DIM, (h + 1) * MEM_HEAD_DIM)
        q = q_ref[:, sl].astype(BF16)
        k = kv_ref[:, sl].astype(BF16)
        v = kv_ref[:, width + h * MEM_HEAD_DIM: width + (h + 1) * MEM_HEAD_DIM].astype(BF16)
        s = _nt(q, k) * scale
        s = s - jnp.max(s, axis=-1, keepdims=True)
        e = jnp.exp(s)
        p = e / jnp.sum(e, axis=-1, keepdims=True)
        o_ref[:, sl] = jnp.dot(p.astype(BF16), v, preferred_element_type=F32).astype(o_ref.dtype)


def _cross(q, kv, tq):
    b, t, w = q.shape
    tq = min(tq, t)
    return pl.pallas_call(
        _cross_kernel,
        name="cross_attn",
        grid=(b, t // tq),
        in_specs=[pl.BlockSpec((None, tq, w), lambda i, j: (i, j, 0)),
                  pl.BlockSpec((None, kv.shape[1], kv.shape[2]), lambda i, j: (i, 0, 0))],
        out_specs=pl.BlockSpec((None, tq, w), lambda i, j: (i, j, 0)),
        out_shape=jax.ShapeDtypeStruct((b, t, w), BF16),
        compiler_params=_cp("parallel", "parallel"),
    )(q, kv)


def _shift_rows(h, c_last2, row):
    s1 = jnp.where(row == 0, c_last2[7:8], pltpu.roll(h, 1, 0))
    s2 = jnp.where(row == 0, c_last2[6:7], jnp.where(row == 1, c_last2[7:8], pltpu.roll(h, 2, 0)))
    return s1, s2


def _ffn_up_kernel(x_ref, w1_ref, w2_ref, cw1_ref, cw2_ref, act_ref, st1_ref, st2_ref, carry_ref,
                   *, tiles_per_batch):
    m = pl.program_id(0)
    n = pl.program_id(1)
    tm = x_ref.shape[0]
    tn = w1_ref.shape[1]
    x = x_ref[...]
    h1 = jnp.dot(x, w1_ref[...], preferred_element_type=F32)
    h2 = jnp.dot(x, w2_ref[...], preferred_element_type=F32)

    @pl.when(m % tiles_per_batch == 0)
    def _():
        carry_ref[n] = jnp.zeros((8, 2 * tn), F32)

    prev = carry_ref[n]
    row = lax.broadcasted_iota(jnp.int32, (tm, 1), 0)

    def conv(h, p, cw):
        s1, s2 = _shift_rows(h, p, row)
        return s2 * cw[0:1] + s1 * cw[1:2] + h * cw[2:3]

    c1 = conv(h1, prev[:, :tn], cw1_ref[...])
    c2 = conv(h2, prev[:, tn:], cw2_ref[...])
    act_ref[...] = (c1 * jax.nn.sigmoid(c1) * c2).astype(BF16)
    l1 = h1[tm - 8:]
    l2 = h2[tm - 8:]
    carry_ref[n] = jnp.concatenate([l1, l2], axis=1)
    st1_ref[...] = l1
    st2_ref[...] = l2


def _ffn_up(x, w_up, conv_w, t, tm, tn=256):
    m, d = x.shape
    dff = w_up.shape[1] // 2
    nt = dff // tn
    tm = min(tm, t)
    tpb = t // tm
    kern = functools.partial(_ffn_up_kernel, tiles_per_batch=tpb)
    act, st1, st2 = pl.pallas_call(
        kern,
        grid=(m // tm, nt),
        in_specs=[pl.BlockSpec((tm, d), lambda i, j: (i, 0)),
                  pl.BlockSpec((d, tn), lambda i, j: (0, j)),
                  pl.BlockSpec((d, tn), lambda i, j: (0, j + nt)),
                  pl.BlockSpec((3, tn), lambda i, j: (0, j)),
                  pl.BlockSpec((3, tn), lambda i, j: (0, j + nt))],
        out_specs=[pl.BlockSpec((tm, tn), lambda i, j: (i, j)),
                   pl.BlockSpec((None, 8, tn), lambda i, j: (i, 0, j)),
                   pl.BlockSpec((None, 8, tn), lambda i, j: (i, 0, j))],
        out_shape=[jax.ShapeDtypeStruct((m, dff), BF16),
                   jax.ShapeDtypeStruct((m // tm, 8, dff), F32),
                   jax.ShapeDtypeStruct((m // tm, 8, dff), F32)],
        scratch_shapes=[pltpu.VMEM((nt, 8, 2 * tn), F32)],
        compiler_params=_cp("arbitrary", "arbitrary"),
        name="ffn_up",
    )(x, w_up, w_up, conv_w, conv_w)
    return act, st1[tpb - 1::tpb], st2[tpb - 1::tpb]


def _ffn_up_dec_kernel(x_ref, w1_ref, w2_ref, cw1_ref, cw2_ref, p1_ref, p2_ref, act_ref, n1_ref, n2_ref):
    x = x_ref[...]
    h1 = jnp.dot(x, w1_ref[...], preferred_element_type=F32)
    h2 = jnp.dot(x, w2_ref[...], preferred_element_type=F32)

    def conv(h, p_ref, cw):
        return p_ref[:, 0, :] * cw[0:1] + p_ref[:, 1, :] * cw[1:2] + h * cw[2:3]

    c1 = conv(h1, p1_ref, cw1_ref[...])
    c2 = conv(h2, p2_ref, cw2_ref[...])
    act_ref[...] = (c1 * jax.nn.sigmoid(c1) * c2).astype(BF16)
    n1_ref[:, 0, :] = p1_ref[:, 1, :]
    n1_ref[:, 1, :] = h1
    n2_ref[:, 0, :] = p2_ref[:, 1, :]
    n2_ref[:, 1, :] = h2


def _ffn_up_dec(x, w_up, conv_w, prev, tn=256):
    b, d = x.shape
    dff = w_up.shape[1] // 2
    nt = dff // tn
    pspec1 = pl.BlockSpec((b, 2, tn), lambda j: (0, 0, j))
    pspec2 = pl.BlockSpec((b, 2, tn), lambda j: (0, 0, j + nt))
    act, n1, n2 = pl.pallas_call(
        _ffn_up_dec_kernel,
        grid=(nt,),
        in_specs=[pl.BlockSpec((b, d), lambda j: (0, 0)),
                  pl.BlockSpec((d, tn), lambda j: (0, j)),
                  pl.BlockSpec((d, tn), lambda j: (0, j + nt)),
                  pl.BlockSpec((3, tn), lambda j: (0, j)),
                  pl.BlockSpec((3, tn), lambda j: (0, j + nt)),
                  pspec1, pspec2],
        out_specs=[pl.BlockSpec((b, tn), lambda j: (0, j)),
                   pl.BlockSpec((b, 2, tn), lambda j: (0, 0, j)),
                   pl.BlockSpec((b, 2, tn), lambda j: (0, 0, j))],
        out_shape=[jax.ShapeDtypeStruct((b, dff), BF16),
                   jax.ShapeDtypeStruct((b, 2, dff), F32),
                   jax.ShapeDtypeStruct((b, 2, dff), F32)],
        compiler_params=_cp("parallel"),
    )(x, w_up, w_up, conv_w, conv_w, prev, prev)
    return act, jnp.concatenate([n1, n2], axis=-1)


def _bias_kernel(rb_ref, dist_ref, o_ref):
    head = pl.program_id(0)
    n = jnp.maximum(dist_ref[...], 0)
    max_exact = N_BUCKETS // 2
    nf = jnp.maximum(n, 1).astype(F32)
    large = max_exact + (jnp.log(nf / max_exact) / math.log(MAX_DISTANCE / max_exact)
                         * (N_BUCKETS - max_exact)).astype(jnp.int32)
    bucket = jnp.where(n < max_exact, n, jnp.minimum(large, N_BUCKETS - 1))
    out = jnp.zeros(n.shape, F32)
    for k in range(N_BUCKETS):
        out = jnp.where(bucket == k, rb_ref[k, head], out)
    o_ref[...] = out


def _bias_lookup(rel_bias, dist, tr):
    r, c = dist.shape
    nh = rel_bias.shape[1]
    tr = min(tr, r)
    return pl.pallas_call(
        _bias_kernel,
        grid=(nh, r // tr),
        in_specs=[pl.BlockSpec(memory_space=pltpu.SMEM),
                  pl.BlockSpec((tr, c), lambda h, i: (i, 0))],
        out_specs=pl.BlockSpec((None, tr, c), lambda h, i: (h, i, 0)),
        out_shape=jax.ShapeDtypeStruct((nh, r, c), F32),
        compiler_params=_cp("parallel", "parallel"),
        name="bias_lookup",
    )(rel_bias, dist)


def _bias_lookup_blocked(rel_bias, dist, tc):
    r, c = dist.shape
    nh = rel_bias.shape[1]
    return pl.pallas_call(
        _bias_kernel,
        grid=(nh, c // tc),
        in_specs=[pl.BlockSpec(memory_space=pltpu.SMEM),
                  pl.BlockSpec((r, tc), lambda h, j: (0, j))],
        out_specs=pl.BlockSpec((None, None, r, tc), lambda h, j: (j, h, 0, 0)),
        out_shape=jax.ShapeDtypeStruct((c // tc, nh, r, tc), F32),
        compiler_params=_cp("parallel", "parallel"),
        name="bias_lookup_blocked",
    )(rel_bias, dist)


def _pool_groups(sums, cur, pos, wp_ref, sc_ref, y_ref):
    for g, w in enumerate(POOL_WINDOWS):
        sl = slice(g * POOL_GROUP_DIM, (g + 1) * POOL_GROUP_DIM)
        cnt = jnp.minimum(w, pos + 1).astype(F32)
        d = sums[g] / cnt - cur[:, sl]
        y = jnp.dot(d.astype(BF16), wp_ref[g], preferred_element_type=F32) * sc_ref[:, sl]
        y_ref[:, sl] = y.astype(y_ref.dtype)


def _pool_kernel(prev_ref, cur_ref, wp_ref, sc_ref, y_ref):
    t = pl.program_id(1)
    tt = cur_ref.shape[0]
    cur = cur_ref[...]
    prev = jnp.where(t == 0, 0.0, prev_ref[...])
    ext = jnp.concatenate([prev, cur], axis=0)
    gd = POOL_GROUP_DIM
    s2 = ext + pltpu.roll(ext, 1, 0)
    x4 = s2[:, gd:]
    s4 = x4 + pltpu.roll(x4, 2, 0)
    x8 = s4[:, gd:]
    s8 = x8 + pltpu.roll(x8, 4, 0)
    x16 = s8[:, gd:]
    s16 = x16 + pltpu.roll(x16, 8, 0)
    sums = [s2[16:, :gd], s4[16:, :gd], s8[16:, :gd], s16[16:]]
    pos = t * tt + lax.broadcasted_iota(jnp.int32, (tt, 1), 0)
    _pool_groups(sums, cur, pos, wp_ref, sc_ref, y_ref)


def _pool_prompt(h, col_blk, w_pool, pool_scale, tt=256):
    b, t, _ = h.shape
    wdt = w_pool.shape[0] * POOL_GROUP_DIM
    tt = min(tt, t)
    r = tt // 16
    return pl.pallas_call(
        _pool_kernel,
        name="pool_prompt",
        grid=(b, t // tt),
        in_specs=[pl.BlockSpec((None, 16, wdt), lambda i, j: (i, jnp.maximum(j * r - 1, 0), col_blk)),
                  pl.BlockSpec((None, tt, wdt), lambda i, j: (i, j, col_blk)),
                  pl.BlockSpec(w_pool.shape, lambda i, j: (0, 0, 0)),
                  pl.BlockSpec((1, wdt), lambda i, j: (0, 0))],
        out_specs=pl.BlockSpec((None, tt, wdt), lambda i, j: (i, j, 0)),
        out_shape=jax.ShapeDtypeStruct((b, t, wdt), BF16),
        compiler_params=_cp("parallel", "parallel"),
    )(h, h, w_pool, pool_scale.reshape(1, wdt))


def _pool_dec_kernel(sp_ref, u_ref, wp_ref, sc_ref, y_ref, ns_ref, *, pos):
    u = u_ref[...]
    gd = POOL_GROUP_DIM
    acc = u
    sums = []
    back = 1
    for g, w in enumerate(POOL_WINDOWS):
        while back < w:
            acc = acc + sp_ref[:, POOL_STATE - back, :]
            back += 1
        sums.append(acc[:, g * gd:(g + 1) * gd])
    posv = jnp.full((u.shape[0], 1), pos, jnp.int32)
    _pool_groups(sums, u, posv, wp_ref, sc_ref, y_ref)
    for j in range(POOL_STATE - 1):
        ns_ref[:, j, :] = sp_ref[:, j + 1, :]
    ns_ref[:, POOL_STATE - 1, :] = u


def _pool_dec(state, u, w_pool, pool_scale, pos):
    b, wdt = u.shape
    return pl.pallas_call(
        functools.partial(_pool_dec_kernel, pos=pos),
        out_shape=[jax.ShapeDtypeStruct((b, wdt), BF16), jax.ShapeDtypeStruct(state.shape, F32)],
        compiler_params=pltpu.CompilerParams(vmem_limit_bytes=VMEM_LIMIT),
    )(state, u, w_pool, pool_scale.reshape(1, wdt))


def _sconv_kernel(px_ref, pc_ref, x_ref, b_ref, c_ref, cw_ref, y_ref, st_ref):
    t = pl.program_id(1)
    tt = x_ref.shape[0]
    v = c_ref[...] * x_ref[...]
    pv = jnp.where(t == 0, 0.0, pc_ref[...] * px_ref[...])
    row = lax.broadcasted_iota(jnp.int32, (tt, 1), 0)
    s1, s2 = _shift_rows(v, pv, row)
    cw = cw_ref[...]
    conv = s2 * cw[0:1] + s1 * cw[1:2] + v * cw[2:3]
    y_ref[...] = (b_ref[...] * conv).astype(y_ref.dtype)
    st_ref[...] = v[tt - 8:]


def _sconv_prompt(h, conv_w, tt=256):
    b, t, _ = h.shape
    wdt = conv_w.shape[1]
    tt = min(tt, t)
    r = tt // 8

    def prev(blk):
        return pl.BlockSpec((None, 8, wdt), lambda i, j: (i, jnp.maximum(j * r - 1, 0), blk))

    def cur(blk):
        return pl.BlockSpec((None, tt, wdt), lambda i, j: (i, j, blk))

    return pl.pallas_call(
        _sconv_kernel,
        name="sconv_prompt",
        grid=(b, t // tt),
        in_specs=[prev(0), prev(2), cur(0), cur(1), cur(2), pl.BlockSpec((3, wdt), lambda i, j: (0, 0))],
        out_specs=[pl.BlockSpec((None, tt, wdt), lambda i, j: (i, j, 0)),
                   pl.BlockSpec((None, 8, wdt), lambda i, j: (i, 0, 0))],
        out_shape=[jax.ShapeDtypeStruct((b, t, wdt), BF16), jax.ShapeDtypeStruct((b, 8, wdt), F32)],
        compiler_params=_cp("parallel", "arbitrary"),
    )(h, h, h, h, h, conv_w)


def _sconv_dec_kernel(st_ref, x_ref, b_ref, c_ref, cw_ref, y_ref, ns_ref):
    v = c_ref[...] * x_ref[...]
    cw = cw_ref[...]
    conv = st_ref[:, 0, :] * cw[0:1] + st_ref[:, 1, :] * cw[1:2] + v * cw[2:3]
    y_ref[...] = (b_ref[...] * conv).astype(y_ref.dtype)
    ns_ref[:, 0, :] = st_ref[:, 1, :]
    ns_ref[:, 1, :] = v


def _sconv_dec(state, h, conv_w):
    b = h.shape[0]
    wdt = conv_w.shape[1]

    def col(blk):
        return pl.BlockSpec((b, wdt), lambda i: (0, blk))

    return pl.pallas_call(
        _sconv_dec_kernel,
        grid=(1,),
        in_specs=[pl.BlockSpec(state.shape, lambda i: (0, 0, 0)), col(0), col(1), col(2),
                  pl.BlockSpec((3, wdt), lambda i: (0, 0))],
        out_specs=[pl.BlockSpec((b, wdt), lambda i: (0, 0)), pl.BlockSpec(state.shape, lambda i: (0, 0, 0))],
        out_shape=[jax.ShapeDtypeStruct((b, wdt), BF16), jax.ShapeDtypeStruct(state.shape, F32)],
        compiler_params=_cp("arbitrary"),
    )(state, h, h, h, conv_w)


def _softplus(z):
    return jnp.maximum(z, 0.0) + jnp.log(1.0 + jnp.exp(-jnp.abs(z)))


def _suffix_sums(x, upper):
    r = x.shape[0]
    hi = x.astype(BF16)
    lo = (x - hi.astype(F32)).astype(BF16)
    s = jnp.dot(jnp.concatenate([hi, lo], axis=0), upper, preferred_element_type=F32)
    return s[:r] + s[r:]


SB_HEADS_PER_STEP = 4


def _sb_kernel(q_ref, k_ref, v_ref, o_ref, *, tq):
    i = pl.program_id(2)
    scale = HEAD_DIM ** -0.5
    nhs = SB_HEADS_PER_STEP
    hsl = [slice(j * HEAD_DIM, (j + 1) * HEAD_DIM) for j in range(nhs)]
    qs = [q_ref[:, sl].astype(BF16) for sl in hsl]
    rj = lax.broadcasted_iota(jnp.int32, (tq, tq), 0)
    cj = lax.broadcasted_iota(jnp.int32, (tq, tq), 1)
    upper = jnp.where(rj > cj, 1.0, 0.0).astype(BF16)

    before = jnp.concatenate([cj < rj] * nhs, axis=0)

    def chunk(c, carry, diag):
        r_run, accs = carry
        off = pl.multiple_of(c * tq, tq)
        z = jnp.concatenate([_nt(qs[j], k_ref[pl.ds(off, tq), hsl[j]].astype(BF16)) for j in range(nhs)],
                            axis=0) * scale
        sp = _softplus(z)
        if diag:
            sp = jnp.where(before, sp, 0.0)
        a = jnp.exp(z - sp - _suffix_sums(sp, upper) - r_run)
        if diag:
            a = jnp.where(before, a, 0.0)
        ab = a.astype(BF16)
        accs = tuple(accs[j] + jnp.dot(ab[j * tq:(j + 1) * tq], v_ref[pl.ds(off, tq), hsl[j]].astype(BF16),
                                       preferred_element_type=F32) for j in range(nhs))
        return r_run + jnp.sum(sp, axis=1, keepdims=True), accs

    init = (jnp.zeros((nhs * tq, 1), F32), tuple(jnp.zeros((tq, HEAD_DIM), F32) for _ in range(nhs)))
    first = chunk(i, init, True)
    _, accs = lax.fori_loop(0, i, lambda s, carry: chunk(i - 1 - s, carry, False), first)
    for j in range(nhs):
        o_ref[:, hsl[j]] = accs[j].astype(o_ref.dtype)


def _sb_prompt(h, tq=256):
    b, t, _ = h.shape
    nh = (O_K - O_Q) // HEAD_DIM
    tq = min(tq, t)
    wdt = SB_HEADS_PER_STEP * HEAD_DIM
    qb, kb, vb = O_Q // wdt, O_K // wdt, O_V // wdt
    return pl.pallas_call(
        functools.partial(_sb_kernel, tq=tq),
        grid=(b, nh // SB_HEADS_PER_STEP, t // tq),
        in_specs=[pl.BlockSpec((None, tq, wdt), lambda bi, hi, i: (bi, i, qb + hi)),
                  pl.BlockSpec((None, t, wdt), lambda bi, hi, i: (bi, 0, kb + hi)),
                  pl.BlockSpec((None, t, wdt), lambda bi, hi, i: (bi, 0, vb + hi))],
        out_specs=pl.BlockSpec((None, tq, wdt), lambda bi, hi, i: (bi, i, hi)),
        out_shape=jax.ShapeDtypeStruct((b, t, nh * HEAD_DIM), BF16),
        compiler_params=_cp("parallel", "parallel", "parallel"),
        name="sb_prompt",
    )(h, h, h)


def _sb_dec_kernel(pt_ref, q_ref, k_ref, v_ref, o_ref, z_sc, r_sc, acc_sc):
    c = pl.program_id(1)
    scale = HEAD_DIM ** -0.5
    nh = q_ref.shape[0]

    @pl.when(c == 0)
    def _():
        z_sc[...] = jnp.zeros_like(z_sc)
        r_sc[...] = jnp.zeros_like(r_sc)
        acc_sc[...] = jnp.zeros_like(acc_sc)

    rj = lax.broadcasted_iota(jnp.int32, (PAGE_SIZE, PAGE_SIZE), 0)
    cj = lax.broadcasted_iota(jnp.int32, (PAGE_SIZE, PAGE_SIZE), 1)
    upper = jnp.where(rj > cj, 1.0, 0.0).astype(BF16)
    hrow = lax.broadcasted_iota(jnp.int32, (nh, HEAD_DIM), 0)

    def heads_on_lanes(ref):
        return jnp.concatenate([ref[pl.ds(j, PAGE_SIZE, stride=nh), :].astype(BF16) for j in range(nh)], axis=1)

    live = c > 0
    z = z_sc[...]
    sp = jnp.where(live, _softplus(z), 0.0)
    a = jnp.where(live, jnp.exp(z - sp - _suffix_sums(sp, upper) - r_sc[...]), 0.0)
    r_sc[...] += jnp.sum(sp, axis=1, keepdims=True)

    q = q_ref[...]
    q_bd = jnp.concatenate([jnp.where(hrow == j, q, 0.0) for j in range(nh)], axis=1).astype(BF16)
    z_sc[...] = _nt(q_bd, heads_on_lanes(k_ref)) * scale

    res = jnp.dot(a.astype(BF16), heads_on_lanes(v_ref), preferred_element_type=F32)
    upd = jnp.zeros((nh, HEAD_DIM), F32)
    for j in range(nh):
        upd = upd + jnp.where(hrow == j, res[:, j * HEAD_DIM:(j + 1) * HEAD_DIM], 0.0)
    acc_sc[...] += upd

    @pl.when(c == pl.num_programs(1) - 1)
    def _():
        o_ref[...] = acc_sc[...].astype(o_ref.dtype)


def _sb_dec(q, k_pool, v_pool, layer, page_table):
    b, nh, _ = q.shape
    n_pages = page_table.shape[1]
    rows = PAGE_SIZE * nh
    k_pool = k_pool.reshape(k_pool.shape[:2] + (rows, HEAD_DIM))
    v_pool = v_pool.reshape(v_pool.shape[:2] + (rows, HEAD_DIM))

    def k_page(bi, c, pt):
        return (layer, pt[bi, n_pages - 1 - jnp.minimum(c, n_pages - 1)], 0, 0)

    def v_page(bi, c, pt):
        return (layer, pt[bi, n_pages - 1 - jnp.maximum(c - 1, 0)], 0, 0)

    grid_spec = pltpu.PrefetchScalarGridSpec(
        num_scalar_prefetch=1,
        grid=(b, n_pages + 1),
        in_specs=[pl.BlockSpec((None, nh, HEAD_DIM), lambda bi, c, pt: (bi, 0, 0)),
                  pl.BlockSpec((None, None, rows, HEAD_DIM), k_page),
                  pl.BlockSpec((None, None, rows, HEAD_DIM), v_page)],
        out_specs=pl.BlockSpec((None, nh, HEAD_DIM), lambda bi, c, pt: (bi, 0, 0)),
        scratch_shapes=[pltpu.VMEM((nh, PAGE_SIZE), F32), pltpu.VMEM((nh, 1), F32), pltpu.VMEM((nh, HEAD_DIM), F32)],
    )
    return pl.pallas_call(
        _sb_dec_kernel,
        grid_spec=grid_spec,
        out_shape=jax.ShapeDtypeStruct((b, nh, HEAD_DIM), BF16),
        compiler_params=_cp("parallel", "arbitrary"),
        name="sb_decode",
    )(page_table, q, k_pool, v_pool)


def _chunk_sums(x, w0, w1):
    x3 = x.reshape(x.shape[0] // CMP_STRIDE, CMP_STRIDE, x.shape[1])
    return jnp.sum(x3 * w0[None], axis=1), jnp.sum(x3 * w1[None], axis=1)


def _compress_kernel(kc_ref, vc_ref, wpos_ref, wlin_ref, ko_ref, vo_ref):
    nchunk = kc_ref.shape[0] // CMP_STRIDE
    for idx, (src, dst) in enumerate(((kc_ref, ko_ref), (vc_ref, vo_ref))):
        a, bsum = _chunk_sums(src[...], wpos_ref[idx, 0:CMP_STRIDE, :], wpos_ref[idx, CMP_STRIDE:, :])
        pooled = a + pltpu.roll(bsum, nchunk - 1, 0)
        for h in range(NSA_KV_HEADS):
            sl = slice(h * HEAD_DIM, (h + 1) * HEAD_DIM)
            dst[:, sl] = jnp.dot(pooled[:, sl].astype(BF16), wlin_ref[idx, h], preferred_element_type=F32)


def _compress_prompt(h, wpos, wlin):
    b, t, _ = h.shape
    kvw = NSA_KV_HEADS * HEAD_DIM
    nchunk = t // CMP_STRIDE
    out = jax.ShapeDtypeStruct((b, nchunk, kvw), F32)
    ospec = pl.BlockSpec((None, nchunk, kvw), lambda i: (i, 0, 0))
    return pl.pallas_call(
        _compress_kernel,
        name="compress_prompt",
        grid=(b,),
        in_specs=[pl.BlockSpec((None, t, kvw), lambda i: (i, 0, E_KC // kvw)),
                  pl.BlockSpec((None, t, kvw), lambda i: (i, 0, E_VC // kvw)),
                  pl.BlockSpec(wpos.shape, lambda i: (0, 0, 0)),
                  pl.BlockSpec(wlin.shape, lambda i: (0, 0, 0, 0))],
        out_specs=[ospec, ospec],
        out_shape=[out, out],
        compiler_params=_cp("parallel"),
    )(h, h, wpos, wlin)


def _topk_keep(score, k_top):
    srow = lax.broadcasted_iota(jnp.int32, score.shape, 0)
    rank = jnp.zeros(score.shape, jnp.int32)
    for s2 in range(score.shape[0]):
        row = score[s2:s2 + 1, :]
        rank = rank + jnp.where(row > score, 1, jnp.where(row == score, jnp.where(srow > s2, 1, 0), 0))
    return jnp.where(rank < k_top, jnp.where(score > 0.5 * NEG_INF, 1.0, 0.0), 0.0)


def _flash_init(m_sc, l_sc, acc_sc):
    m_sc[...] = jnp.full(m_sc.shape, NEG_INF, F32)
    l_sc[...] = jnp.zeros(l_sc.shape, F32)
    acc_sc[...] = jnp.zeros(acc_sc.shape, F32)


def _flash_chunk(q_all, k, v, bias_fn, mask, m_sc, l_sc, acc_sc):
    ng = NSA_GROUP
    r = q_all.shape[0] // ng
    s_all = _nt(k.astype(BF16), q_all) * (HEAD_DIM ** -0.5)
    v_t = v.T.astype(BF16)
    m_prev = m_sc[...]
    m_parts, p_parts = [], []
    for g in range(ng):
        sl = slice(g * r, (g + 1) * r)
        s = s_all[:, sl] + bias_fn(g)
        if mask is not None:
            s = jnp.where(mask, s, NEG_INF)
        mn = jnp.maximum(m_prev[:, sl], jnp.max(s, axis=0, keepdims=True))
        p_parts.append(jnp.exp(s - mn))
        m_parts.append(mn)
    m_new = jnp.concatenate(m_parts, axis=1)
    p = jnp.concatenate(p_parts, axis=1)
    alpha = jnp.exp(m_prev - m_new)
    l_sc[...] = alpha * l_sc[...] + jnp.sum(p, axis=0, keepdims=True)
    acc_sc[...] = alpha * acc_sc[...] + jnp.dot(v_t, p.astype(BF16), preferred_element_type=F32)
    m_sc[...] = m_new


def _nsa_kernel(b31_ref, q_ref, g_ref, kc_ref, vc_ref, ks_ref, vs_ref, kvw_ref, tabc_ref, tabd_ref, o_ref,
                m_sc, l_sc, acc_sc, oc_sc, os_sc, keep_sc):
    i = pl.program_id(1)
    qb = q_ref.shape[0]
    t_len = ks_ref.shape[0]
    nc = kc_ref.shape[0]
    n_cmp = (t_len - CMP_BLOCK) // CMP_STRIDE + 1
    n_slc = t_len // SEL_BLOCK
    k_top = min(N_SEL, n_slc)
    gsz = NSA_GROUP
    scale = HEAD_DIM ** -0.5
    keyi = lax.broadcasted_iota(jnp.int32, (qb, qb), 0)
    qi = lax.broadcasted_iota(jnp.int32, (qb, qb), 1)
    causal = keyi <= qi
    wedge = keyi >= qi
    first_half = keyi < SEL_BLOCK
    pos_row = i * qb + lax.broadcasted_iota(jnp.int32, (1, qb), 1)
    sig_t = jax.nn.sigmoid(g_ref[...]).T
    osb = lax.broadcasted_iota(jnp.int32, (n_slc, nc), 0)
    on = lax.broadcasted_iota(jnp.int32, (n_slc, nc), 1)
    c_start = on * CMP_STRIDE
    overlap_t = jnp.where((c_start < osb * SEL_BLOCK + SEL_BLOCK) & (c_start + CMP_BLOCK - 1 >= osb * SEL_BLOCK)
                          & (on < n_cmp), 1.0, 0.0).astype(BF16)
    nrow = lax.broadcasted_iota(jnp.int32, (nc, qb), 0)
    cmask = ((nrow * CMP_STRIDE + CMP_BLOCK - 1) <= pos_row) & (nrow < n_cmp)
    srow = lax.broadcasted_iota(jnp.int32, (n_slc, qb), 0)
    cur = pos_row >> SEL_SHIFT
    forced = (srow == 0) | (srow == cur) | (srow == cur - 1)
    future = srow > cur

    for h in range(NSA_KV_HEADS):
        hs = slice(h * HEAD_DIM, (h + 1) * HEAD_DIM)
        g0 = h * gsz
        q_all = jnp.concatenate(
            [q_ref[:, (g0 + g) * HEAD_DIM:(g0 + g + 1) * HEAD_DIM] for g in range(gsz)], axis=0).astype(BF16)

        s_all = _nt(kc_ref[:, hs].astype(BF16), q_all) * scale
        vc_t = vc_ref[:, hs].T.astype(BF16)
        pcs = jnp.zeros((nc, qb), F32)
        pc_parts = []
        for g in range(gsz):
            s = s_all[:, g * qb:(g + 1) * qb] + tabc_ref[g0 + g]
            s = jnp.where(cmask, s, NEG_INF)
            s = s - jnp.max(s, axis=0, keepdims=True)
            e = jnp.exp(s)
            pc = jnp.where(cmask, e / jnp.sum(e, axis=0, keepdims=True), 0.0)
            pcs = pcs + pc
            pc_parts.append(pc.astype(BF16))
        oc_sc[...] = jnp.dot(vc_t, jnp.concatenate(pc_parts, axis=1), preferred_element_type=F32)
        imp = sum(jnp.dot(overlap_t, piece, preferred_element_type=F32) for piece in _split3(pcs))
        score = jnp.where(future, NEG_INF, jnp.where(forced, BIG, imp))
        keep_sc[...] = _topk_keep(score, k_top)

        def sel_mask(c, nk=1):
            parts = []
            for u in range(nk):
                first = keep_sc[pl.ds(2 * (c + u), 1), :]
                second = keep_sc[pl.ds(2 * (c + u) + 1, 1), :]
                parts.append(jnp.where(first_half, first, second))
            return jnp.concatenate(parts, axis=0) > 0.5

        def chunk(k_ref, v_ref, c, ksl, vsl, bias_fn, mask, nk=1):
            off = pl.multiple_of(c * qb, qb)
            _flash_chunk(q_all, k_ref[pl.ds(off, nk * qb), ksl], v_ref[pl.ds(off, nk * qb), vsl], bias_fn, mask,
                         m_sc, l_sc, acc_sc)

        tab0 = lambda g: tabd_ref[g0 + g, 0]
        tab1 = lambda g: tabd_ref[g0 + g, 1]
        far = lambda g: b31_ref[g0 + g]

        _flash_init(m_sc, l_sc, acc_sc)
        chunk(ks_ref, vs_ref, i, hs, hs, tab0, sel_mask(i) & causal)

        @pl.when(i >= 1)
        def _():
            chunk(ks_ref, vs_ref, i - 1, hs, hs, tab1, sel_mask(i - 1))

        n_far = jnp.maximum(i - 1, 0)

        def far_body(pair, carry):
            chunk(ks_ref, vs_ref, 2 * pair, hs, hs, far, sel_mask(2 * pair, 2), 2)
            return carry

        lax.fori_loop(0, n_far // 2, far_body, 0)

        @pl.when(n_far % 2 == 1)
        def _():
            chunk(ks_ref, vs_ref, n_far - 1, hs, hs, far, sel_mask(n_far - 1))

        os_sc[...] = acc_sc[...] / l_sc[...]

        kw_sl = hs
        vw_sl = slice(NSA_KV_HEADS * HEAD_DIM + h * HEAD_DIM, NSA_KV_HEADS * HEAD_DIM + (h + 1) * HEAD_DIM)
        _flash_init(m_sc, l_sc, acc_sc)
        chunk(kvw_ref, kvw_ref, i, kw_sl, vw_sl, tab0, causal)

        @pl.when(i >= 1)
        def _():
            chunk(kvw_ref, kvw_ref, i - 1, kw_sl, vw_sl, tab1, None)

        @pl.when(i >= 3)
        def _():
            chunk(kvw_ref, kvw_ref, i - 3, kw_sl, vw_sl, far, None, 2)

        @pl.when(i == 2)
        def _():
            chunk(kvw_ref, kvw_ref, 0, kw_sl, vw_sl, far, None)

        @pl.when(i >= 4)
        def _():
            chunk(kvw_ref, kvw_ref, i - 4, kw_sl, vw_sl, far, wedge)

        ow = acc_sc[...] / l_sc[...]

        def gate(j):
            return jnp.concatenate([sig_t[(g0 + g) * 3 + j:(g0 + g) * 3 + j + 1, :] for g in range(gsz)], axis=1)

        y_t = gate(0) * oc_sc[...] + gate(1) * os_sc[...] + gate(2) * ow
        for g in range(gsz):
            o_ref[:, (g0 + g) * HEAD_DIM:(g0 + g + 1) * HEAD_DIM] = y_t[:, g * qb:(g + 1) * qb].T.astype(o_ref.dtype)


def _nsa_prompt(h, kcmp, vcmp, tabc, tabd, b31, qb=128):
    b, t, _ = h.shape
    assert qb == HEAD_DIM == 2 * SEL_BLOCK and WINDOW == 4 * qb and t % qb == 0
    qw = NSA_HEADS * HEAD_DIM
    kvw = NSA_KV_HEADS * HEAD_DIM
    nchunk = kcmp.shape[1]
    gq = NSA_GROUP * qb
    return pl.pallas_call(
        _nsa_kernel,
        grid=(b, t // qb),
        in_specs=[pl.BlockSpec(memory_space=pltpu.SMEM),
                  pl.BlockSpec((None, qb, qw), lambda bi, i: (bi, i, E_Q // qw)),
                  pl.BlockSpec((None, qb, LANES), lambda bi, i: (bi, i, E_G // LANES)),
                  pl.BlockSpec((None, nchunk, kvw), lambda bi, i: (bi, 0, 0)),
                  pl.BlockSpec((None, nchunk, kvw), lambda bi, i: (bi, 0, 0)),
                  pl.BlockSpec((None, t, kvw), lambda bi, i: (bi, 0, E_KS // kvw)),
                  pl.BlockSpec((None, t, kvw), lambda bi, i: (bi, 0, E_VS // kvw)),
                  pl.BlockSpec((None, t, 2 * kvw), lambda bi, i: (bi, 0, E_KVW // (2 * kvw))),
                  pl.BlockSpec((None, NSA_HEADS, nchunk, qb), lambda bi, i: (i, 0, 0, 0)),
                  pl.BlockSpec((NSA_HEADS, 2, qb, qb), lambda bi, i: (0, 0, 0, 0))],
        out_specs=pl.BlockSpec((None, qb, qw), lambda bi, i: (bi, i, 0)),
        out_shape=jax.ShapeDtypeStruct((b, t, qw), BF16),
        scratch_shapes=[pltpu.VMEM((1, gq), F32), pltpu.VMEM((1, gq), F32),
                        pltpu.VMEM((HEAD_DIM, gq), F32), pltpu.VMEM((HEAD_DIM, gq), F32),
                        pltpu.VMEM((HEAD_DIM, gq), F32), pltpu.VMEM((t // SEL_BLOCK, qb), F32)],
        compiler_params=_cp("parallel", "parallel"),
        name="nsa_prompt",
    )(b31, h, h, kcmp, vcmp, h, h, h, tabc, tabd)


def _cmp_dec_kernel(pt_ref, *refs, n_pg):
    k_refs = refs[:n_pg]
    v_refs = refs[n_pg:2 * n_pg]
    wpos_ref = refs[2 * n_pg]
    ak_ref, bk_ref, av_ref, bv_ref = refs[2 * n_pg + 1:]
    per = PAGE_SIZE // CMP_STRIDE
    for idx, (srcs, a_ref, b_ref) in enumerate(((k_refs, ak_ref, bk_ref), (v_refs, av_ref, bv_ref))):
        for r in range(n_pg):
            for h in range(NSA_KV_HEADS):
                sl = slice(h * HEAD_DIM, (h + 1) * HEAD_DIM)
                x = srcs[r][pl.ds(h, PAGE_SIZE, stride=NSA_KV_HEADS), :]
                a, bsum = _chunk_sums(x, wpos_ref[idx, 0:CMP_STRIDE, sl], wpos_ref[idx, CMP_STRIDE:, sl])
                a_ref[r * per:(r + 1) * per, sl] = a
                b_ref[r * per:(r + 1) * per, sl] = bsum


def _cmp_dec(k_pool, v_pool, layer, page_table, wpos, n_pg=8):
    b, n_pages = page_table.shape
    kvw = NSA_KV_HEADS * HEAD_DIM
    per = PAGE_SIZE // CMP_STRIDE
    rows = PAGE_SIZE * NSA_KV_HEADS
    k_pool = k_pool.reshape(k_pool.shape[:2] + (rows, HEAD_DIM))
    v_pool = v_pool.reshape(v_pool.shape[:2] + (rows, HEAD_DIM))

    def page(r):
        return pl.BlockSpec((None, None, rows, HEAD_DIM), lambda bi, c, pt: (layer, pt[bi, c * n_pg + r], 0, 0))

    ospec = pl.BlockSpec((None, n_pg * per, kvw), lambda bi, c, pt: (bi, c, 0))
    out = jax.ShapeDtypeStruct((b, n_pages * per, kvw), F32)
    grid_spec = pltpu.PrefetchScalarGridSpec(
        num_scalar_prefetch=1,
        grid=(b, n_pages // n_pg),
        in_specs=[page(r) for r in range(n_pg)] * 2 + [pl.BlockSpec(wpos.shape, lambda bi, c, pt: (0, 0, 0))],
        out_specs=[ospec] * 4,
    )
    return pl.pallas_call(
        functools.partial(_cmp_dec_kernel, n_pg=n_pg),
        grid_spec=grid_spec,
        out_shape=[out] * 4,
        compiler_params=_cp("parallel", "parallel"),
        name="cmp_decode",
    )(page_table, *([k_pool] * n_pg), *([v_pool] * n_pg), wpos)


def _nsa_dec_cmp_kernel(q_ref, ak_ref, bk_ref, av_ref, bv_ref, wlin_ref, bias_ref, oc_ref, idx_ref,
                        *, pos, n_cmp, n_slc):
    nrow = ak_ref.shape[0]
    sw = idx_ref.shape[-1]
    lane_n = lax.broadcasted_iota(jnp.int32, (1, nrow), 1)
    cmask = ((lane_n * CMP_STRIDE + CMP_BLOCK - 1) <= pos) & (lane_n < n_cmp)
    on = lax.broadcasted_iota(jnp.int32, (nrow, sw), 0)
    osb = lax.broadcasted_iota(jnp.int32, (nrow, sw), 1)
    c_start = on * CMP_STRIDE
    overlap = jnp.where((c_start < osb * SEL_BLOCK + SEL_BLOCK) & (c_start + CMP_BLOCK - 1 >= osb * SEL_BLOCK)
                        & (on < n_cmp) & (osb < n_slc), 1.0, 0.0).astype(BF16)
    pk = ak_ref[...] + pltpu.roll(bk_ref[...], nrow - 1, 0)
    pv = av_ref[...] + pltpu.roll(bv_ref[...], nrow - 1, 0)
    lane_s = lax.broadcasted_iota(jnp.int32, (1, sw), 1)
    ri = lax.broadcasted_iota(jnp.int32, (sw, sw), 0)
    ci = lax.broadcasted_iota(jnp.int32, (sw, sw), 1)
    k_top = min(N_SEL, n_slc)
    cur = pos // SEL_BLOCK
    for h in range(NSA_KV_HEADS):
        sl = slice(h * HEAD_DIM, (h + 1) * HEAD_DIM)
        kc = jnp.dot(pk[:, sl].astype(BF16), wlin_ref[0, h], preferred_element_type=F32).astype(BF16)
        vc = jnp.dot(pv[:, sl].astype(BF16), wlin_ref[1, h], preferred_element_type=F32).astype(BF16)
        q = q_ref[h].astype(BF16)
        s = _nt(q, kc) * (HEAD_DIM ** -0.5) + bias_ref[h * NSA_GROUP:(h + 1) * NSA_GROUP]
        s = jnp.where(cmask, s, NEG_INF)
        s = s - jnp.max(s, axis=-1, keepdims=True)
        e = jnp.exp(s)
        pc = jnp.where(cmask, e / jnp.sum(e, axis=-1, keepdims=True), 0.0)
        oc_ref[h] = jnp.dot(pc.astype(BF16), vc, preferred_element_type=F32)
        pcs = jnp.sum(pc, axis=0, keepdims=True)
        hi, mid, lo = _split3(jnp.broadcast_to(pcs, (8, nrow)))
        imp = (jnp.dot(hi, overlap, preferred_element_type=F32) + jnp.dot(mid, overlap, preferred_element_type=F32)
               + jnp.dot(lo, overlap, preferred_element_type=F32))[0:1]
        forced = (lane_s == 0) | (lane_s == cur) | (lane_s == cur - 1)
        score = jnp.where(lane_s > cur, NEG_INF, jnp.where(forced, BIG, imp))
        score = jnp.where(lane_s < n_slc, score, -jnp.inf)
        rowm = jnp.broadcast_to(score, (sw, sw))
        colm = rowm.T
        beats = jnp.where(colm > rowm, 1, jnp.where(colm == rowm, jnp.where(ri < ci, 1, 0), 0))
        beats = jnp.where(ri < n_slc, beats, 0)
        rank = jnp.sum(beats, axis=0, keepdims=True)
        keep = (rank < k_top) & (score > 0.5 * NEG_INF) & (lane_s < n_slc)
        out = jnp.full((1, sw), -1, jnp.int32)
        for r in range(k_top):
            hit = keep & (rank == r)
            val = jnp.sum(jnp.where(hit, lane_s + 1, 0), axis=1, keepdims=True) - 1
            out = jnp.where(lane_s == r, val, out)
        idx_ref[h] = out


def _nsa_dec_cmp(q, ak, bk, av, bv, wlin, bias_c, pos, n_cmp, n_slc):
    b = q.shape[0]
    nrow = ak.shape[1]
    kvw = ak.shape[2]
    sw = -(-n_slc // LANES) * LANES
    part = pl.BlockSpec((None, nrow, kvw), lambda i: (i, 0, 0))
    kern = functools.partial(_nsa_dec_cmp_kernel, pos=pos, n_cmp=n_cmp, n_slc=n_slc)
    return pl.pallas_call(
        kern,
        name="nsa_dec_cmp",
        grid=(b,),
        in_specs=[pl.BlockSpec((None,) + q.shape[1:], lambda i: (i, 0, 0, 0)), part, part, part, part,
                  pl.BlockSpec(wlin.shape, lambda i: (0, 0, 0, 0)),
                  pl.BlockSpec(bias_c.shape, lambda i: (0, 0))],
        out_specs=[pl.BlockSpec((None,) + q.shape[1:], lambda i: (i, 0, 0, 0)),
                   pl.BlockSpec((None, NSA_KV_HEADS, 1, sw), lambda i: (i, 0, 0, 0))],
        out_shape=[jax.ShapeDtypeStruct(q.shape, F32), jax.ShapeDtypeStruct((b, NSA_KV_HEADS, 1, sw), jnp.int32)],
        compiler_params=_cp("parallel"),
    )(q, ak, bk, av, bv, wlin, bias_c)


def _bucket_bias(dist, rb):
    n = jnp.maximum(dist, 0)
    max_exact = N_BUCKETS // 2
    nf = jnp.maximum(n, 1).astype(F32)
    large = max_exact + (jnp.log(nf / max_exact) / math.log(MAX_DISTANCE / max_exact)
                         * (N_BUCKETS - max_exact)).astype(jnp.int32)
    bucket = jnp.where(n < max_exact, n, jnp.minimum(large, N_BUCKETS - 1))
    out = jnp.zeros((rb.shape[0], dist.shape[1]), F32)
    for k in range(N_BUCKETS):
        out = jnp.where(bucket == k, rb[:, k:k + 1], out)
    return out


def _nsa_dec_sel_kernel(pg_ref, hf_ref, blk_ref, q_ref, kn_ref, vn_ref, rb_ref, k0_ref, k1_ref, v0_ref, v1_ref,
                        o_ref, m_sc, l_sc, acc_sc, *, pos):
    bi = pl.program_id(0)
    j = pl.program_id(1)
    scale = HEAD_DIM ** -0.5
    lane = lax.broadcasted_iota(jnp.int32, (1, SEL_BLOCK), 1)

    @pl.when(j == 0)
    def _():
        for h in range(NSA_KV_HEADS):
            q = q_ref[h]
            rb = rb_ref[h]
            s = jnp.sum(q.astype(BF16).astype(F32) * kn_ref[h:h + 1, :].astype(BF16).astype(F32),
                        axis=-1, keepdims=True) * scale + rb[:, 0:1]
            m_sc[h] = s
            l_sc[h] = jnp.ones_like(s)
            acc_sc[h] = jnp.broadcast_to(vn_ref[h:h + 1, :].astype(BF16).astype(F32), (q.shape[0], HEAD_DIM))

    for h, (k_ref, v_ref) in enumerate(((k0_ref, v0_ref), (k1_ref, v1_ref))):
        blk = blk_ref[bi, h * pl.num_programs(1) + j]

        @pl.when(blk >= 0)
        def _(h=h, k_ref=k_ref, v_ref=v_ref, blk=blk):
            q = q_ref[h].astype(BF16)
            k = k_ref[:, h, :].astype(BF16)
            v = v_ref[:, h, :].astype(BF16)
            tok = blk * SEL_BLOCK + lane
            ok = tok <= pos
            s = _nt(q, k) * scale + _bucket_bias(pos - tok, rb_ref[h])
            s = jnp.where(ok, s, NEG_INF)
            m_prev = m_sc[h]
            m_new = jnp.maximum(m_prev, jnp.max(s, axis=-1, keepdims=True))
            alpha = jnp.exp(m_prev - m_new)
            p = jnp.where(ok, jnp.exp(s - m_new), 0.0)
            l_sc[h] = alpha * l_sc[h] + jnp.sum(p, axis=-1, keepdims=True)
            acc_sc[h] = alpha * acc_sc[h] + jnp.dot(p.astype(BF16), v, preferred_element_type=F32)
            m_sc[h] = m_new

    @pl.when(j == pl.num_programs(1) - 1)
    def _():
        for h in range(NSA_KV_HEADS):
            o_ref[h] = acc_sc[h] / l_sc[h]


def _nsa_dec_sel(q, k_new, v_new, rb, k_pool, v_pool, layer, pages, halves, blocks, pos):
    b = q.shape[0]
    k_top = blocks.shape[-1]
    pages, halves, blocks = (a.reshape(b, NSA_KV_HEADS * k_top) for a in (pages, halves, blocks))

    def blkspec(h):
        return pl.BlockSpec((None, None, SEL_BLOCK, NSA_KV_HEADS, HEAD_DIM),
                            lambda bi, j, pg, hf, bl: (layer, pg[bi, h * k_top + j], hf[bi, h * k_top + j], 0, 0))

    qspec = pl.BlockSpec((None,) + q.shape[1:], lambda bi, j, pg, hf, bl: (bi, 0, 0, 0))
    nspec = pl.BlockSpec((None, NSA_KV_HEADS, HEAD_DIM), lambda bi, j, pg, hf, bl: (bi, 0, 0))
    grid_spec = pltpu.PrefetchScalarGridSpec(
        num_scalar_prefetch=3,
        grid=(b, k_top),
        in_specs=[qspec, nspec, nspec, pl.BlockSpec(rb.shape, lambda bi, j, pg, hf, bl: (0, 0, 0)),
                  blkspec(0), blkspec(1), blkspec(0), blkspec(1)],
        out_specs=qspec,
        scratch_shapes=[pltpu.VMEM((NSA_KV_HEADS, NSA_GROUP, 1), F32), pltpu.VMEM((NSA_KV_HEADS, NSA_GROUP, 1), F32),
                        pltpu.VMEM((NSA_KV_HEADS, NSA_GROUP, HEAD_DIM), F32)],
    )
    return pl.pallas_call(
        functools.partial(_nsa_dec_sel_kernel, pos=pos),
        name="nsa_dec_sel",
        grid_spec=grid_spec,
        out_shape=jax.ShapeDtypeStruct(q.shape, F32),
        compiler_params=_cp("parallel", "arbitrary"),
    )(pages, halves, blocks, q, k_new, v_new, rb, k_pool, k_pool, v_pool, v_pool)


def _nsa_dec_win_kernel(q_ref, g_ref, win_ref, new_ref, bias_ref, oc_ref, os_ref, y_ref, nw_ref):
    scale = HEAD_DIM ** -0.5
    nwin = win_ref.shape[0]
    sig = jax.nn.sigmoid(g_ref[...])
    for h in range(NSA_KV_HEADS):
        q = q_ref[h].astype(BF16)
        k = win_ref[:, 0, h, :].astype(BF16)
        v = win_ref[:, 1, h, :].astype(BF16)
        kn = new_ref[0, h:h + 1, :].astype(BF16)
        vn = new_ref[1, h:h + 1, :].astype(BF16)
        bias = bias_ref[h * NSA_GROUP:(h + 1) * NSA_GROUP]
        s = _nt(q, k) * scale + bias[:, :nwin]
        s_new = jnp.sum(q.astype(F32) * kn.astype(F32), axis=-1, keepdims=True) * scale + bias[:, nwin:nwin + 1]
        m = jnp.maximum(jnp.max(s, axis=-1, keepdims=True), s_new)
        p = jnp.exp(s - m)
        p_new = jnp.exp(s_new - m)
        den = jnp.sum(p, axis=-1, keepdims=True) + p_new
        ow = (jnp.dot(p.astype(BF16), v, preferred_element_type=F32)
              + p_new.astype(BF16).astype(F32) * vn.astype(F32)) / den
        gs = sig[h]
        y_ref[h] = (gs[:, 0:1] * oc_ref[h] + gs[:, 1:2] * os_ref[h] + gs[:, 2:3] * ow).astype(y_ref.dtype)
    nw_ref[pl.ds(0, nwin - 1)] = win_ref[pl.ds(1, nwin - 1)]
    nw_ref[nwin - 1] = new_ref[...]


def _nsa_dec_win(q, gates, win, new_kv, bias_w, o_c, o_s):
    b = q.shape[0]
    qspec = pl.BlockSpec((None,) + q.shape[1:], lambda i: (i, 0, 0, 0))
    wspec = pl.BlockSpec((None,) + win.shape[1:], lambda i: (i, 0, 0, 0, 0))
    bias2 = bias_w[:, 0, :]
    return pl.pallas_call(
        _nsa_dec_win_kernel,
        name="nsa_dec_win",
        grid=(b,),
        in_specs=[qspec, pl.BlockSpec((None,) + gates.shape[1:], lambda i: (i, 0, 0, 0)), wspec,
                  pl.BlockSpec((None,) + new_kv.shape[1:], lambda i: (i, 0, 0, 0)),
                  pl.BlockSpec(bias2.shape, lambda i: (0, 0)), qspec, qspec],
        out_specs=[qspec, wspec],
        out_shape=[jax.ShapeDtypeStruct(q.shape, BF16), jax.ShapeDtypeStruct(win.shape, F32)],
        compiler_params=_cp("parallel"),
    )(q, gates, win, new_kv, bias2, o_c, o_s)


def _tail(x, xb_unused, y_mix, mem_kv, ffn_prev, p, l, bsz, t, alpha):
    d = x.shape[1]
    x1, x1b = _add_ln(x, y_mix, p['ln_g'][l, 0], p['ln_b'][l, 0], alpha)
    tm = 1024
    q = _mm(x1b, p['w_cq'][l], F32, tm, 512, "mm_cq")
    o = _cross(q.reshape(bsz, t, -1), mem_kv, 512).reshape(bsz * t, -1)
    yc = _mm(o, p['w_co'][l], F32, tm, 512, "mm_co")
    x2, x2b = _add_ln(x1, yc, p['ln_g'][l, 1], p['ln_b'][l, 1], alpha)
    if t > 1:
        act, st1, st2 = _ffn_up(x2b, p['w_up'][l], p['ffn_conv'][l], t, 1024)
        ffn_new = jnp.concatenate([st1[:, 6:], st2[:, 6:]], axis=-1)
    else:
        act, ffn_new = _ffn_up_dec(x2b, p['w_up'][l], p['ffn_conv'][l], ffn_prev)
    dff = act.shape[1]
    f = _mm_acc(act, p['w_down'][l], F32, 1024, 512, dff // 2)
    x3, x3b = _add_ln(x2, f, p['ln_g'][l, 2], p['ln_b'][l, 2], alpha)
    return x3, x3b, ffn_new


def kernel(x_prompt, x_sample, mem_prompt, state_pool, cache_nsa_cmp_k, cache_nsa_cmp_v, cache_nsa_sel_k, cache_nsa_sel_v, state_nsa_win, state_sc, cache_sb_k, cache_sb_v, state_ffn, cache_mem, page_table, w_in_even, w_pool, pool_scale, w_cmp_pos, w_cmp_lin, rel_bias, w_out_even, w_in_odd, sc_conv, w_out_odd, w_cq, w_ckv, w_co, w_up, ffn_conv, w_down, ln_g, ln_b):
    bp, t, d = x_prompt.shape
    bs = x_sample.shape[0]
    depth = w_cq.shape[0]
    n_pages = page_table.shape[1]
    past = n_pages * PAGE_SIZE
    alpha = (2.0 * depth) ** 0.25
    kvw = NSA_KV_HEADS * HEAD_DIM
    assert x_sample.shape[1] == 1 and state_nsa_win.shape[2] == WINDOW

    we = w_in_even
    w_even = jnp.concatenate([we[:, :, 1024:4096], we[:, :, :1024], we[:, :, 4096:],
                              jnp.zeros(we.shape[:2] + (E_TOT - we.shape[2],), we.dtype)], axis=2).astype(BF16)
    p = {'w_cq': w_cq.astype(BF16), 'w_co': w_co.astype(BF16), 'w_up': w_up.astype(BF16),
         'w_down': w_down.astype(BF16), 'ffn_conv': ffn_conv, 'ln_g': ln_g, 'ln_b': ln_b}
    w_odd = w_in_odd.astype(BF16)
    w_oe = w_out_even.astype(BF16)
    w_oo = w_out_odd.astype(BF16)
    w_kv = w_ckv.astype(BF16)
    w_pool_b = w_pool.astype(BF16)
    w_lin_b = w_cmp_lin.astype(BF16)
    w_pos = w_cmp_pos.reshape(w_cmp_pos.shape[0], 2, CMP_BLOCK, kvw)

    qb = 128
    kj = np.arange(qb)[:, None]
    qt = np.arange(qb)[None, :]
    dist_d = np.stack([qt - kj, qb + qt - kj]).reshape(2 * qb, qb).astype(np.int32)
    tabd = _bias_lookup(rel_bias, jnp.asarray(dist_d), 2 * qb).reshape(NSA_HEADS, 2, qb, qb)
    nchunk = t // CMP_STRIDE
    dist_c = (np.arange(t)[None, :] - (np.arange(nchunk)[:, None] * CMP_STRIDE + CMP_BLOCK - 1)).astype(np.int32)
    tabc = _bias_lookup_blocked(rel_bias, jnp.asarray(dist_c), qb)
    b31 = rel_bias[N_BUCKETS - 1]

    mem_b = mem_prompt.reshape(bp * mem_prompt.shape[1], d).astype(BF16)
    n_mem = mem_prompt.shape[1]

    x = x_prompt.reshape(bp * t, d)
    xb = x.astype(BF16)
    outs_p = {k: [] for k in ('pool', 'cmp_k', 'cmp_v', 'sel_k', 'sel_v', 'win', 'sc', 'sb_k', 'sb_v', 'ffn', 'mem')}
    for l in range(depth):
        e = l // 2
        memkv = _mm(mem_b, w_kv[l], F32, 1024, 512)
        outs_p['mem'].append(memkv.reshape(bp, n_mem, 2, MEM_HEADS, MEM_HEAD_DIM))
        if l % 2 == 0:
            h = _mm(xb, w_even[e], F32, 1024, 640, "mm_in_even").reshape(bp, t, E_TOT)
            y_pool = _pool_prompt(h, E_U // 1024, w_pool_b[e], pool_scale[e])
            kcmp, vcmp = _compress_prompt(h, w_pos[e], w_lin_b[e])
            y_nsa = _nsa_prompt(h, kcmp, vcmp, tabc, tabd, b31)
            mix = jnp.concatenate([y_pool, y_nsa], axis=-1).reshape(bp * t, d)
            y = _mm(mix, w_oe[e], F32, 1024, 512, "mm_out")
            outs_p['pool'].append(h[:, t - POOL_STATE:, E_U:E_U + 1024])
            for name, off in (('cmp_k', E_KC), ('cmp_v', E_VC), ('sel_k', E_KS), ('sel_v', E_VS)):
                outs_p[name].append(h[:, :, off:off + kvw].reshape(bp, t, NSA_KV_HEADS, HEAD_DIM))
            nw = min(WINDOW, t)
            outs_p['win'].append(h[:, t - nw:, E_KVW:E_KVW + 2 * kvw].reshape(bp, nw, 2, NSA_KV_HEADS, HEAD_DIM))
        else:
            h = _mm(xb, w_odd[e], F32, 1024, 512, "mm_in_odd").reshape(bp, t, -1)
            y_sc, sc_st = _sconv_prompt(h, sc_conv[e])
            y_sb = _sb_prompt(h)
            mix = jnp.concatenate([y_sc, y_sb], axis=-1).reshape(bp * t, d)
            y = _mm(mix, w_oo[e], F32, 1024, 512, "mm_out")
            outs_p['sc'].append(sc_st[:, 6:])
            nh = (O_K - O_Q) // HEAD_DIM
            outs_p['sb_k'].append(h[:, :, O_K:O_V].reshape(bp, t, nh, HEAD_DIM))
            outs_p['sb_v'].append(h[:, :, O_V:].reshape(bp, t, nh, HEAD_DIM))
        x, xb, ffn_new = _tail(x, xb, y, memkv.reshape(bp, n_mem, -1), None, p, l, bp, t, alpha)
        outs_p['ffn'].append(ffn_new)
    y_prompt = x.reshape(bp, t, d)

    pos = past
    length = past + 1
    n_cmp = (length - CMP_BLOCK) // CMP_STRIDE + 1
    n_slc = -(-length // SEL_BLOCK)
    k_top = min(N_SEL, n_slc)
    ncrow = n_pages * (PAGE_SIZE // CMP_STRIDE)
    dist_cd = (pos - (np.arange(ncrow) * CMP_STRIDE + CMP_BLOCK - 1)).astype(np.int32)
    bias_cd = _bias_lookup(rel_bias, jnp.asarray(np.broadcast_to(dist_cd, (8, ncrow))), 8)[:, 0, :]
    dist_wd = np.maximum(WINDOW - np.arange(WINDOW + LANES), 0).astype(np.int32)
    bias_wd = _bias_lookup(rel_bias, jnp.asarray(np.broadcast_to(dist_wd, (8, WINDOW + LANES))), 8)
    rb_hg = rel_bias.T.reshape(NSA_KV_HEADS, NSA_GROUP, N_BUCKETS)
    mem_s = cache_mem.reshape(depth, bs, cache_mem.shape[2], -1)

    x = x_sample.reshape(bs, d)
    xb = x.astype(BF16)
    outs_s = {k: [] for k in ('pool', 'cmp_k', 'cmp_v', 'sel_k', 'sel_v', 'win', 'sc', 'sb_k', 'sb_v', 'ffn')}
    for l in range(depth):
        e = l // 2
        if l % 2 == 0:
            h = _mm(xb, w_even[e], F32, 8, 640)
            y_pool, pool_new = _pool_dec(state_pool[e], h[:, E_U:E_U + 1024], w_pool_b[e], pool_scale[e], pos)
            q4 = h[:, E_Q:E_Q + NSA_HEADS * HEAD_DIM].reshape(bs, NSA_KV_HEADS, NSA_GROUP, HEAD_DIM)
            gates = h[:, E_G:E_G + NSA_HEADS * 3].reshape(bs, NSA_KV_HEADS, NSA_GROUP, 3)
            ak, bk, av, bv = _cmp_dec(cache_nsa_cmp_k, cache_nsa_cmp_v, e, page_table, w_pos[e])
            o_c, idx = _nsa_dec_cmp(q4, ak, bk, av, bv, w_lin_b[e], bias_cd, pos, n_cmp, n_slc)
            blocks = idx[:, :, 0, :k_top]
            past_blk = jnp.where((blocks >= 0) & (blocks * SEL_BLOCK < past), blocks, -1)
            safe = jnp.maximum(past_blk, 0)
            per_page = PAGE_SIZE // SEL_BLOCK
            pages = jnp.take_along_axis(page_table[:, None, :], safe // per_page, axis=2)
            halves = safe % per_page
            ks_new = h[:, E_KS:E_KS + kvw].reshape(bs, NSA_KV_HEADS, HEAD_DIM)
            vs_new = h[:, E_VS:E_VS + kvw].reshape(bs, NSA_KV_HEADS, HEAD_DIM)
            o_s = _nsa_dec_sel(q4, ks_new, vs_new, rb_hg, cache_nsa_sel_k, cache_nsa_sel_v, e,
                               pages, halves, past_blk, pos)
            new_kv = h[:, E_KVW:E_KVW + 2 * kvw].reshape(bs, 2, NSA_KV_HEADS, HEAD_DIM)
            y_nsa, win_new = _nsa_dec_win(q4, gates, state_nsa_win[e], new_kv, bias_wd, o_c, o_s)
            mix = jnp.concatenate([y_pool, y_nsa.reshape(bs, -1)], axis=-1)
            y = _mm(mix, w_oe[e], F32, 8, 512)
            outs_s['pool'].append(pool_new)
            for name, off in (('cmp_k', E_KC), ('cmp_v', E_VC), ('sel_k', E_KS), ('sel_v', E_VS)):
                outs_s[name].append(h[:, off:off + kvw].reshape(bs, 1, NSA_KV_HEADS, HEAD_DIM))
            outs_s['win'].append(win_new)
        else:
            h = _mm(xb, w_odd[e], F32, 8, 512)
            y_sc, sc_new = _sconv_dec(state_sc[e], h, sc_conv[e])
            nh = (O_K - O_Q) // HEAD_DIM
            q3 = h[:, O_Q:O_K].reshape(bs, nh, HEAD_DIM)
            y_sb = _sb_dec(q3, cache_sb_k, cache_sb_v, e, page_table)
            mix = jnp.concatenate([y_sc, y_sb.reshape(bs, -1)], axis=-1)
            y = _mm(mix, w_oo[e], F32, 8, 512)
            outs_s['sc'].append(sc_new)
            outs_s['sb_k'].append(h[:, O_K:O_V].reshape(bs, 1, nh, HEAD_DIM))
            outs_s['sb_v'].append(h[:, O_V:].reshape(bs, 1, nh, HEAD_DIM))
        x, xb, ffn_new = _tail(x, xb, y, mem_s[l], state_ffn[l], p, l, bs, 1, alpha)
        outs_s['ffn'].append(ffn_new)
    y_sample = x.reshape(bs, 1, d)

    sp = {k: jnp.stack(v) for k, v in outs_p.items()}
    ss = {k: jnp.stack(v) for k, v in outs_s.items()}
    return (y_prompt, y_sample,
            sp['pool'], sp['cmp_k'], sp['cmp_v'], sp['sel_k'], sp['sel_v'], sp['win'],
            sp['sc'], sp['sb_k'], sp['sb_v'], sp['ffn'], sp['mem'],
            ss['pool'], ss['cmp_k'], ss['cmp_v'], ss['sel_k'], ss['sel_v'], ss['win'],
            ss['sc'], ss['sb_k'], ss['sb_v'], ss['ffn'])
```

```python
import functools
import math

import jax
import jax.numpy as jnp
import numpy as np
from jax import lax
from jax.experimental import pallas as pl
from jax.experimental.pallas import tpu as pltpu

F32 = jnp.float32
BF16 = jnp.bfloat16

HEAD_DIM = 128
PAGE_SIZE = 128
POOL_WINDOWS = (2, 4, 8, 16)
POOL_GROUP_DIM = 256
POOL_STATE = 15
NSA_KV_HEADS = 2
NSA_GROUP = 12
NSA_HEADS = 24
CMP_BLOCK = 32
CMP_STRIDE = 16
SEL_BLOCK = 64
SEL_SHIFT = 6
N_SEL = 16
WINDOW = 512
N_BUCKETS = 32
MAX_DISTANCE = 128
MEM_HEADS = 4
MEM_HEAD_DIM = 256
LN_EPS = 1e-5
NEG_INF = -1e30
BIG = 1e30

LANES = 128
VMEM_LIMIT = 56 * 1024 * 1024

E_Q, E_U, E_KC, E_VC, E_KS, E_VS, E_KVW, E_G, E_TOT = 0, 3072, 4096, 4352, 4608, 4864, 5120, 5632, 6144
O_X, O_B, O_C, O_Q, O_K, O_V = 0, 1024, 2048, 3072, 6144, 9216


def _cp(*sem):
    return pltpu.CompilerParams(dimension_semantics=sem, vmem_limit_bytes=VMEM_LIMIT)


def _nt(a, b):
    return lax.dot_general(a, b, (((1,), (1,)), ((), ())), preferred_element_type=F32)


def _split3(x):
    hi = x.astype(BF16)
    r1 = x - hi.astype(F32)
    mid = r1.astype(BF16)
    lo = (r1 - mid.astype(F32)).astype(BF16)
    return hi, mid, lo


def _mm_kernel(x_ref, w_ref, o_ref):
    o_ref[...] = jnp.dot(x_ref[...], w_ref[...], preferred_element_type=F32).astype(o_ref.dtype)


def _mm(x, w, out_dtype, tm, tn, name="mm", col0=0, ncols=None):
    m, k = x.shape
    n = w.shape[1] - col0 if ncols is None else ncols
    tm = min(tm, m)
    c0 = col0 // tn
    assert col0 % tn == 0 and n % tn == 0
    return pl.pallas_call(
        _mm_kernel,
        name=name,
        grid=(m // tm, n // tn),
        in_specs=[pl.BlockSpec((tm, k), lambda i, j: (i, 0)),
                  pl.BlockSpec((k, tn), lambda i, j: (0, j + c0))],
        out_specs=pl.BlockSpec((tm, tn), lambda i, j: (i, j)),
        out_shape=jax.ShapeDtypeStruct((m, n), out_dtype),
        compiler_params=_cp("parallel", "parallel"),
    )(x, w)


def _mm_acc_kernel(x_ref, w_ref, o_ref, acc_ref):
    k = pl.program_id(2)

    @pl.when(k == 0)
    def _():
        acc_ref[...] = jnp.zeros_like(acc_ref)

    acc_ref[...] += jnp.dot(x_ref[...], w_ref[...], preferred_element_type=F32)

    @pl.when(k == pl.num_programs(2) - 1)
    def _():
        o_ref[...] = acc_ref[...].astype(o_ref.dtype)


def _mm_acc(x, w, out_dtype, tm, tn, tk):
    m, k = x.shape
    n = w.shape[1]
    tm = min(tm, m)
    return pl.pallas_call(
        _mm_acc_kernel,
        name="mm_down",
        grid=(m // tm, n // tn, k // tk),
        in_specs=[pl.BlockSpec((tm, tk), lambda i, j, kk: (i, kk)),
                  pl.BlockSpec((tk, tn), lambda i, j, kk: (kk, j))],
        out_specs=pl.BlockSpec((tm, tn), lambda i, j, kk: (i, j)),
        out_shape=jax.ShapeDtypeStruct((m, n), out_dtype),
        scratch_shapes=[pltpu.VMEM((tm, tn), F32)],
        compiler_params=_cp("parallel", "parallel", "arbitrary"),
    )(x, w)


def _ln_kernel(x_ref, y_ref, g_ref, b_ref, of_ref, ob_ref, *, alpha):
    z = alpha * x_ref[...] + y_ref[...]
    mu = jnp.mean(z, axis=-1, keepdims=True)
    zc = z - mu
    var = jnp.mean(zc * zc, axis=-1, keepdims=True)
    out = zc * lax.rsqrt(var + LN_EPS) * g_ref[...] + b_ref[...]
    of_ref[...] = out
    ob_ref[...] = out.astype(BF16)


def _add_ln(x, y, g, b, alpha, tm=256):
    m, d = x.shape
    tm = min(tm, m)
    row = pl.BlockSpec((tm, d), lambda i: (i, 0))
    vec = pl.BlockSpec((1, d), lambda i: (0, 0))
    return pl.pallas_call(
        functools.partial(_ln_kernel, alpha=alpha),
        name="add_ln",
        grid=(m // tm,),
        in_specs=[row, row, vec, vec],
        out_specs=[row, row],
        out_shape=[jax.ShapeDtypeStruct((m, d), F32), jax.ShapeDtypeStruct((m, d), BF16)],
        compiler_params=_cp("parallel"),
    )(x, y, g.reshape(1, d), b.reshape(1, d))


def _cross_kernel(q_ref, kv_ref, o_ref):
    width = MEM_HEADS * MEM_HEAD_DIM
    scale = MEM_HEAD_DIM ** -0.5
    for h in range(MEM_HEADS):
        sl = slice(h * MEM_HEAD_DIM, (h + 1) * MEM_HEAD_DIM)
        q = q_ref[:, sl].astype(BF16)
        k = kv_ref[:, sl].astype(BF16)
        v = kv_ref[:, width + h * MEM_HEAD_DIM: width + (h + 1) * MEM_HEAD_DIM].astype(BF16)
        s = _nt(q, k) * scale
        s = s - jnp.max(s, axis=-1, keepdims=True)
        e = jnp.exp(s)
        p = e / jnp.sum(e, axis=-1, keepdims=True)
        o_ref[:, sl] = jnp.dot(p.astype(BF16), v, preferred_element_type=F32).astype(o_ref.dtype)


def _cross(q, kv, tq):
    b, t, w = q.shape
    tq = min(tq, t)
    return pl.pallas_call(
        _cross_kernel,
        name="cross_attn",
        grid=(b, t // tq),
        in_specs=[pl.BlockSpec((None, tq, w), lambda i, j: (i, j, 0)),
                  pl.BlockSpec((None, kv.shape[1], kv.shape[2]), lambda i, j: (i, 0, 0))],
        out_specs=pl.BlockSpec((None, tq, w), lambda i, j: (i, j, 0)),
        out_shape=jax.ShapeDtypeStruct((b, t, w), BF16),
        compiler_params=_cp("parallel", "parallel"),
    )(q, kv)


def _shift_rows(h, c_last2, row):
    s1 = jnp.where(row == 0, c_last2[7:8], pltpu.roll(h, 1, 0))
    s2 = jnp.where(row == 0, c_last2[6:7], jnp.where(row == 1, c_last2[7:8], pltpu.roll(h, 2, 0)))
    return s1, s2


def _ffn_up_kernel(x_ref, w1_ref, w2_ref, cw1_ref, cw2_ref, act_ref, st1_ref, st2_ref, *, sub):
    t = x_ref.shape[0]
    tn = w1_ref.shape[1]
    ns = t // sub
    w1 = w1_ref[...].astype(BF16)
    w2 = w2_ref[...].astype(BF16)
    cw1 = cw1_ref[...]
    cw2 = cw2_ref[...]
    row = lax.broadcasted_iota(jnp.int32, (sub, 1), 0)
    zeros8 = jnp.zeros((8, tn), F32)
    h = [None] * ns

    def dots(s):
        xs = x_ref[s * sub:(s + 1) * sub, :]
        return jnp.dot(xs, w1, preferred_element_type=F32), jnp.dot(xs, w2, preferred_element_type=F32)

    def conv(hc, prev8, cw):
        s1, s2 = _shift_rows(hc, prev8, row)
        return s2 * cw[0:1] + s1 * cw[1:2] + hc * cw[2:3]

    def epilogue(s):
        p1, p2 = (zeros8, zeros8) if s == 0 else (h[s - 1][0][sub - 8:], h[s - 1][1][sub - 8:])
        c1 = conv(h[s][0], p1, cw1)
        c2 = conv(h[s][1], p2, cw2)
        act_ref[s * sub:(s + 1) * sub, :] = (c1 * jax.nn.sigmoid(c1) * c2).astype(BF16)

    h[0] = dots(0)
    for s in range(1, ns):
        h[s] = dots(s)
        epilogue(s - 1)
    epilogue(ns - 1)
    st1_ref[...] = h[ns - 1][0][sub - 8:]
    st2_ref[...] = h[ns - 1][1][sub - 8:]


def _ffn_up(x, w_up, layer, conv_w, t, sub=512, tn=256):
    m, d = x.shape
    dff = w_up.shape[2] // 2
    nt = dff // tn
    sub = min(sub, t)
    bsz = m // t
    return pl.pallas_call(
        functools.partial(_ffn_up_kernel, sub=sub),
        grid=(bsz, nt),
        in_specs=[pl.BlockSpec((t, d), lambda i, j: (i, 0), pipeline_mode=pl.Buffered(1)),
                  pl.BlockSpec((None, d, tn), lambda i, j: (layer, 0, j)),
                  pl.BlockSpec((None, d, tn), lambda i, j: (layer, 0, j + nt)),
                  pl.BlockSpec((3, tn), lambda i, j: (0, j)),
                  pl.BlockSpec((3, tn), lambda i, j: (0, j + nt))],
        out_specs=[pl.BlockSpec((t, tn), lambda i, j: (i, j)),
                   pl.BlockSpec((None, 8, tn), lambda i, j: (i, 0, j)),
                   pl.BlockSpec((None, 8, tn), lambda i, j: (i, 0, j))],
        out_shape=[jax.ShapeDtypeStruct((m, dff), BF16),
                   jax.ShapeDtypeStruct((bsz, 8, dff), F32),
                   jax.ShapeDtypeStruct((bsz, 8, dff), F32)],
        compiler_params=_cp("parallel", "parallel"),
        name="ffn_up",
    )(x, w_up, w_up, conv_w, conv_w)


def _ffn_up_dec_kernel(x_ref, w1_ref, w2_ref, cw1_ref, cw2_ref, p1_ref, p2_ref, act_ref, n1_ref, n2_ref):
    x = x_ref[...]
    h1 = jnp.dot(x, w1_ref[...].astype(BF16), preferred_element_type=F32)
    h2 = jnp.dot(x, w2_ref[...].astype(BF16), preferred_element_type=F32)

    def conv(h, p_ref, cw):
        return p_ref[:, 0, :] * cw[0:1] + p_ref[:, 1, :] * cw[1:2] + h * cw[2:3]

    c1 = conv(h1, p1_ref, cw1_ref[...])
    c2 = conv(h2, p2_ref, cw2_ref[...])
    act_ref[...] = (c1 * jax.nn.sigmoid(c1) * c2).astype(BF16)
    n1_ref[:, 0, :] = p1_ref[:, 1, :]
    n1_ref[:, 1, :] = h1
    n2_ref[:, 0, :] = p2_ref[:, 1, :]
    n2_ref[:, 1, :] = h2


def _ffn_up_dec(x, w_up, layer, conv_w, prev, tn=256):
    b, d = x.shape
    dff = w_up.shape[2] // 2
    nt = dff // tn
    pspec1 = pl.BlockSpec((b, 2, tn), lambda j: (0, 0, j))
    pspec2 = pl.BlockSpec((b, 2, tn), lambda j: (0, 0, j + nt))
    act, n1, n2 = pl.pallas_call(
        _ffn_up_dec_kernel,
        name="ffn_up_dec",
        grid=(nt,),
        in_specs=[pl.BlockSpec((b, d), lambda j: (0, 0)),
                  pl.BlockSpec((None, d, tn), lambda j: (layer, 0, j)),
                  pl.BlockSpec((None, d, tn), lambda j: (layer, 0, j + nt)),
                  pl.BlockSpec((3, tn), lambda j: (0, j)),
                  pl.BlockSpec((3, tn), lambda j: (0, j + nt)),
                  pspec1, pspec2],
        out_specs=[pl.BlockSpec((b, tn), lambda j: (0, j)),
                   pl.BlockSpec((b, 2, tn), lambda j: (0, 0, j)),
                   pl.BlockSpec((b, 2, tn), lambda j: (0, 0, j))],
        out_shape=[jax.ShapeDtypeStruct((b, dff), BF16),
                   jax.ShapeDtypeStruct((b, 2, dff), F32),
                   jax.ShapeDtypeStruct((b, 2, dff), F32)],
        compiler_params=_cp("parallel"),
    )(x, w_up, w_up, conv_w, conv_w, prev, prev)
    return act, jnp.concatenate([n1, n2], axis=-1)


def _bias_kernel(rb_ref, dist_ref, o_ref):
    head = pl.program_id(0)
    n = jnp.maximum(dist_ref[...], 0)
    max_exact = N_BUCKETS // 2
    nf = jnp.maximum(n, 1).astype(F32)
    large = max_exact + (jnp.log(nf / max_exact) / math.log(MAX_DISTANCE / max_exact)
                         * (N_BUCKETS - max_exact)).astype(jnp.int32)
    bucket = jnp.where(n < max_exact, n, jnp.minimum(large, N_BUCKETS - 1))
    out = jnp.zeros(n.shape, F32)
    for k in range(N_BUCKETS):
        out = jnp.where(bucket == k, rb_ref[k, head], out)
    o_ref[...] = out


def _bias_lookup(rel_bias, dist, tr):
    r, c = dist.shape
    nh = rel_bias.shape[1]
    tr = min(tr, r)
    return pl.pallas_call(
        _bias_kernel,
        grid=(nh, r // tr),
        in_specs=[pl.BlockSpec(memory_space=pltpu.SMEM),
                  pl.BlockSpec((tr, c), lambda h, i: (i, 0))],
        out_specs=pl.BlockSpec((None, tr, c), lambda h, i: (h, i, 0)),
        out_shape=jax.ShapeDtypeStruct((nh, r, c), F32),
        compiler_params=_cp("parallel", "parallel"),
        name="bias_lookup",
    )(rel_bias, dist)


def _bias_blocked_kernel(rb_ref, dist_ref, o_ref):
    n = jnp.maximum(dist_ref[...], 0)
    max_exact = N_BUCKETS // 2
    nf = jnp.maximum(n, 1).astype(F32)
    large = max_exact + (jnp.log(nf / max_exact) / math.log(MAX_DISTANCE / max_exact)
                         * (N_BUCKETS - max_exact)).astype(jnp.int32)
    bucket = jnp.where(n < max_exact, n, jnp.minimum(large, N_BUCKETS - 1))
    hits = [bucket == k for k in range(N_BUCKETS)]

    def head(hd, carry):
        out = jnp.zeros(n.shape, F32)
        for k in range(N_BUCKETS):
            out = jnp.where(hits[k], rb_ref[k, hd], out)
        o_ref[hd] = out
        return carry

    lax.fori_loop(0, o_ref.shape[0], head, 0)


def _bias_lookup_blocked(rel_bias, dist, tc):
    r, c = dist.shape
    nh = rel_bias.shape[1]
    return pl.pallas_call(
        _bias_blocked_kernel,
        grid=(c // tc,),
        in_specs=[pl.BlockSpec(memory_space=pltpu.SMEM),
                  pl.BlockSpec((r, tc), lambda j: (0, j))],
        out_specs=pl.BlockSpec((None, nh, r, tc), lambda j: (j, 0, 0, 0)),
        out_shape=jax.ShapeDtypeStruct((c // tc, nh, r, tc), F32),
        compiler_params=_cp("parallel"),
        name="bias_lookup_blocked",
    )(rel_bias, dist)


def _pool_groups(sums, cur, pos, wp_ref, sc_ref, y_ref):
    for g, w in enumerate(POOL_WINDOWS):
        sl = slice(g * POOL_GROUP_DIM, (g + 1) * POOL_GROUP_DIM)
        cnt = jnp.minimum(w, pos + 1).astype(F32)
        d = sums[g] / cnt - cur[:, sl]
        y = jnp.dot(d.astype(BF16), wp_ref[g], preferred_element_type=F32) * sc_ref[:, sl]
        y_ref[:, sl] = y.astype(y_ref.dtype)


def _pool_kernel(prev_ref, cur_ref, wp_ref, sc_ref, y_ref):
    t = pl.program_id(1)
    tt = cur_ref.shape[0]
    cur = cur_ref[...]
    prev = jnp.where(t == 0, 0.0, prev_ref[...])
    ext = jnp.concatenate([prev, cur], axis=0)
    gd = POOL_GROUP_DIM
    s2 = ext + pltpu.roll(ext, 1, 0)
    x4 = s2[:, gd:]
    s4 = x4 + pltpu.roll(x4, 2, 0)
    x8 = s4[:, gd:]
    s8 = x8 + pltpu.roll(x8, 4, 0)
    x16 = s8[:, gd:]
    s16 = x16 + pltpu.roll(x16, 8, 0)
    sums = [s2[16:, :gd], s4[16:, :gd], s8[16:, :gd], s16[16:]]
    pos = t * tt + lax.broadcasted_iota(jnp.int32, (tt, 1), 0)
    _pool_groups(sums, cur, pos, wp_ref, sc_ref, y_ref)


def _pool_prompt(h, col_blk, w_pool, pool_scale, tt=256):
    b, t, _ = h.shape
    wdt = w_pool.shape[0] * POOL_GROUP_DIM
    tt = min(tt, t)
    r = tt // 16
    return pl.pallas_call(
        _pool_kernel,
        name="pool_prompt",
        grid=(b, t // tt),
        in_specs=[pl.BlockSpec((None, 16, wdt), lambda i, j: (i, jnp.maximum(j * r - 1, 0), col_blk)),
                  pl.BlockSpec((None, tt, wdt), lambda i, j: (i, j, col_blk)),
                  pl.BlockSpec(w_pool.shape, lambda i, j: (0, 0, 0)),
                  pl.BlockSpec((1, wdt), lambda i, j: (0, 0))],
        out_specs=pl.BlockSpec((None, tt, wdt), lambda i, j: (i, j, 0)),
        out_shape=jax.ShapeDtypeStruct((b, t, wdt), BF16),
        compiler_params=_cp("parallel", "parallel"),
    )(h, h, w_pool, pool_scale.reshape(1, wdt))


def _pool_dec_kernel(sp_ref, u_ref, wp_ref, sc_ref, y_ref, ns_ref, *, pos):
    u = u_ref[...]
    gd = POOL_GROUP_DIM
    acc = u
    sums = []
    back = 1
    for g, w in enumerate(POOL_WINDOWS):
        while back < w:
            acc = acc + sp_ref[:, POOL_STATE - back, :]
            back += 1
        sums.append(acc[:, g * gd:(g + 1) * gd])
    posv = jnp.full((u.shape[0], 1), pos, jnp.int32)
    _pool_groups(sums, u, posv, wp_ref, sc_ref, y_ref)
    for j in range(POOL_STATE - 1):
        ns_ref[:, j, :] = sp_ref[:, j + 1, :]
    ns_ref[:, POOL_STATE - 1, :] = u


def _pool_dec(state, u, w_pool, pool_scale, pos):
    b, wdt = u.shape
    return pl.pallas_call(
        functools.partial(_pool_dec_kernel, pos=pos),
        out_shape=[jax.ShapeDtypeStruct((b, wdt), BF16), jax.ShapeDtypeStruct(state.shape, F32)],
        compiler_params=pltpu.CompilerParams(vmem_limit_bytes=VMEM_LIMIT),
    )(state, u, w_pool, pool_scale.reshape(1, wdt))


def _sconv_kernel(px_ref, pc_ref, x_ref, b_ref, c_ref, cw_ref, y_ref, st_ref):
    t = pl.program_id(1)
    tt = x_ref.shape[0]
    v = c_ref[...] * x_ref[...]
    pv = jnp.where(t == 0, 0.0, pc_ref[...] * px_ref[...])
    row = lax.broadcasted_iota(jnp.int32, (tt, 1), 0)
    s1, s2 = _shift_rows(v, pv, row)
    cw = cw_ref[...]
    conv = s2 * cw[0:1] + s1 * cw[1:2] + v * cw[2:3]
    y_ref[...] = (b_ref[...] * conv).astype(y_ref.dtype)
    st_ref[...] = v[tt - 8:]


def _sconv_prompt(h, conv_w, tt=256):
    b, t, _ = h.shape
    wdt = conv_w.shape[1]
    tt = min(tt, t)
    r = tt // 8

    def prev(blk):
        return pl.BlockSpec((None, 8, wdt), lambda i, j: (i, jnp.maximum(j * r - 1, 0), blk))

    def cur(blk):
        return pl.BlockSpec((None, tt, wdt), lambda i, j: (i, j, blk))

    return pl.pallas_call(
        _sconv_kernel,
        name="sconv_prompt",
        grid=(b, t // tt),
        in_specs=[prev(0), prev(2), cur(0), cur(1), cur(2), pl.BlockSpec((3, wdt), lambda i, j: (0, 0))],
        out_specs=[pl.BlockSpec((None, tt, wdt), lambda i, j: (i, j, 0)),
                   pl.BlockSpec((None, 8, wdt), lambda i, j: (i, 0, 0))],
        out_shape=[jax.ShapeDtypeStruct((b, t, wdt), BF16), jax.ShapeDtypeStruct((b, 8, wdt), F32)],
        compiler_params=_cp("parallel", "arbitrary"),
    )(h, h, h, h, h, conv_w)


def _sconv_dec_kernel(st_ref, x_ref, b_ref, c_ref, cw_ref, y_ref, ns_ref):
    v = c_ref[...] * x_ref[...]
    cw = cw_ref[...]
    conv = st_ref[:, 0, :] * cw[0:1] + st_ref[:, 1, :] * cw[1:2] + v * cw[2:3]
    y_ref[...] = (b_ref[...] * conv).astype(y_ref.dtype)
    ns_ref[:, 0, :] = st_ref[:, 1, :]
    ns_ref[:, 1, :] = v


def _sconv_dec(state, h, conv_w):
    b = h.shape[0]
    wdt = conv_w.shape[1]

    def col(blk):
        return pl.BlockSpec((b, wdt), lambda i: (0, blk))

    return pl.pallas_call(
        _sconv_dec_kernel,
        grid=(1,),
        in_specs=[pl.BlockSpec(state.shape, lambda i: (0, 0, 0)), col(0), col(1), col(2),
                  pl.BlockSpec((3, wdt), lambda i: (0, 0))],
        out_specs=[pl.BlockSpec((b, wdt), lambda i: (0, 0)), pl.BlockSpec(state.shape, lambda i: (0, 0, 0))],
        out_shape=[jax.ShapeDtypeStruct((b, wdt), BF16), jax.ShapeDtypeStruct(state.shape, F32)],
        compiler_params=_cp("arbitrary"),
    )(state, h, h, h, conv_w)


def _softplus(z):
    return jnp.maximum(z, 0.0) + jnp.log(1.0 + jnp.exp(-jnp.abs(z)))


def _suffix_sums(x, upper):
    r = x.shape[0]
    hi = x.astype(BF16)
    lo = (x - hi.astype(F32)).astype(BF16)
    s = jnp.dot(jnp.concatenate([hi, lo], axis=0), upper, preferred_element_type=F32)
    return s[:r] + s[r:]


SB_HEADS_PER_STEP = 4


def _sb_kernel(q_ref, k_ref, v_ref, o_ref, *, tq):
    i = pl.program_id(2)
    scale = HEAD_DIM ** -0.5
    nhs = SB_HEADS_PER_STEP
    hsl = [slice(j * HEAD_DIM, (j + 1) * HEAD_DIM) for j in range(nhs)]
    qs = [q_ref[:, sl].astype(BF16) for sl in hsl]
    rj = lax.broadcasted_iota(jnp.int32, (tq, tq), 0)
    cj = lax.broadcasted_iota(jnp.int32, (tq, tq), 1)
    upper = jnp.where(rj > cj, 1.0, 0.0).astype(BF16)

    before = jnp.concatenate([cj < rj] * nhs, axis=0)

    def chunk(c, carry, diag):
        r_run, accs = carry
        off = pl.multiple_of(c * tq, tq)
        z = jnp.concatenate([_nt(qs[j], k_ref[pl.ds(off, tq), hsl[j]].astype(BF16)) for j in range(nhs)],
                            axis=0) * scale
        sp = _softplus(z)
        if diag:
            sp = jnp.where(before, sp, 0.0)
        a = jnp.exp(z - sp - _suffix_sums(sp, upper) - r_run)
        if diag:
            a = jnp.where(before, a, 0.0)
        ab = a.astype(BF16)
        accs = tuple(accs[j] + jnp.dot(ab[j * tq:(j + 1) * tq], v_ref[pl.ds(off, tq), hsl[j]].astype(BF16),
                                       preferred_element_type=F32) for j in range(nhs))
        return r_run + jnp.sum(sp, axis=1, keepdims=True), accs

    init = (jnp.zeros((nhs * tq, 1), F32), tuple(jnp.zeros((tq, HEAD_DIM), F32) for _ in range(nhs)))
    first = chunk(i, init, True)
    _, accs = lax.fori_loop(0, i, lambda s, carry: chunk(i - 1 - s, carry, False), first)
    for j in range(nhs):
        o_ref[:, hsl[j]] = accs[j].astype(o_ref.dtype)


def _sb_prompt(q, k, v, tq=256):
    b, t, width = q.shape
    tq = min(tq, t)
    wdt = SB_HEADS_PER_STEP * HEAD_DIM
    return pl.pallas_call(
        functools.partial(_sb_kernel, tq=tq),
        grid=(b, width // wdt, t // tq),
        in_specs=[pl.BlockSpec((None, tq, wdt), lambda bi, hi, i: (bi, i, hi)),
                  pl.BlockSpec((None, t, wdt), lambda bi, hi, i: (bi, 0, hi)),
                  pl.BlockSpec((None, t, wdt), lambda bi, hi, i: (bi, 0, hi))],
        out_specs=pl.BlockSpec((None, tq, wdt), lambda bi, hi, i: (bi, i, hi)),
        out_shape=jax.ShapeDtypeStruct((b, t, width), BF16),
        compiler_params=_cp("parallel", "parallel", "parallel"),
        name="sb_prompt",
    )(q, k, v)


SB_PAGES_PER_STEP = 2


def _sb_dec_kernel(pt_ref, q_ref, *refs):
    npg = SB_PAGES_PER_STEP
    k_refs, v_refs = refs[:npg], refs[npg:2 * npg]
    o_ref, z_sc, r_sc, acc_sc = refs[2 * npg:]
    c = pl.program_id(1)
    scale = HEAD_DIM ** -0.5
    nh = q_ref.shape[0]

    @pl.when(c == 0)
    def _():
        z_sc[...] = jnp.zeros_like(z_sc)
        r_sc[...] = jnp.zeros_like(r_sc)
        acc_sc[...] = jnp.zeros_like(acc_sc)

    rj = lax.broadcasted_iota(jnp.int32, (PAGE_SIZE, PAGE_SIZE), 0)
    cj = lax.broadcasted_iota(jnp.int32, (PAGE_SIZE, PAGE_SIZE), 1)
    upper = jnp.where(rj > cj, 1.0, 0.0).astype(BF16)
    hrow = lax.broadcasted_iota(jnp.int32, (nh, HEAD_DIM), 0)

    def heads_on_lanes(ref):
        return jnp.concatenate([ref[pl.ds(j, PAGE_SIZE, stride=nh), :].astype(BF16) for j in range(nh)], axis=1)

    live = c > 0
    r_run = r_sc[...]
    weights = []
    for r in range(npg):
        z = z_sc[r]
        sp = jnp.where(live, _softplus(z), 0.0)
        weights.append(jnp.where(live, jnp.exp(z - sp - _suffix_sums(sp, upper) - r_run), 0.0).astype(BF16))
        r_run = r_run + jnp.sum(sp, axis=1, keepdims=True)
    r_sc[...] = r_run

    q = q_ref[...]
    q_bd = jnp.concatenate([jnp.where(hrow == j, q, 0.0) for j in range(nh)], axis=1).astype(BF16)
    for r in range(npg):
        z_sc[r] = _nt(q_bd, heads_on_lanes(k_refs[r])) * scale

    upd = jnp.zeros((nh, HEAD_DIM), F32)
    for r in range(npg):
        res = jnp.dot(weights[r], heads_on_lanes(v_refs[r]), preferred_element_type=F32)
        for j in range(nh):
            upd = upd + jnp.where(hrow == j, res[:, j * HEAD_DIM:(j + 1) * HEAD_DIM], 0.0)
    acc_sc[...] += upd

    @pl.when(c == pl.num_programs(1) - 1)
    def _():
        o_ref[...] = acc_sc[...].astype(o_ref.dtype)


def _sb_dec(q, k_pool, v_pool, layer, page_table):
    b, nh, _ = q.shape
    n_pages = page_table.shape[1]
    npg = SB_PAGES_PER_STEP
    assert n_pages % npg == 0
    n_steps = n_pages // npg
    rows = PAGE_SIZE * nh
    k_pool = k_pool.reshape(k_pool.shape[:2] + (rows, HEAD_DIM))
    v_pool = v_pool.reshape(v_pool.shape[:2] + (rows, HEAD_DIM))

    def k_spec(r):
        return pl.BlockSpec((None, None, rows, HEAD_DIM), lambda bi, c, pt: (
            layer, pt[bi, n_pages - 1 - (npg * jnp.minimum(c, n_steps - 1) + r)], 0, 0))

    def v_spec(r):
        return pl.BlockSpec((None, None, rows, HEAD_DIM), lambda bi, c, pt: (
            layer, pt[bi, n_pages - 1 - (npg * jnp.maximum(c - 1, 0) + r)], 0, 0))

    grid_spec = pltpu.PrefetchScalarGridSpec(
        num_scalar_prefetch=1,
        grid=(b, n_steps + 1),
        in_specs=[pl.BlockSpec((None, nh, HEAD_DIM), lambda bi, c, pt: (bi, 0, 0))]
        + [k_spec(r) for r in range(npg)] + [v_spec(r) for r in range(npg)],
        out_specs=pl.BlockSpec((None, nh, HEAD_DIM), lambda bi, c, pt: (bi, 0, 0)),
        scratch_shapes=[pltpu.VMEM((npg, nh, PAGE_SIZE), F32), pltpu.VMEM((nh, 1), F32),
                        pltpu.VMEM((nh, HEAD_DIM), F32)],
    )
    return pl.pallas_call(
        _sb_dec_kernel,
        grid_spec=grid_spec,
        out_shape=jax.ShapeDtypeStruct((b, nh, HEAD_DIM), BF16),
        compiler_params=_cp("parallel", "arbitrary"),
        name="sb_decode",
    )(page_table, q, *([k_pool] * npg), *([v_pool] * npg))


def _chunk_sums(x, w0, w1):
    x3 = x.reshape(x.shape[0] // CMP_STRIDE, CMP_STRIDE, x.shape[1])
    return jnp.sum(x3 * w0[None], axis=1), jnp.sum(x3 * w1[None], axis=1)


def _compress_kernel(kc_ref, vc_ref, wpos_ref, wlin_ref, ko_ref, vo_ref):
    nchunk = kc_ref.shape[0] // CMP_STRIDE
    for idx, (src, dst) in enumerate(((kc_ref, ko_ref), (vc_ref, vo_ref))):
        a, bsum = _chunk_sums(src[...], wpos_ref[idx, 0:CMP_STRIDE, :], wpos_ref[idx, CMP_STRIDE:, :])
        pooled = a + pltpu.roll(bsum, nchunk - 1, 0)
        for h in range(NSA_KV_HEADS):
            sl = slice(h * HEAD_DIM, (h + 1) * HEAD_DIM)
            dst[:, sl] = jnp.dot(pooled[:, sl].astype(BF16), wlin_ref[idx, h], preferred_element_type=F32)


def _compress_prompt(h, wpos, wlin):
    b, t, _ = h.shape
    kvw = NSA_KV_HEADS * HEAD_DIM
    nchunk = t // CMP_STRIDE
    out = jax.ShapeDtypeStruct((b, nchunk, kvw), F32)
    ospec = pl.BlockSpec((None, nchunk, kvw), lambda i: (i, 0, 0))
    return pl.pallas_call(
        _compress_kernel,
        name="compress_prompt",
        grid=(b,),
        in_specs=[pl.BlockSpec((None, t, kvw), lambda i: (i, 0, E_KC // kvw)),
                  pl.BlockSpec((None, t, kvw), lambda i: (i, 0, E_VC // kvw)),
                  pl.BlockSpec(wpos.shape, lambda i: (0, 0, 0)),
                  pl.BlockSpec(wlin.shape, lambda i: (0, 0, 0, 0))],
        out_specs=[ospec, ospec],
        out_shape=[out, out],
        compiler_params=_cp("parallel"),
    )(h, h, wpos, wlin)


def _topk_keep(score, k_top):
    srow = lax.broadcasted_iota(jnp.int32, score.shape, 0)
    rank = jnp.zeros(score.shape, jnp.int32)
    for s2 in range(score.shape[0]):
        row = score[s2:s2 + 1, :]
        rank = rank + jnp.where(row > score, 1, jnp.where(row == score, jnp.where(srow > s2, 1, 0), 0))
    return jnp.where(rank < k_top, jnp.where(score > 0.5 * NEG_INF, 1.0, 0.0), 0.0)


def _flash_init(m_sc, l_sc, acc_sc):
    m_sc[...] = jnp.full(m_sc.shape, NEG_INF, F32)
    l_sc[...] = jnp.zeros(l_sc.shape, F32)
    acc_sc[...] = jnp.zeros(acc_sc.shape, F32)


def _flash_chunk(q_all, k, v, bias_fn, mask, m_sc, l_sc, acc_sc):
    ng = NSA_GROUP
    r = q_all.shape[0] // ng
    s_all = _nt(k.astype(BF16), q_all) * (HEAD_DIM ** -0.5)
    v_t = v.T.astype(BF16)
    m_prev = m_sc[...]
    m_parts, p_parts = [], []
    for g in range(ng):
        sl = slice(g * r, (g + 1) * r)
        s = s_all[:, sl] + bias_fn(g)
        if mask is not None:
            s = jnp.where(mask, s, NEG_INF)
        mn = jnp.maximum(m_prev[:, sl], jnp.max(s, axis=0, keepdims=True))
        p_parts.append(jnp.exp(s - mn))
        m_parts.append(mn)
    m_new = jnp.concatenate(m_parts, axis=1)
    p = jnp.concatenate(p_parts, axis=1)
    alpha = jnp.exp(m_prev - m_new)
    l_sc[...] = alpha * l_sc[...] + jnp.sum(p, axis=0, keepdims=True)
    acc_sc[...] = alpha * acc_sc[...] + jnp.dot(v_t, p.astype(BF16), preferred_element_type=F32)
    m_sc[...] = m_new


def _nsa_kernel(b31_ref, q_ref, g_ref, kc_ref, vc_ref, ks_ref, vs_ref, kvw_ref, tabc_ref, tabd_ref, o_ref,
                m_sc, l_sc, acc_sc, oc_sc, os_sc, keep_sc):
    i = pl.program_id(1)
    qb = q_ref.shape[0]
    t_len = ks_ref.shape[0]
    nc = kc_ref.shape[0]
    n_cmp = (t_len - CMP_BLOCK) // CMP_STRIDE + 1
    n_slc = t_len // SEL_BLOCK
    k_top = min(N_SEL, n_slc)
    gsz = NSA_GROUP
    scale = HEAD_DIM ** -0.5
    keyi = lax.broadcasted_iota(jnp.int32, (qb, qb), 0)
    qi = lax.broadcasted_iota(jnp.int32, (qb, qb), 1)
    causal = keyi <= qi
    wedge = keyi >= qi
    first_half = keyi < SEL_BLOCK
    pos_row = i * qb + lax.broadcasted_iota(jnp.int32, (1, qb), 1)
    sig_t = jax.nn.sigmoid(g_ref[...]).T
    osb = lax.broadcasted_iota(jnp.int32, (n_slc, nc), 0)
    on = lax.broadcasted_iota(jnp.int32, (n_slc, nc), 1)
    c_start = on * CMP_STRIDE
    overlap_t = jnp.where((c_start < osb * SEL_BLOCK + SEL_BLOCK) & (c_start + CMP_BLOCK - 1 >= osb * SEL_BLOCK)
                          & (on < n_cmp), 1.0, 0.0).astype(BF16)
    nrow = lax.broadcasted_iota(jnp.int32, (nc, qb), 0)
    cmask = ((nrow * CMP_STRIDE + CMP_BLOCK - 1) <= pos_row) & (nrow < n_cmp)
    srow = lax.broadcasted_iota(jnp.int32, (n_slc, qb), 0)
    cur = pos_row >> SEL_SHIFT
    forced = (srow == 0) | (srow == cur) | (srow == cur - 1)
    future = srow > cur

    for h in range(NSA_KV_HEADS):
        hs = slice(h * HEAD_DIM, (h + 1) * HEAD_DIM)
        g0 = h * gsz
        q_all = jnp.concatenate(
            [q_ref[:, (g0 + g) * HEAD_DIM:(g0 + g + 1) * HEAD_DIM] for g in range(gsz)], axis=0).astype(BF16)

        s_all = _nt(kc_ref[:, hs].astype(BF16), q_all) * scale
        vc_t = vc_ref[:, hs].T.astype(BF16)
        pcs = jnp.zeros((nc, qb), F32)
        pc_parts = []
        for g in range(gsz):
            s = s_all[:, g * qb:(g + 1) * qb] + tabc_ref[g0 + g]
            s = jnp.where(cmask, s, NEG_INF)
            s = s - jnp.max(s, axis=0, keepdims=True)
            e = jnp.exp(s)
            pc = jnp.where(cmask, e / jnp.sum(e, axis=0, keepdims=True), 0.0)
            pcs = pcs + pc
            pc_parts.append(pc.astype(BF16))
        oc_sc[...] = jnp.dot(vc_t, jnp.concatenate(pc_parts, axis=1), preferred_element_type=F32)
        imp = sum(jnp.dot(overlap_t, piece, preferred_element_type=F32) for piece in _split3(pcs))
        score = jnp.where(future, NEG_INF, jnp.where(forced, BIG, imp))
        keep_sc[...] = _topk_keep(score, k_top)

        def sel_mask(c, nk=1):
            parts = []
            for u in range(nk):
                first = keep_sc[pl.ds(2 * (c + u), 1), :]
                second = keep_sc[pl.ds(2 * (c + u) + 1, 1), :]
                parts.append(jnp.where(first_half, first, second))
            return jnp.concatenate(parts, axis=0) > 0.5

        def chunk(k_ref, v_ref, c, ksl, vsl, bias_fn, mask, nk=1):
            off = pl.multiple_of(c * qb, qb)
            _flash_chunk(q_all, k_ref[pl.ds(off, nk * qb), ksl], v_ref[pl.ds(off, nk * qb), vsl], bias_fn, mask,
                         m_sc, l_sc, acc_sc)

        tab0 = lambda g: tabd_ref[g0 + g, 0]
        tab1 = lambda g: tabd_ref[g0 + g, 1]
        far = lambda g: b31_ref[g0 + g]

        _flash_init(m_sc, l_sc, acc_sc)
        chunk(ks_ref, vs_ref, i, hs, hs, tab0, sel_mask(i) & causal)

        @pl.when(i >= 1)
        def _():
            chunk(ks_ref, vs_ref, i - 1, hs, hs, tab1, sel_mask(i - 1))

        n_far = jnp.maximum(i - 1, 0)

        def far_body(pair, carry):
            chunk(ks_ref, vs_ref, 2 * pair, hs, hs, far, sel_mask(2 * pair, 2), 2)
            return carry

        lax.fori_loop(0, n_far // 2, far_body, 0)

        @pl.when(n_far % 2 == 1)
        def _():
            chunk(ks_ref, vs_ref, n_far - 1, hs, hs, far, sel_mask(n_far - 1))

        os_sc[...] = acc_sc[...] / l_sc[...]

        kw_sl = hs
        vw_sl = slice(NSA_KV_HEADS * HEAD_DIM + h * HEAD_DIM, NSA_KV_HEADS * HEAD_DIM + (h + 1) * HEAD_DIM)
        _flash_init(m_sc, l_sc, acc_sc)
        chunk(kvw_ref, kvw_ref, i, kw_sl, vw_sl, tab0, causal)

        @pl.when(i >= 1)
        def _():
            chunk(kvw_ref, kvw_ref, i - 1, kw_sl, vw_sl, tab1, None)

        @pl.when(i >= 3)
        def _():
            chunk(kvw_ref, kvw_ref, i - 3, kw_sl, vw_sl, far, None, 2)

        @pl.when(i == 2)
        def _():
            chunk(kvw_ref, kvw_ref, 0, kw_sl, vw_sl, far, None)

        @pl.when(i >= 4)
        def _():
            chunk(kvw_ref, kvw_ref, i - 4, kw_sl, vw_sl, far, wedge)

        ow = acc_sc[...] / l_sc[...]

        def gate(j):
            return jnp.concatenate([sig_t[(g0 + g) * 3 + j:(g0 + g) * 3 + j + 1, :] for g in range(gsz)], axis=1)

        y_t = gate(0) * oc_sc[...] + gate(1) * os_sc[...] + gate(2) * ow
        for g in range(gsz):
            o_ref[:, (g0 + g) * HEAD_DIM:(g0 + g + 1) * HEAD_DIM] = y_t[:, g * qb:(g + 1) * qb].T.astype(o_ref.dtype)


def _nsa_prompt(h, kcmp, vcmp, tabc, tabd, b31, qb=128):
    b, t, _ = h.shape
    assert qb == HEAD_DIM == 2 * SEL_BLOCK and WINDOW == 4 * qb and t % qb == 0
    qw = NSA_HEADS * HEAD_DIM
    kvw = NSA_KV_HEADS * HEAD_DIM
    nchunk = kcmp.shape[1]
    gq = NSA_GROUP * qb
    return pl.pallas_call(
        _nsa_kernel,
        grid=(b, t // qb),
        in_specs=[pl.BlockSpec(memory_space=pltpu.SMEM),
                  pl.BlockSpec((None, qb, qw), lambda bi, i: (bi, i, E_Q // qw)),
                  pl.BlockSpec((None, qb, LANES), lambda bi, i: (bi, i, E_G // LANES)),
                  pl.BlockSpec((None, nchunk, kvw), lambda bi, i: (bi, 0, 0)),
                  pl.BlockSpec((None, nchunk, kvw), lambda bi, i: (bi, 0, 0)),
                  pl.BlockSpec((None, t, kvw), lambda bi, i: (bi, 0, E_KS // kvw)),
                  pl.BlockSpec((None, t, kvw), lambda bi, i: (bi, 0, E_VS // kvw)),
                  pl.BlockSpec((None, t, 2 * kvw), lambda bi, i: (bi, 0, E_KVW // (2 * kvw))),
                  pl.BlockSpec((None, NSA_HEADS, nchunk, qb), lambda bi, i: (i, 0, 0, 0)),
                  pl.BlockSpec((NSA_HEADS, 2, qb, qb), lambda bi, i: (0, 0, 0, 0))],
        out_specs=pl.BlockSpec((None, qb, qw), lambda bi, i: (bi, i, 0)),
        out_shape=jax.ShapeDtypeStruct((b, t, qw), BF16),
        scratch_shapes=[pltpu.VMEM((1, gq), F32), pltpu.VMEM((1, gq), F32),
                        pltpu.VMEM((HEAD_DIM, gq), F32), pltpu.VMEM((HEAD_DIM, gq), F32),
                        pltpu.VMEM((HEAD_DIM, gq), F32), pltpu.VMEM((t // SEL_BLOCK, qb), F32)],
        compiler_params=_cp("parallel", "parallel"),
        name="nsa_prompt",
    )(b31, h, h, kcmp, vcmp, h, h, h, tabc, tabd)


def _cmp_dec_kernel(pt_ref, *refs, n_pg):
    k_refs = refs[:n_pg]
    v_refs = refs[n_pg:2 * n_pg]
    wpos_ref = refs[2 * n_pg]
    ak_ref, bk_ref, av_ref, bv_ref = refs[2 * n_pg + 1:]
    per = PAGE_SIZE // CMP_STRIDE
    for idx, (srcs, a_ref, b_ref) in enumerate(((k_refs, ak_ref, bk_ref), (v_refs, av_ref, bv_ref))):
        for r in range(n_pg):
            for h in range(NSA_KV_HEADS):
                sl = slice(h * HEAD_DIM, (h + 1) * HEAD_DIM)
                x = srcs[r][pl.ds(h, PAGE_SIZE, stride=NSA_KV_HEADS), :]
                a, bsum = _chunk_sums(x, wpos_ref[idx, 0:CMP_STRIDE, sl], wpos_ref[idx, CMP_STRIDE:, sl])
                a_ref[r * per:(r + 1) * per, sl] = a
                b_ref[r * per:(r + 1) * per, sl] = bsum


def _cmp_dec(k_pool, v_pool, layer, page_table, wpos, n_pg=8):
    b, n_pages = page_table.shape
    kvw = NSA_KV_HEADS * HEAD_DIM
    per = PAGE_SIZE // CMP_STRIDE
    rows = PAGE_SIZE * NSA_KV_HEADS
    k_pool = k_pool.reshape(k_pool.shape[:2] + (rows, HEAD_DIM))
    v_pool = v_pool.reshape(v_pool.shape[:2] + (rows, HEAD_DIM))

    def page(r):
        return pl.BlockSpec((None, None, rows, HEAD_DIM), lambda bi, c, pt: (layer, pt[bi, c * n_pg + r], 0, 0))

    ospec = pl.BlockSpec((None, n_pg * per, kvw), lambda bi, c, pt: (bi, c, 0))
    out = jax.ShapeDtypeStruct((b, n_pages * per, kvw), F32)
    grid_spec = pltpu.PrefetchScalarGridSpec(
        num_scalar_prefetch=1,
        grid=(b, n_pages // n_pg),
        in_specs=[page(r) for r in range(n_pg)] * 2 + [pl.BlockSpec(wpos.shape, lambda bi, c, pt: (0, 0, 0))],
        out_specs=[ospec] * 4,
    )
    return pl.pallas_call(
        functools.partial(_cmp_dec_kernel, n_pg=n_pg),
        grid_spec=grid_spec,
        out_shape=[out] * 4,
        compiler_params=_cp("parallel", "parallel"),
        name="cmp_decode",
    )(page_table, *([k_pool] * n_pg), *([v_pool] * n_pg), wpos)


def _nsa_dec_cmp_kernel(q_ref, ak_ref, bk_ref, av_ref, bv_ref, wlin_ref, bias_ref, oc_ref, idx_ref,
                        *, pos, n_cmp, n_slc):
    nrow = ak_ref.shape[0]
    sw = idx_ref.shape[-1]
    lane_n = lax.broadcasted_iota(jnp.int32, (1, nrow), 1)
    cmask = ((lane_n * CMP_STRIDE + CMP_BLOCK - 1) <= pos) & (lane_n < n_cmp)
    on = lax.broadcasted_iota(jnp.int32, (nrow, sw), 0)
    osb = lax.broadcasted_iota(jnp.int32, (nrow, sw), 1)
    c_start = on * CMP_STRIDE
    overlap = jnp.where((c_start < osb * SEL_BLOCK + SEL_BLOCK) & (c_start + CMP_BLOCK - 1 >= osb * SEL_BLOCK)
                        & (on < n_cmp) & (osb < n_slc), 1.0, 0.0).astype(BF16)
    pk = ak_ref[...] + pltpu.roll(bk_ref[...], nrow - 1, 0)
    pv = av_ref[...] + pltpu.roll(bv_ref[...], nrow - 1, 0)
    lane_s = lax.broadcasted_iota(jnp.int32, (1, sw), 1)
    ri = lax.broadcasted_iota(jnp.int32, (sw, sw), 0)
    ci = lax.broadcasted_iota(jnp.int32, (sw, sw), 1)
    k_top = min(N_SEL, n_slc)
    cur = pos // SEL_BLOCK
    for h in range(NSA_KV_HEADS):
        sl = slice(h * HEAD_DIM, (h + 1) * HEAD_DIM)
        kc = jnp.dot(pk[:, sl].astype(BF16), wlin_ref[0, h], preferred_element_type=F32).astype(BF16)
        vc = jnp.dot(pv[:, sl].astype(BF16), wlin_ref[1, h], preferred_element_type=F32).astype(BF16)
        q = q_ref[h].astype(BF16)
        s = _nt(q, kc) * (HEAD_DIM ** -0.5) + bias_ref[h * NSA_GROUP:(h + 1) * NSA_GROUP]
        s = jnp.where(cmask, s, NEG_INF)
        s = s - jnp.max(s, axis=-1, keepdims=True)
        e = jnp.exp(s)
        pc = jnp.where(cmask, e / jnp.sum(e, axis=-1, keepdims=True), 0.0)
        oc_ref[h] = jnp.dot(pc.astype(BF16), vc, preferred_element_type=F32)
        pcs = jnp.sum(pc, axis=0, keepdims=True)
        hi, mid, lo = _split3(jnp.broadcast_to(pcs, (8, nrow)))
        imp = (jnp.dot(hi, overlap, preferred_element_type=F32) + jnp.dot(mid, overlap, preferred_element_type=F32)
               + jnp.dot(lo, overlap, preferred_element_type=F32))[0:1]
        forced = (lane_s == 0) | (lane_s == cur) | (lane_s == cur - 1)
        score = jnp.where(lane_s > cur, NEG_INF, jnp.where(forced, BIG, imp))
        score = jnp.where(lane_s < n_slc, score, -jnp.inf)
        rowm = jnp.broadcast_to(score, (sw, sw))
        colm = rowm.T
        beats = jnp.where(colm > rowm, 1, jnp.where(colm == rowm, jnp.where(ri < ci, 1, 0), 0))
        beats = jnp.where(ri < n_slc, beats, 0)
        rank = jnp.sum(beats, axis=0, keepdims=True)
        keep = (rank < k_top) & (score > 0.5 * NEG_INF) & (lane_s < n_slc)
        out = jnp.full((1, sw), -1, jnp.int32)
        for r in range(k_top):
            hit = keep & (rank == r)
            val = jnp.sum(jnp.where(hit, lane_s + 1, 0), axis=1, keepdims=True) - 1
            out = jnp.where(lane_s == r, val, out)
        idx_ref[h] = out


def _nsa_dec_cmp(q, ak, bk, av, bv, wlin, bias_c, pos, n_cmp, n_slc):
    b = q.shape[0]
    nrow = ak.shape[1]
    kvw = ak.shape[2]
    sw = -(-n_slc // LANES) * LANES
    part = pl.BlockSpec((None, nrow, kvw), lambda i: (i, 0, 0))
    kern = functools.partial(_nsa_dec_cmp_kernel, pos=pos, n_cmp=n_cmp, n_slc=n_slc)
    return pl.pallas_call(
        kern,
        name="nsa_dec_cmp",
        grid=(b,),
        in_specs=[pl.BlockSpec((None,) + q.shape[1:], lambda i: (i, 0, 0, 0)), part, part, part, part,
                  pl.BlockSpec(wlin.shape, lambda i: (0, 0, 0, 0)),
                  pl.BlockSpec(bias_c.shape, lambda i: (0, 0))],
        out_specs=[pl.BlockSpec((None,) + q.shape[1:], lambda i: (i, 0, 0, 0)),
                   pl.BlockSpec((None, NSA_KV_HEADS, 1, sw), lambda i: (i, 0, 0, 0))],
        out_shape=[jax.ShapeDtypeStruct(q.shape, F32), jax.ShapeDtypeStruct((b, NSA_KV_HEADS, 1, sw), jnp.int32)],
        compiler_params=_cp("parallel"),
    )(q, ak, bk, av, bv, wlin, bias_c)


def _bucket_bias(dist, rb):
    n = jnp.maximum(dist, 0)
    max_exact = N_BUCKETS // 2
    nf = jnp.maximum(n, 1).astype(F32)
    large = max_exact + (jnp.log(nf / max_exact) / math.log(MAX_DISTANCE / max_exact)
                         * (N_BUCKETS - max_exact)).astype(jnp.int32)
    bucket = jnp.where(n < max_exact, n, jnp.minimum(large, N_BUCKETS - 1))
    out = jnp.zeros((rb.shape[0], dist.shape[1]), F32)
    for k in range(N_BUCKETS):
        out = jnp.where(bucket == k, rb[:, k:k + 1], out)
    return out


def _nsa_dec_sel_kernel(pg_ref, hf_ref, blk_ref, q_ref, kn_ref, vn_ref, rb_ref, k0_ref, k1_ref, v0_ref, v1_ref,
                        o_ref, m_sc, l_sc, acc_sc, *, pos):
    bi = pl.program_id(0)
    j = pl.program_id(1)
    scale = HEAD_DIM ** -0.5
    lane = lax.broadcasted_iota(jnp.int32, (1, SEL_BLOCK), 1)

    @pl.when(j == 0)
    def _():
        for h in range(NSA_KV_HEADS):
            q = q_ref[h]
            rb = rb_ref[h]
            s = jnp.sum(q.astype(BF16).astype(F32) * kn_ref[h:h + 1, :].astype(BF16).astype(F32),
                        axis=-1, keepdims=True) * scale + rb[:, 0:1]
            m_sc[h] = s
            l_sc[h] = jnp.ones_like(s)
            acc_sc[h] = jnp.broadcast_to(vn_ref[h:h + 1, :].astype(BF16).astype(F32), (q.shape[0], HEAD_DIM))

    for h, (k_ref, v_ref) in enumerate(((k0_ref, v0_ref), (k1_ref, v1_ref))):
        blk = blk_ref[bi, h * pl.num_programs(1) + j]

        @pl.when(blk >= 0)
        def _(h=h, k_ref=k_ref, v_ref=v_ref, blk=blk):
            q = q_ref[h].astype(BF16)
            k = k_ref[:, h, :].astype(BF16)
            v = v_ref[:, h, :].astype(BF16)
            tok = blk * SEL_BLOCK + lane
            ok = tok <= pos
            s = _nt(q, k) * scale + _bucket_bias(pos - tok, rb_ref[h])
            s = jnp.where(ok, s, NEG_INF)
            m_prev = m_sc[h]
            m_new = jnp.maximum(m_prev, jnp.max(s, axis=-1, keepdims=True))
            alpha = jnp.exp(m_prev - m_new)
            p = jnp.where(ok, jnp.exp(s - m_new), 0.0)
            l_sc[h] = alpha * l_sc[h] + jnp.sum(p, axis=-1, keepdims=True)
            acc_sc[h] = alpha * acc_sc[h] + jnp.dot(p.astype(BF16), v, preferred_element_type=F32)
            m_sc[h] = m_new

    @pl.when(j == pl.num_programs(1) - 1)
    def _():
        for h in range(NSA_KV_HEADS):
            o_ref[h] = acc_sc[h] / l_sc[h]


def _nsa_dec_sel(q, k_new, v_new, rb, k_pool, v_pool, layer, pages, halves, blocks, pos):
    b = q.shape[0]
    k_top = blocks.shape[-1]
    pages, halves, blocks = (a.reshape(b, NSA_KV_HEADS * k_top) for a in (pages, halves, blocks))

    def blkspec(h):
        return pl.BlockSpec((None, None, SEL_BLOCK, NSA_KV_HEADS, HEAD_DIM),
                            lambda bi, j, pg, hf, bl: (layer, pg[bi, h * k_top + j], hf[bi, h * k_top + j], 0, 0))

    qspec = pl.BlockSpec((None,) + q.shape[1:], lambda bi, j, pg, hf, bl: (bi, 0, 0, 0))
    nspec = pl.BlockSpec((None, NSA_KV_HEADS, HEAD_DIM), lambda bi, j, pg, hf, bl: (bi, 0, 0))
    grid_spec = pltpu.PrefetchScalarGridSpec(
        num_scalar_prefetch=3,
        grid=(b, k_top),
        in_specs=[qspec, nspec, nspec, pl.BlockSpec(rb.shape, lambda bi, j, pg, hf, bl: (0, 0, 0)),
                  blkspec(0), blkspec(1), blkspec(0), blkspec(1)],
        out_specs=qspec,
        scratch_shapes=[pltpu.VMEM((NSA_KV_HEADS, NSA_GROUP, 1), F32), pltpu.VMEM((NSA_KV_HEADS, NSA_GROUP, 1), F32),
                        pltpu.VMEM((NSA_KV_HEADS, NSA_GROUP, HEAD_DIM), F32)],
    )
    return pl.pallas_call(
        functools.partial(_nsa_dec_sel_kernel, pos=pos),
        name="nsa_dec_sel",
        grid_spec=grid_spec,
        out_shape=jax.ShapeDtypeStruct(q.shape, F32),
        compiler_params=_cp("parallel", "arbitrary"),
    )(pages, halves, blocks, q, k_new, v_new, rb, k_pool, k_pool, v_pool, v_pool)


def _nsa_dec_win_kernel(q_ref, g_ref, win_ref, new_ref, bias_ref, oc_ref, os_ref, y_ref, nw_ref):
    scale = HEAD_DIM ** -0.5
    nwin = win_ref.shape[0]
    sig = jax.nn.sigmoid(g_ref[...])
    for h in range(NSA_KV_HEADS):
        q = q_ref[h].astype(BF16)
        k = win_ref[:, 0, h, :].astype(BF16)
        v = win_ref[:, 1, h, :].astype(BF16)
        kn = new_ref[0, h:h + 1, :].astype(BF16)
        vn = new_ref[1, h:h + 1, :].astype(BF16)
        bias = bias_ref[h * NSA_GROUP:(h + 1) * NSA_GROUP]
        s = _nt(q, k) * scale + bias[:, :nwin]
        s_new = jnp.sum(q.astype(F32) * kn.astype(F32), axis=-1, keepdims=True) * scale + bias[:, nwin:nwin + 1]
        m = jnp.maximum(jnp.max(s, axis=-1, keepdims=True), s_new)
        p = jnp.exp(s - m)
        p_new = jnp.exp(s_new - m)
        den = jnp.sum(p, axis=-1, keepdims=True) + p_new
        ow = (jnp.dot(p.astype(BF16), v, preferred_element_type=F32)
              + p_new.astype(BF16).astype(F32) * vn.astype(F32)) / den
        gs = sig[h]
        y_ref[h] = (gs[:, 0:1] * oc_ref[h] + gs[:, 1:2] * os_ref[h] + gs[:, 2:3] * ow).astype(y_ref.dtype)
    nw_ref[pl.ds(0, nwin - 1)] = win_ref[pl.ds(1, nwin - 1)]
    nw_ref[nwin - 1] = new_ref[...]


def _nsa_dec_win(q, gates, win, new_kv, bias_w, o_c, o_s):
    b = q.shape[0]
    qspec = pl.BlockSpec((None,) + q.shape[1:], lambda i: (i, 0, 0, 0))
    wspec = pl.BlockSpec((None,) + win.shape[1:], lambda i: (i, 0, 0, 0, 0))
    bias2 = bias_w[:, 0, :]
    return pl.pallas_call(
        _nsa_dec_win_kernel,
        name="nsa_dec_win",
        grid=(b,),
        in_specs=[qspec, pl.BlockSpec((None,) + gates.shape[1:], lambda i: (i, 0, 0, 0)), wspec,
                  pl.BlockSpec((None,) + new_kv.shape[1:], lambda i: (i, 0, 0, 0)),
                  pl.BlockSpec(bias2.shape, lambda i: (0, 0)), qspec, qspec],
        out_specs=[qspec, wspec],
        out_shape=[jax.ShapeDtypeStruct(q.shape, BF16), jax.ShapeDtypeStruct(win.shape, F32)],
        compiler_params=_cp("parallel"),
    )(q, gates, win, new_kv, bias2, o_c, o_s)


def _tail(x, xb_unused, y_mix, mem_kv, ffn_prev, p, l, bsz, t, alpha):
    d = x.shape[1]
    x1, x1b = _add_ln(x, y_mix, p['ln_g'][l, 0], p['ln_b'][l, 0], alpha)
    tm = 1024
    q = _mm(x1b, p['w_cq'][l], F32, tm, 512, "mm_cq")
    o = _cross(q.reshape(bsz, t, -1), mem_kv, 512).reshape(bsz * t, -1)
    yc = _mm(o, p['w_co'][l], F32, tm, 512, "mm_co")
    x2, x2b = _add_ln(x1, yc, p['ln_g'][l, 1], p['ln_b'][l, 1], alpha)
    if t > 1:
        act, st1, st2 = _ffn_up(x2b, p['w_up'], l, p['ffn_conv'][l], t)
        ffn_new = jnp.concatenate([st1[:, 6:], st2[:, 6:]], axis=-1)
    else:
        act, ffn_new = _ffn_up_dec(x2b, p['w_up'], l, p['ffn_conv'][l], ffn_prev)
    dff = act.shape[1]
    f = _mm_acc(act, p['w_down'][l], F32, 1024, 512, dff // 2)
    x3, x3b = _add_ln(x2, f, p['ln_g'][l, 2], p['ln_b'][l, 2], alpha)
    return x3, x3b, ffn_new


def kernel(x_prompt, x_sample, mem_prompt, state_pool, cache_nsa_cmp_k, cache_nsa_cmp_v, cache_nsa_sel_k, cache_nsa_sel_v, state_nsa_win, state_sc, cache_sb_k, cache_sb_v, state_ffn, cache_mem, page_table, w_in_even, w_pool, pool_scale, w_cmp_pos, w_cmp_lin, rel_bias, w_out_even, w_in_odd, sc_conv, w_out_odd, w_cq, w_ckv, w_co, w_up, ffn_conv, w_down, ln_g, ln_b):
    bp, t, d = x_prompt.shape
    bs = x_sample.shape[0]
    depth = w_cq.shape[0]
    n_pages = page_table.shape[1]
    past = n_pages * PAGE_SIZE
    alpha = (2.0 * depth) ** 0.25
    kvw = NSA_KV_HEADS * HEAD_DIM
    assert x_sample.shape[1] == 1 and state_nsa_win.shape[2] == WINDOW

    we = w_in_even
    w_even = jnp.concatenate([we[:, :, 1024:4096], we[:, :, :1024], we[:, :, 4096:],
                              jnp.zeros(we.shape[:2] + (E_TOT - we.shape[2],), we.dtype)], axis=2).astype(BF16)
    p = {'w_cq': w_cq.astype(BF16), 'w_co': w_co.astype(BF16), 'w_up': w_up,
         'w_down': w_down.astype(BF16), 'ffn_conv': ffn_conv, 'ln_g': ln_g, 'ln_b': ln_b}
    w_odd = w_in_odd.astype(BF16)
    w_oe = w_out_even.astype(BF16)
    w_oo = w_out_odd.astype(BF16)
    w_kv = w_ckv.astype(BF16)
    w_pool_b = w_pool.astype(BF16)
    w_lin_b = w_cmp_lin.astype(BF16)
    w_pos = w_cmp_pos.reshape(w_cmp_pos.shape[0], 2, CMP_BLOCK, kvw)

    qb = 128
    kj = np.arange(qb)[:, None]
    qt = np.arange(qb)[None, :]
    dist_d = np.stack([qt - kj, qb + qt - kj]).reshape(2 * qb, qb).astype(np.int32)
    tabd = _bias_lookup(rel_bias, jnp.asarray(dist_d), 2 * qb).reshape(NSA_HEADS, 2, qb, qb)
    nchunk = t // CMP_STRIDE
    dist_c = (np.arange(t)[None, :] - (np.arange(nchunk)[:, None] * CMP_STRIDE + CMP_BLOCK - 1)).astype(np.int32)
    tabc = _bias_lookup_blocked(rel_bias, jnp.asarray(dist_c), qb)
    b31 = rel_bias[N_BUCKETS - 1]

    mem_b = mem_prompt.reshape(bp * mem_prompt.shape[1], d).astype(BF16)
    n_mem = mem_prompt.shape[1]

    x = x_prompt.reshape(bp * t, d)
    xb = x.astype(BF16)
    outs_p = {k: [] for k in ('pool', 'cmp_k', 'cmp_v', 'sel_k', 'sel_v', 'win', 'sc', 'sb_k', 'sb_v', 'ffn', 'mem')}
    for l in range(depth):
        e = l // 2
        memkv = _mm(mem_b, w_kv[l], F32, 1024, 512)
        outs_p['mem'].append(memkv.reshape(bp, n_mem, 2, MEM_HEADS, MEM_HEAD_DIM))
        if l % 2 == 0:
            h = _mm(xb, w_even[e], F32, 1024, 512, "mm_in_even").reshape(bp, t, E_TOT)
            y_pool = _pool_prompt(h, E_U // 1024, w_pool_b[e], pool_scale[e])
            kcmp, vcmp = _compress_prompt(h, w_pos[e], w_lin_b[e])
            y_nsa = _nsa_prompt(h, kcmp, vcmp, tabc, tabd, b31)
            mix = jnp.concatenate([y_pool, y_nsa], axis=-1).reshape(bp * t, d)
            y = _mm(mix, w_oe[e], F32, 1024, 512, "mm_out")
            outs_p['pool'].append(h[:, t - POOL_STATE:, E_U:E_U + 1024])
            for name, off in (('cmp_k', E_KC), ('cmp_v', E_VC), ('sel_k', E_KS), ('sel_v', E_VS)):
                outs_p[name].append(h[:, :, off:off + kvw].reshape(bp, t, NSA_KV_HEADS, HEAD_DIM))
            nw = min(WINDOW, t)
            outs_p['win'].append(h[:, t - nw:, E_KVW:E_KVW + 2 * kvw].reshape(bp, nw, 2, NSA_KV_HEADS, HEAD_DIM))
        else:
            sbw = O_K - O_Q
            hc = _mm(xb, w_odd[e], F32, 1024, 512, "mm_in_odd", O_X, O_Q).reshape(bp, t, O_Q)
            hq = _mm(xb, w_odd[e], BF16, 1024, 512, "mm_in_odd", O_Q, sbw).reshape(bp, t, sbw)
            hk = _mm(xb, w_odd[e], F32, 1024, 512, "mm_in_odd", O_K, sbw).reshape(bp, t, sbw)
            hv = _mm(xb, w_odd[e], F32, 1024, 512, "mm_in_odd", O_V, sbw).reshape(bp, t, sbw)
            y_sc, sc_st = _sconv_prompt(hc, sc_conv[e])
            y_sb = _sb_prompt(hq, hk, hv)
            mix = jnp.concatenate([y_sc, y_sb], axis=-1).reshape(bp * t, d)
            y = _mm(mix, w_oo[e], F32, 1024, 512, "mm_out")
            outs_p['sc'].append(sc_st[:, 6:])
            nh = sbw // HEAD_DIM
            outs_p['sb_k'].append(hk.reshape(bp, t, nh, HEAD_DIM))
            outs_p['sb_v'].append(hv.reshape(bp, t, nh, HEAD_DIM))
        x, xb, ffn_new = _tail(x, xb, y, memkv.reshape(bp, n_mem, -1), None, p, l, bp, t, alpha)
        outs_p['ffn'].append(ffn_new)
    y_prompt = x.reshape(bp, t, d)

    pos = past
    length = past + 1
    n_cmp = (length - CMP_BLOCK) // CMP_STRIDE + 1
    n_slc = -(-length // SEL_BLOCK)
    k_top = min(N_SEL, n_slc)
    ncrow = n_pages * (PAGE_SIZE // CMP_STRIDE)
    dist_cd = (pos - (np.arange(ncrow) * CMP_STRIDE + CMP_BLOCK - 1)).astype(np.int32)
    bias_cd = _bias_lookup(rel_bias, jnp.asarray(np.broadcast_to(dist_cd, (8, ncrow))), 8)[:, 0, :]
    dist_wd = np.maximum(WINDOW - np.arange(WINDOW + LANES), 0).astype(np.int32)
    bias_wd = _bias_lookup(rel_bias, jnp.asarray(np.broadcast_to(dist_wd, (8, WINDOW + LANES))), 8)
    rb_hg = rel_bias.T.reshape(NSA_KV_HEADS, NSA_GROUP, N_BUCKETS)
    mem_s = cache_mem.reshape(depth, bs, cache_mem.shape[2], -1)

    x = x_sample.reshape(bs, d)
    xb = x.astype(BF16)
    outs_s = {k: [] for k in ('pool', 'cmp_k', 'cmp_v', 'sel_k', 'sel_v', 'win', 'sc', 'sb_k', 'sb_v', 'ffn')}
    for l in range(depth):
        e = l // 2
        if l % 2 == 0:
            h = _mm(xb, w_even[e], F32, 8, 512)
            y_pool, pool_new = _pool_dec(state_pool[e], h[:, E_U:E_U + 1024], w_pool_b[e], pool_scale[e], pos)
            q4 = h[:, E_Q:E_Q + NSA_HEADS * HEAD_DIM].reshape(bs, NSA_KV_HEADS, NSA_GROUP, HEAD_DIM)
            gates = h[:, E_G:E_G + NSA_HEADS * 3].reshape(bs, NSA_KV_HEADS, NSA_GROUP, 3)
            ak, bk, av, bv = _cmp_dec(cache_nsa_cmp_k, cache_nsa_cmp_v, e, page_table, w_pos[e])
            o_c, idx = _nsa_dec_cmp(q4, ak, bk, av, bv, w_lin_b[e], bias_cd, pos, n_cmp, n_slc)
            blocks = idx[:, :, 0, :k_top]
            past_blk = jnp.where((blocks >= 0) & (blocks * SEL_BLOCK < past), blocks, -1)
            safe = jnp.maximum(past_blk, 0)
            per_page = PAGE_SIZE // SEL_BLOCK
            pages = jnp.take_along_axis(page_table[:, None, :], safe // per_page, axis=2)
            halves = safe % per_page
            ks_new = h[:, E_KS:E_KS + kvw].reshape(bs, NSA_KV_HEADS, HEAD_DIM)
            vs_new = h[:, E_VS:E_VS + kvw].reshape(bs, NSA_KV_HEADS, HEAD_DIM)
            o_s = _nsa_dec_sel(q4, ks_new, vs_new, rb_hg, cache_nsa_sel_k, cache_nsa_sel_v, e,
                               pages, halves, past_blk, pos)
            new_kv = h[:, E_KVW:E_KVW + 2 * kvw].reshape(bs, 2, NSA_KV_HEADS, HEAD_DIM)
            y_nsa, win_new = _nsa_dec_win(q4, gates, state_nsa_win[e], new_kv, bias_wd, o_c, o_s)
            mix = jnp.concatenate([y_pool, y_nsa.reshape(bs, -1)], axis=-1)
            y = _mm(mix, w_oe[e], F32, 8, 512)
            outs_s['pool'].append(pool_new)
            for name, off in (('cmp_k', E_KC), ('cmp_v', E_VC), ('sel_k', E_KS), ('sel_v', E_VS)):
                outs_s[name].append(h[:, off:off + kvw].reshape(bs, 1, NSA_KV_HEADS, HEAD_DIM))
            outs_s['win'].append(win_new)
        else:
            h = _mm(xb, w_odd[e], F32, 8, 512)
            y_sc, sc_new = _sconv_dec(state_sc[e], h, sc_conv[e])
            nh = (O_K - O_Q) // HEAD_DIM
            q3 = h[:, O_Q:O_K].reshape(bs, nh, HEAD_DIM)
            y_sb = _sb_dec(q3, cache_sb_k, cache_sb_v, e, page_table)
            mix = jnp.concatenate([y_sc, y_sb.reshape(bs, -1)], axis=-1)
            y = _mm(mix, w_oo[e], F32, 8, 512)
            outs_s['sc'].append(sc_new)
            outs_s['sb_k'].append(h[:, O_K:O_V].reshape(bs, 1, nh, HEAD_DIM))
            outs_s['sb_v'].append(h[:, O_V:].reshape(bs, 1, nh, HEAD_DIM))
        x, xb, ffn_new = _tail(x, xb, y, mem_s[l], state_ffn[l], p, l, bs, 1, alpha)
        outs_s['ffn'].append(ffn_new)
    y_sample = x.reshape(bs, 1, d)

    sp = {k: jnp.stack(v) for k, v in outs_p.items()}
    ss = {k: jnp.stack(v) for k, v in outs_s.items()}
    return (y_prompt, y_sample,
            sp['pool'], sp['cmp_k'], sp['cmp_v'], sp['sel_k'], sp['sel_v'], sp['win'],
            sp['sc'], sp['sb_k'], sp['sb_v'], sp['ffn'], sp['mem'],
            ss['pool'], ss['cmp_k'], ss['cmp_v'], ss['sel_k'], ss['sel_v'], ss['win'],
            ss['sc'], ss['sb_k'], ss['sb_v'], ss['ffn'])
```

```python
import functools
import math

import jax
import jax.numpy as jnp
import numpy as np
from jax import lax
from jax.experimental import pallas as pl
from jax.experimental.pallas import tpu as pltpu

F32 = jnp.float32
BF16 = jnp.bfloat16

HEAD_DIM = 128
PAGE_SIZE = 128
POOL_WINDOWS = (2, 4, 8, 16)
POOL_GROUP_DIM = 256
POOL_STATE = 15
NSA_KV_HEADS = 2
NSA_GROUP = 12
NSA_HEADS = 24
CMP_BLOCK = 32
CMP_STRIDE = 16
SEL_BLOCK = 64
SEL_SHIFT = 6
N_SEL = 16
WINDOW = 512
N_BUCKETS = 32
MAX_DISTANCE = 128
MEM_HEADS = 4
MEM_HEAD_DIM = 256
LN_EPS = 1e-5
NEG_INF = -1e30
BIG = 1e30

LANES = 128
VMEM_LIMIT = 56 * 1024 * 1024

E_Q, E_U, E_KC, E_VC, E_KS, E_VS, E_KVW, E_G, E_TOT = 0, 3072, 4096, 4352, 4608, 4864, 5120, 5632, 6144
O_X, O_B, O_C, O_Q, O_K, O_V = 0, 1024, 2048, 3072, 6144, 9216


def _cp(*sem):
    return pltpu.CompilerParams(dimension_semantics=sem, vmem_limit_bytes=VMEM_LIMIT)


def _nt(a, b):
    return lax.dot_general(a, b, (((1,), (1,)), ((), ())), preferred_element_type=F32)


def _split3(x):
    hi = x.astype(BF16)
    r1 = x - hi.astype(F32)
    mid = r1.astype(BF16)
    lo = (r1 - mid.astype(F32)).astype(BF16)
    return hi, mid, lo


def _mm_kernel(x_ref, w_ref, o_ref):
    o_ref[...] = jnp.dot(x_ref[...], w_ref[...], preferred_element_type=F32).astype(o_ref.dtype)


def _mm(x, w, out_dtype, tm, tn, name="mm", col0=0, ncols=None, layer=None):
    m, k = x.shape
    n = w.shape[-1] - col0 if ncols is None else ncols
    tm = min(tm, m)
    c0 = col0 // tn
    assert col0 % tn == 0 and n % tn == 0
    if layer is None:
        wspec = pl.BlockSpec((k, tn), lambda i, j: (0, j + c0))
    else:
        wspec = pl.BlockSpec((None, k, tn), lambda i, j: (layer, 0, j + c0))
    return pl.pallas_call(
        _mm_kernel,
        name=name,
        grid=(m // tm, n // tn),
        in_specs=[pl.BlockSpec((tm, k), lambda i, j: (i, 0)), wspec],
        out_specs=pl.BlockSpec((tm, tn), lambda i, j: (i, j)),
        out_shape=jax.ShapeDtypeStruct((m, n), out_dtype),
        compiler_params=_cp("parallel", "parallel"),
    )(x, w)


def _mm_acc_kernel(x_ref, w_ref, o_ref, acc_ref):
    k = pl.program_id(2)

    @pl.when(k == 0)
    def _():
        acc_ref[...] = jnp.zeros_like(acc_ref)

    acc_ref[...] += jnp.dot(x_ref[...], w_ref[...], preferred_element_type=F32)

    @pl.when(k == pl.num_programs(2) - 1)
    def _():
        o_ref[...] = acc_ref[...].astype(o_ref.dtype)


def _mm_acc(x, w, layer, out_dtype, tm, tn, tk):
    m, k = x.shape
    n = w.shape[2]
    tm = min(tm, m)
    return pl.pallas_call(
        _mm_acc_kernel,
        name="mm_down",
        grid=(m // tm, n // tn, k // tk),
        in_specs=[pl.BlockSpec((tm, tk), lambda i, j, kk: (i, kk)),
                  pl.BlockSpec((None, tk, tn), lambda i, j, kk: (layer, kk, j))],
        out_specs=pl.BlockSpec((tm, tn), lambda i, j, kk: (i, j)),
        out_shape=jax.ShapeDtypeStruct((m, n), out_dtype),
        scratch_shapes=[pltpu.VMEM((tm, tn), F32)],
        compiler_params=_cp("parallel", "parallel", "arbitrary"),
    )(x, w)


def _ln_kernel(x_ref, y_ref, g_ref, b_ref, of_ref, ob_ref, *, alpha):
    z = alpha * x_ref[...] + y_ref[...]
    mu = jnp.mean(z, axis=-1, keepdims=True)
    zc = z - mu
    var = jnp.mean(zc * zc, axis=-1, keepdims=True)
    out = zc * lax.rsqrt(var + LN_EPS) * g_ref[...] + b_ref[...]
    of_ref[...] = out
    ob_ref[...] = out.astype(BF16)


def _add_ln(x, y, g, b, alpha, tm=256):
    m, d = x.shape
    tm = min(tm, m)
    row = pl.BlockSpec((tm, d), lambda i: (i, 0))
    vec = pl.BlockSpec((1, d), lambda i: (0, 0))
    return pl.pallas_call(
        functools.partial(_ln_kernel, alpha=alpha),
        name="add_ln",
        grid=(m // tm,),
        in_specs=[row, row, vec, vec],
        out_specs=[row, row],
        out_shape=[jax.ShapeDtypeStruct((m, d), F32), jax.ShapeDtypeStruct((m, d), BF16)],
        compiler_params=_cp("parallel"),
    )(x, y, g.reshape(1, d), b.reshape(1, d))


def _mm_ln_kernel(x_ref, w_ref, r_ref, g_ref, b_ref, of_ref, ob_ref, *, alpha):
    y = jnp.dot(x_ref[...], w_ref[...], preferred_element_type=F32)
    z = alpha * r_ref[...] + y
    mu = jnp.mean(z, axis=-1, keepdims=True)
    zc = z - mu
    var = jnp.mean(zc * zc, axis=-1, keepdims=True)
    out = zc * lax.rsqrt(var + LN_EPS) * g_ref[...] + b_ref[...]
    of_ref[...] = out
    ob_ref[...] = out.astype(BF16)


def _mm_add_ln(x, w, layer, res, g, b, alpha, tm=256):
    m, k = x.shape
    d = w.shape[2]
    tm = min(tm, m)
    row = pl.BlockSpec((tm, d), lambda i: (i, 0))
    vec = pl.BlockSpec((1, d), lambda i: (0, 0))
    return pl.pallas_call(
        functools.partial(_mm_ln_kernel, alpha=alpha),
        name="mm_add_ln",
        grid=(m // tm,),
        in_specs=[pl.BlockSpec((tm, k), lambda i: (i, 0)),
                  pl.BlockSpec((None, k, d), lambda i: (layer, 0, 0)), row, vec, vec],
        out_specs=[row, row],
        out_shape=[jax.ShapeDtypeStruct((m, d), F32), jax.ShapeDtypeStruct((m, d), BF16)],
        compiler_params=_cp("parallel"),
    )(x, w, res, g.reshape(1, d), b.reshape(1, d))


def _cross_kernel(q_ref, kv_ref, o_ref):
    width = MEM_HEADS * MEM_HEAD_DIM
    scale = MEM_HEAD_DIM ** -0.5
    for h in range(MEM_HEADS):
        sl = slice(h * MEM_HEAD_DIM, (h + 1) * MEM_HEAD_DIM)
        q = q_ref[:, sl].astype(BF16)
        k = kv_ref[:, sl].astype(BF16)
        v = kv_ref[:, width + h * MEM_HEAD_DIM: width + (h + 1) * MEM_HEAD_DIM].astype(BF16)
        s = _nt(q, k) * scale
        s = s - jnp.max(s, axis=-1, keepdims=True)
        e = jnp.exp(s)
        p = e / jnp.sum(e, axis=-1, keepdims=True)
        o_ref[:, sl] = jnp.dot(p.astype(BF16), v, preferred_element_type=F32).astype(o_ref.dtype)


def _cross(q, kv, tq):
    b, t, w = q.shape
    tq = min(tq, t)
    return pl.pallas_call(
        _cross_kernel,
        name="cross_attn",
        grid=(b, t // tq),
        in_specs=[pl.BlockSpec((None, tq, w), lambda i, j: (i, j, 0)),
                  pl.BlockSpec((None, kv.shape[1], kv.shape[2]), lambda i, j: (i, 0, 0))],
        out_specs=pl.BlockSpec((None, tq, w), lambda i, j: (i, j, 0)),
        out_shape=jax.ShapeDtypeStruct((b, t, w), BF16),
        compiler_params=_cp("parallel", "parallel"),
    )(q, kv)


def _shift_rows(h, c_last2, row):
    s1 = jnp.where(row == 0, c_last2[7:8], pltpu.roll(h, 1, 0))
    s2 = jnp.where(row == 0, c_last2[6:7], jnp.where(row == 1, c_last2[7:8], pltpu.roll(h, 2, 0)))
    return s1, s2


def _ffn_up_kernel(x_ref, w1_ref, w2_ref, cw1_ref, cw2_ref, act_ref, st1_ref, st2_ref, *, sub):
    t = x_ref.shape[0]
    tn = w1_ref.shape[1]
    ns = t // sub
    w1 = w1_ref[...].astype(BF16)
    w2 = w2_ref[...].astype(BF16)
    cw1 = cw1_ref[...]
    cw2 = cw2_ref[...]
    row = lax.broadcasted_iota(jnp.int32, (sub, 1), 0)
    zeros8 = jnp.zeros((8, tn), F32)
    h = [None] * ns

    def dots(s):
        xs = x_ref[s * sub:(s + 1) * sub, :]
        return jnp.dot(xs, w1, preferred_element_type=F32), jnp.dot(xs, w2, preferred_element_type=F32)

    def conv(hc, prev8, cw):
        s1, s2 = _shift_rows(hc, prev8, row)
        return s2 * cw[0:1] + s1 * cw[1:2] + hc * cw[2:3]

    def epilogue(s):
        p1, p2 = (zeros8, zeros8) if s == 0 else (h[s - 1][0][sub - 8:], h[s - 1][1][sub - 8:])
        c1 = conv(h[s][0], p1, cw1)
        c2 = conv(h[s][1], p2, cw2)
        act_ref[s * sub:(s + 1) * sub, :] = (c1 * jax.nn.sigmoid(c1) * c2).astype(BF16)

    h[0] = dots(0)
    for s in range(1, ns):
        h[s] = dots(s)
        epilogue(s - 1)
    epilogue(ns - 1)
    st1_ref[...] = h[ns - 1][0][sub - 8:]
    st2_ref[...] = h[ns - 1][1][sub - 8:]


def _ffn_up(x, w_up, layer, conv_w, t, sub=512, tn=256):
    m, d = x.shape
    dff = w_up.shape[2] // 2
    nt = dff // tn
    sub = min(sub, t)
    bsz = m // t
    return pl.pallas_call(
        functools.partial(_ffn_up_kernel, sub=sub),
        grid=(bsz, nt),
        in_specs=[pl.BlockSpec((t, d), lambda i, j: (i, 0), pipeline_mode=pl.Buffered(1)),
                  pl.BlockSpec((None, d, tn), lambda i, j: (layer, 0, j)),
                  pl.BlockSpec((None, d, tn), lambda i, j: (layer, 0, j + nt)),
                  pl.BlockSpec((3, tn), lambda i, j: (0, j)),
                  pl.BlockSpec((3, tn), lambda i, j: (0, j + nt))],
        out_specs=[pl.BlockSpec((t, tn), lambda i, j: (i, j)),
                   pl.BlockSpec((None, 8, tn), lambda i, j: (i, 0, j)),
                   pl.BlockSpec((None, 8, tn), lambda i, j: (i, 0, j))],
        out_shape=[jax.ShapeDtypeStruct((m, dff), BF16),
                   jax.ShapeDtypeStruct((bsz, 8, dff), F32),
                   jax.ShapeDtypeStruct((bsz, 8, dff), F32)],
        compiler_params=_cp("parallel", "parallel"),
        name="ffn_up",
    )(x, w_up, w_up, conv_w, conv_w)


def _ffn_up_dec_kernel(x_ref, w1_ref, w2_ref, cw1_ref, cw2_ref, p1_ref, p2_ref, act_ref, n1_ref, n2_ref):
    x = x_ref[...]
    h1 = jnp.dot(x, w1_ref[...].astype(BF16), preferred_element_type=F32)
    h2 = jnp.dot(x, w2_ref[...].astype(BF16), preferred_element_type=F32)

    def conv(h, p_ref, cw):
        return p_ref[:, 0, :] * cw[0:1] + p_ref[:, 1, :] * cw[1:2] + h * cw[2:3]

    c1 = conv(h1, p1_ref, cw1_ref[...])
    c2 = conv(h2, p2_ref, cw2_ref[...])
    act_ref[...] = (c1 * jax.nn.sigmoid(c1) * c2).astype(BF16)
    n1_ref[:, 0, :] = p1_ref[:, 1, :]
    n1_ref[:, 1, :] = h1
    n2_ref[:, 0, :] = p2_ref[:, 1, :]
    n2_ref[:, 1, :] = h2


def _ffn_up_dec(x, w_up, layer, conv_w, prev, tn=256):
    b, d = x.shape
    dff = w_up.shape[2] // 2
    nt = dff // tn
    pspec1 = pl.BlockSpec((b, 2, tn), lambda j: (0, 0, j))
    pspec2 = pl.BlockSpec((b, 2, tn), lambda j: (0, 0, j + nt))
    act, n1, n2 = pl.pallas_call(
        _ffn_up_dec_kernel,
        name="ffn_up_dec",
        grid=(nt,),
        in_specs=[pl.BlockSpec((b, d), lambda j: (0, 0)),
                  pl.BlockSpec((None, d, tn), lambda j: (layer, 0, j)),
                  pl.BlockSpec((None, d, tn), lambda j: (layer, 0, j + nt)),
                  pl.BlockSpec((3, tn), lambda j: (0, j)),
                  pl.BlockSpec((3, tn), lambda j: (0, j + nt)),
                  pspec1, pspec2],
        out_specs=[pl.BlockSpec((b, tn), lambda j: (0, j)),
                   pl.BlockSpec((b, 2, tn), lambda j: (0, 0, j)),
                   pl.BlockSpec((b, 2, tn), lambda j: (0, 0, j))],
        out_shape=[jax.ShapeDtypeStruct((b, dff), BF16),
                   jax.ShapeDtypeStruct((b, 2, dff), F32),
                   jax.ShapeDtypeStruct((b, 2, dff), F32)],
        compiler_params=_cp("parallel"),
    )(x, w_up, w_up, conv_w, conv_w, prev, prev)
    return act, jnp.concatenate([n1, n2], axis=-1)


def _bias_kernel(rb_ref, dist_ref, o_ref):
    head = pl.program_id(0)
    n = jnp.maximum(dist_ref[...], 0)
    max_exact = N_BUCKETS // 2
    nf = jnp.maximum(n, 1).astype(F32)
    large = max_exact + (jnp.log(nf / max_exact) / math.log(MAX_DISTANCE / max_exact)
                         * (N_BUCKETS - max_exact)).astype(jnp.int32)
    bucket = jnp.where(n < max_exact, n, jnp.minimum(large, N_BUCKETS - 1))
    out = jnp.zeros(n.shape, F32)
    for k in range(N_BUCKETS):
        out = jnp.where(bucket == k, rb_ref[k, head], out)
    o_ref[...] = out


def _bias_lookup(rel_bias, dist, tr):
    r, c = dist.shape
    nh = rel_bias.shape[1]
    tr = min(tr, r)
    return pl.pallas_call(
        _bias_kernel,
        grid=(nh, r // tr),
        in_specs=[pl.BlockSpec(memory_space=pltpu.SMEM),
                  pl.BlockSpec((tr, c), lambda h, i: (i, 0))],
        out_specs=pl.BlockSpec((None, tr, c), lambda h, i: (h, i, 0)),
        out_shape=jax.ShapeDtypeStruct((nh, r, c), F32),
        compiler_params=_cp("parallel", "parallel"),
        name="bias_lookup",
    )(rel_bias, dist)


def _bias_blocked_kernel(rb_ref, dist_ref, o_ref):
    n = jnp.maximum(dist_ref[...], 0)
    max_exact = N_BUCKETS // 2
    nf = jnp.maximum(n, 1).astype(F32)
    large = max_exact + (jnp.log(nf / max_exact) / math.log(MAX_DISTANCE / max_exact)
                         * (N_BUCKETS - max_exact)).astype(jnp.int32)
    bucket = jnp.where(n < max_exact, n, jnp.minimum(large, N_BUCKETS - 1))
    hits = [bucket == k for k in range(N_BUCKETS)]

    def head(hd, carry):
        out = jnp.zeros(n.shape, F32)
        for k in range(N_BUCKETS):
            out = jnp.where(hits[k], rb_ref[k, hd], out)
        o_ref[hd] = out
        return carry

    lax.fori_loop(0, o_ref.shape[0], head, 0)


def _bias_lookup_blocked(rel_bias, dist, tc):
    r, c = dist.shape
    nh = rel_bias.shape[1]
    return pl.pallas_call(
        _bias_blocked_kernel,
        grid=(c // tc,),
        in_specs=[pl.BlockSpec(memory_space=pltpu.SMEM),
                  pl.BlockSpec((r, tc), lambda j: (0, j))],
        out_specs=pl.BlockSpec((None, nh, r, tc), lambda j: (j, 0, 0, 0)),
        out_shape=jax.ShapeDtypeStruct((c // tc, nh, r, tc), F32),
        compiler_params=_cp("parallel"),
        name="bias_lookup_blocked",
    )(rel_bias, dist)


def _pool_groups(sums, cur, pos, wp_ref, sc_ref, y_ref):
    for g, w in enumerate(POOL_WINDOWS):
        sl = slice(g * POOL_GROUP_DIM, (g + 1) * POOL_GROUP_DIM)
        cnt = jnp.minimum(w, pos + 1).astype(F32)
        d = sums[g] / cnt - cur[:, sl]
        y = jnp.dot(d.astype(BF16), wp_ref[g], preferred_element_type=F32) * sc_ref[:, sl]
        y_ref[:, sl] = y.astype(y_ref.dtype)


def _pool_kernel(prev_ref, cur_ref, wp_ref, sc_ref, y_ref):
    t = pl.program_id(1)
    tt = cur_ref.shape[0]
    cur = cur_ref[...]
    prev = jnp.where(t == 0, 0.0, prev_ref[...])
    ext = jnp.concatenate([prev, cur], axis=0)
    gd = POOL_GROUP_DIM
    s2 = ext + pltpu.roll(ext, 1, 0)
    x4 = s2[:, gd:]
    s4 = x4 + pltpu.roll(x4, 2, 0)
    x8 = s4[:, gd:]
    s8 = x8 + pltpu.roll(x8, 4, 0)
    x16 = s8[:, gd:]
    s16 = x16 + pltpu.roll(x16, 8, 0)
    sums = [s2[16:, :gd], s4[16:, :gd], s8[16:, :gd], s16[16:]]
    pos = t * tt + lax.broadcasted_iota(jnp.int32, (tt, 1), 0)
    _pool_groups(sums, cur, pos, wp_ref, sc_ref, y_ref)


def _pool_prompt(h, col_blk, w_pool, pool_scale, tt=256):
    b, t, _ = h.shape
    wdt = w_pool.shape[0] * POOL_GROUP_DIM
    tt = min(tt, t)
    r = tt // 16
    return pl.pallas_call(
        _pool_kernel,
        name="pool_prompt",
        grid=(b, t // tt),
        in_specs=[pl.BlockSpec((None, 16, wdt), lambda i, j: (i, jnp.maximum(j * r - 1, 0), col_blk)),
                  pl.BlockSpec((None, tt, wdt), lambda i, j: (i, j, col_blk)),
                  pl.BlockSpec(w_pool.shape, lambda i, j: (0, 0, 0)),
                  pl.BlockSpec((1, wdt), lambda i, j: (0, 0))],
        out_specs=pl.BlockSpec((None, tt, wdt), lambda i, j: (i, j, 0)),
        out_shape=jax.ShapeDtypeStruct((b, t, wdt), BF16),
        compiler_params=_cp("parallel", "parallel"),
    )(h, h, w_pool, pool_scale.reshape(1, wdt))


def _pool_dec_kernel(sp_ref, u_ref, wp_ref, sc_ref, y_ref, ns_ref, *, pos):
    u = u_ref[...]
    gd = POOL_GROUP_DIM
    acc = u
    sums = []
    back = 1
    for g, w in enumerate(POOL_WINDOWS):
        while back < w:
            acc = acc + sp_ref[:, POOL_STATE - back, :]
            back += 1
        sums.append(acc[:, g * gd:(g + 1) * gd])
    posv = jnp.full((u.shape[0], 1), pos, jnp.int32)
    _pool_groups(sums, u, posv, wp_ref, sc_ref, y_ref)
    for j in range(POOL_STATE - 1):
        ns_ref[:, j, :] = sp_ref[:, j + 1, :]
    ns_ref[:, POOL_STATE - 1, :] = u


def _pool_dec(state, u, w_pool, pool_scale, pos):
    b, wdt = u.shape
    return pl.pallas_call(
        functools.partial(_pool_dec_kernel, pos=pos),
        out_shape=[jax.ShapeDtypeStruct((b, wdt), BF16), jax.ShapeDtypeStruct(state.shape, F32)],
        compiler_params=pltpu.CompilerParams(vmem_limit_bytes=VMEM_LIMIT),
    )(state, u, w_pool, pool_scale.reshape(1, wdt))


def _sconv_kernel(px_ref, pc_ref, x_ref, b_ref, c_ref, cw_ref, y_ref, st_ref):
    t = pl.program_id(1)
    tt = x_ref.shape[0]
    v = c_ref[...] * x_ref[...]
    pv = jnp.where(t == 0, 0.0, pc_ref[...] * px_ref[...])
    row = lax.broadcasted_iota(jnp.int32, (tt, 1), 0)
    s1, s2 = _shift_rows(v, pv, row)
    cw = cw_ref[...]
    conv = s2 * cw[0:1] + s1 * cw[1:2] + v * cw[2:3]
    y_ref[...] = (b_ref[...] * conv).astype(y_ref.dtype)
    st_ref[...] = v[tt - 8:]


def _sconv_prompt(h, conv_w, tt=256):
    b, t, _ = h.shape
    wdt = conv_w.shape[1]
    tt = min(tt, t)
    r = tt // 8

    def prev(blk):
        return pl.BlockSpec((None, 8, wdt), lambda i, j: (i, jnp.maximum(j * r - 1, 0), blk))

    def cur(blk):
        return pl.BlockSpec((None, tt, wdt), lambda i, j: (i, j, blk))

    return pl.pallas_call(
        _sconv_kernel,
        name="sconv_prompt",
        grid=(b, t // tt),
        in_specs=[prev(0), prev(2), cur(0), cur(1), cur(2), pl.BlockSpec((3, wdt), lambda i, j: (0, 0))],
        out_specs=[pl.BlockSpec((None, tt, wdt), lambda i, j: (i, j, 0)),
                   pl.BlockSpec((None, 8, wdt), lambda i, j: (i, 0, 0))],
        out_shape=[jax.ShapeDtypeStruct((b, t, wdt), BF16), jax.ShapeDtypeStruct((b, 8, wdt), F32)],
        compiler_params=_cp("parallel", "arbitrary"),
    )(h, h, h, h, h, conv_w)


def _sconv_dec_kernel(st_ref, x_ref, b_ref, c_ref, cw_ref, y_ref, ns_ref):
    v = c_ref[...] * x_ref[...]
    cw = cw_ref[...]
    conv = st_ref[:, 0, :] * cw[0:1] + st_ref[:, 1, :] * cw[1:2] + v * cw[2:3]
    y_ref[...] = (b_ref[...] * conv).astype(y_ref.dtype)
    ns_ref[:, 0, :] = st_ref[:, 1, :]
    ns_ref[:, 1, :] = v


def _sconv_dec(state, h, conv_w):
    b = h.shape[0]
    wdt = conv_w.shape[1]

    def col(blk):
        return pl.BlockSpec((b, wdt), lambda i: (0, blk))

    return pl.pallas_call(
        _sconv_dec_kernel,
        grid=(1,),
        in_specs=[pl.BlockSpec(state.shape, lambda i: (0, 0, 0)), col(0), col(1), col(2),
                  pl.BlockSpec((3, wdt), lambda i: (0, 0))],
        out_specs=[pl.BlockSpec((b, wdt), lambda i: (0, 0)), pl.BlockSpec(state.shape, lambda i: (0, 0, 0))],
        out_shape=[jax.ShapeDtypeStruct((b, wdt), BF16), jax.ShapeDtypeStruct(state.shape, F32)],
        compiler_params=_cp("arbitrary"),
    )(state, h, h, h, conv_w)


def _softplus(z):
    return jnp.maximum(z, 0.0) + jnp.log(1.0 + jnp.exp(-jnp.abs(z)))


def _suffix_sums(x, upper):
    r = x.shape[0]
    hi = x.astype(BF16)
    lo = (x - hi.astype(F32)).astype(BF16)
    s = jnp.dot(jnp.concatenate([hi, lo], axis=0), upper, preferred_element_type=F32)
    return s[:r] + s[r:]


SB_HEADS_PER_STEP = 4


def _sb_kernel(q_ref, k_ref, v_ref, o_ref, ab_sc, r_sc, acc_sc, *, tq):
    i = pl.program_id(2)
    scale = HEAD_DIM ** -0.5
    nhs = SB_HEADS_PER_STEP
    hsl = [slice(j * HEAD_DIM, (j + 1) * HEAD_DIM) for j in range(nhs)]
    qs = [q_ref[:, sl].astype(BF16) for sl in hsl]
    rj = lax.broadcasted_iota(jnp.int32, (tq, tq), 0)
    cj = lax.broadcasted_iota(jnp.int32, (tq, tq), 1)
    upper = jnp.where(rj > cj, 1.0, 0.0).astype(BF16)

    before = jnp.concatenate([cj < rj] * nhs, axis=0)

    def logits(c):
        off = pl.multiple_of(c * tq, tq)
        return jnp.concatenate([_nt(qs[j], k_ref[pl.ds(off, tq), hsl[j]].astype(BF16)) for j in range(nhs)],
                               axis=0) * scale

    def weights(z, r_run, diag):
        sp = _softplus(z)
        if diag:
            sp = jnp.where(before, sp, 0.0)
        a = jnp.exp(z - sp - _suffix_sums(sp, upper) - r_run)
        if diag:
            a = jnp.where(before, a, 0.0)
        return a.astype(BF16), r_run + jnp.sum(sp, axis=1, keepdims=True)

    def accumulate(c):
        off = pl.multiple_of(c * tq, tq)
        for j in range(nhs):
            acc_sc[j] += jnp.dot(ab_sc[j * tq:(j + 1) * tq, :], v_ref[pl.ds(off, tq), hsl[j]].astype(BF16),
                                 preferred_element_type=F32)

    acc_sc[...] = jnp.zeros_like(acc_sc)
    ab_sc[...], r_sc[...] = weights(logits(i), jnp.zeros((nhs * tq, 1), F32), True)

    def body(s, carry):
        c = i - s
        accumulate(c + 1)
        ab_sc[...], r_sc[...] = weights(logits(c), r_sc[...], False)
        return carry

    lax.fori_loop(1, i + 1, body, 0)
    accumulate(0)
    for j in range(nhs):
        o_ref[:, hsl[j]] = acc_sc[j].astype(o_ref.dtype)


def _sb_prompt(q, k, v, tq=256):
    b, t, width = q.shape
    tq = min(tq, t)
    wdt = SB_HEADS_PER_STEP * HEAD_DIM
    return pl.pallas_call(
        functools.partial(_sb_kernel, tq=tq),
        grid=(b, width // wdt, t // tq),
        in_specs=[pl.BlockSpec((None, tq, wdt), lambda bi, hi, i: (bi, i, hi)),
                  pl.BlockSpec((None, t, wdt), lambda bi, hi, i: (bi, 0, hi)),
                  pl.BlockSpec((None, t, wdt), lambda bi, hi, i: (bi, 0, hi))],
        out_specs=pl.BlockSpec((None, tq, wdt), lambda bi, hi, i: (bi, i, hi)),
        out_shape=jax.ShapeDtypeStruct((b, t, width), BF16),
        scratch_shapes=[pltpu.VMEM((SB_HEADS_PER_STEP * tq, tq), BF16), pltpu.VMEM((SB_HEADS_PER_STEP * tq, 1), F32),
                        pltpu.VMEM((SB_HEADS_PER_STEP, tq, HEAD_DIM), F32)],
        compiler_params=_cp("parallel", "parallel", "parallel"),
        name="sb_prompt",
    )(q, k, v)


SB_PAGES_PER_STEP = 2


def _sb_dec_kernel(pt_ref, q_ref, *refs):
    npg = SB_PAGES_PER_STEP
    k_refs, v_refs = refs[:npg], refs[npg:2 * npg]
    o_ref, z_sc, r_sc, acc_sc = refs[2 * npg:]
    c = pl.program_id(1)
    scale = HEAD_DIM ** -0.5
    nh = q_ref.shape[0]

    @pl.when(c == 0)
    def _():
        z_sc[...] = jnp.zeros_like(z_sc)
        r_sc[...] = jnp.zeros_like(r_sc)
        acc_sc[...] = jnp.zeros_like(acc_sc)

    rj = lax.broadcasted_iota(jnp.int32, (PAGE_SIZE, PAGE_SIZE), 0)
    cj = lax.broadcasted_iota(jnp.int32, (PAGE_SIZE, PAGE_SIZE), 1)
    upper = jnp.where(rj > cj, 1.0, 0.0).astype(BF16)
    hrow = lax.broadcasted_iota(jnp.int32, (nh, HEAD_DIM), 0)

    def heads_on_lanes(ref):
        return jnp.concatenate([ref[pl.ds(j, PAGE_SIZE, stride=nh), :].astype(BF16) for j in range(nh)], axis=1)

    live = c > 0
    r_run = r_sc[...]
    weights = []
    for r in range(npg):
        z = z_sc[r]
        sp = jnp.where(live, _softplus(z), 0.0)
        weights.append(jnp.where(live, jnp.exp(z - sp - _suffix_sums(sp, upper) - r_run), 0.0).astype(BF16))
        r_run = r_run + jnp.sum(sp, axis=1, keepdims=True)
    r_sc[...] = r_run

    q = q_ref[...]
    q_bd = jnp.concatenate([jnp.where(hrow == j, q, 0.0) for j in range(nh)], axis=1).astype(BF16)
    for r in range(npg):
        z_sc[r] = _nt(q_bd, heads_on_lanes(k_refs[r])) * scale

    upd = jnp.zeros((nh, HEAD_DIM), F32)
    for r in range(npg):
        res = jnp.dot(weights[r], heads_on_lanes(v_refs[r]), preferred_element_type=F32)
        for j in range(nh):
            upd = upd + jnp.where(hrow == j, res[:, j * HEAD_DIM:(j + 1) * HEAD_DIM], 0.0)
    acc_sc[...] += upd

    @pl.when(c == pl.num_programs(1) - 1)
    def _():
        o_ref[...] = acc_sc[...].astype(o_ref.dtype)


def _sb_dec(q, k_pool, v_pool, layer, page_table):
    b, nh, _ = q.shape
    n_pages = page_table.shape[1]
    npg = SB_PAGES_PER_STEP
    assert n_pages % npg == 0
    n_steps = n_pages // npg
    rows = PAGE_SIZE * nh
    k_pool = k_pool.reshape(k_pool.shape[:2] + (rows, HEAD_DIM))
    v_pool = v_pool.reshape(v_pool.shape[:2] + (rows, HEAD_DIM))

    def k_spec(r):
        return pl.BlockSpec((None, None, rows, HEAD_DIM), lambda bi, c, pt: (
            layer, pt[bi, n_pages - 1 - (npg * jnp.minimum(c, n_steps - 1) + r)], 0, 0))

    def v_spec(r):
        return pl.BlockSpec((None, None, rows, HEAD_DIM), lambda bi, c, pt: (
            layer, pt[bi, n_pages - 1 - (npg * jnp.maximum(c - 1, 0) + r)], 0, 0))

    grid_spec = pltpu.PrefetchScalarGridSpec(
        num_scalar_prefetch=1,
        grid=(b, n_steps + 1),
        in_specs=[pl.BlockSpec((None, nh, HEAD_DIM), lambda bi, c, pt: (bi, 0, 0))]
        + [k_spec(r) for r in range(npg)] + [v_spec(r) for r in range(npg)],
        out_specs=pl.BlockSpec((None, nh, HEAD_DIM), lambda bi, c, pt: (bi, 0, 0)),
        scratch_shapes=[pltpu.VMEM((npg, nh, PAGE_SIZE), F32), pltpu.VMEM((nh, 1), F32),
                        pltpu.VMEM((nh, HEAD_DIM), F32)],
    )
    return pl.pallas_call(
        _sb_dec_kernel,
        grid_spec=grid_spec,
        out_shape=jax.ShapeDtypeStruct((b, nh, HEAD_DIM), BF16),
        compiler_params=_cp("parallel", "arbitrary"),
        name="sb_decode",
    )(page_table, q, *([k_pool] * npg), *([v_pool] * npg))


def _chunk_sums(x, w0, w1):
    x3 = x.reshape(x.shape[0] // CMP_STRIDE, CMP_STRIDE, x.shape[1])
    return jnp.sum(x3 * w0[None], axis=1), jnp.sum(x3 * w1[None], axis=1)


def _compress_kernel(kc_ref, vc_ref, wpos_ref, wlin_ref, ko_ref, vo_ref):
    nchunk = kc_ref.shape[0] // CMP_STRIDE
    for idx, (src, dst) in enumerate(((kc_ref, ko_ref), (vc_ref, vo_ref))):
        a, bsum = _chunk_sums(src[...], wpos_ref[idx, 0:CMP_STRIDE, :], wpos_ref[idx, CMP_STRIDE:, :])
        pooled = a + pltpu.roll(bsum, nchunk - 1, 0)
        for h in range(NSA_KV_HEADS):
            sl = slice(h * HEAD_DIM, (h + 1) * HEAD_DIM)
            dst[:, sl] = jnp.dot(pooled[:, sl].astype(BF16), wlin_ref[idx, h], preferred_element_type=F32)


def _compress_prompt(h, wpos, wlin):
    b, t, _ = h.shape
    kvw = NSA_KV_HEADS * HEAD_DIM
    nchunk = t // CMP_STRIDE
    out = jax.ShapeDtypeStruct((b, nchunk, kvw), F32)
    ospec = pl.BlockSpec((None, nchunk, kvw), lambda i: (i, 0, 0))
    return pl.pallas_call(
        _compress_kernel,
        name="compress_prompt",
        grid=(b,),
        in_specs=[pl.BlockSpec((None, t, kvw), lambda i: (i, 0, E_KC // kvw)),
                  pl.BlockSpec((None, t, kvw), lambda i: (i, 0, E_VC // kvw)),
                  pl.BlockSpec(wpos.shape, lambda i: (0, 0, 0)),
                  pl.BlockSpec(wlin.shape, lambda i: (0, 0, 0, 0))],
        out_specs=[ospec, ospec],
        out_shape=[out, out],
        compiler_params=_cp("parallel"),
    )(h, h, wpos, wlin)


def _topk_keep(score, k_top):
    srow = lax.broadcasted_iota(jnp.int32, score.shape, 0)
    rank = jnp.zeros(score.shape, jnp.int32)
    for s2 in range(score.shape[0]):
        row = score[s2:s2 + 1, :]
        rank = rank + jnp.where(row > score, 1, jnp.where(row == score, jnp.where(srow > s2, 1, 0), 0))
    return jnp.where(rank < k_top, jnp.where(score > 0.5 * NEG_INF, 1.0, 0.0), 0.0)


def _flash_init(m_sc, l_sc, acc_sc):
    m_sc[...] = jnp.full(m_sc.shape, NEG_INF, F32)
    l_sc[...] = jnp.zeros(l_sc.shape, F32)
    acc_sc[...] = jnp.zeros(acc_sc.shape, F32)


def _flash_chunk(q_all, k, v, bias_fn, mask, m_sc, l_sc, acc_sc):
    ng = NSA_GROUP
    r = q_all.shape[0] // ng
    s_all = _nt(k.astype(BF16), q_all) * (HEAD_DIM ** -0.5)
    v_t = v.T.astype(BF16)
    m_prev = m_sc[...]
    m_parts, p_parts = [], []
    for g in range(ng):
        sl = slice(g * r, (g + 1) * r)
        s = s_all[:, sl] + bias_fn(g)
        if mask is not None:
            s = jnp.where(mask, s, NEG_INF)
        mn = jnp.maximum(m_prev[:, sl], jnp.max(s, axis=0, keepdims=True))
        p_parts.append(jnp.exp(s - mn))
        m_parts.append(mn)
    m_new = jnp.concatenate(m_parts, axis=1)
    p = jnp.concatenate(p_parts, axis=1)
    alpha = jnp.exp(m_prev - m_new)
    l_sc[...] = alpha * l_sc[...] + jnp.sum(p, axis=0, keepdims=True)
    acc_sc[...] = alpha * acc_sc[...] + jnp.dot(v_t, p.astype(BF16), preferred_element_type=F32)
    m_sc[...] = m_new


def _nsa_kernel(b31_ref, q_ref, g_ref, kc_ref, vc_ref, ks_ref, vs_ref, kvw_ref, tabc_ref, tabd_ref, o_ref,
                m_sc, l_sc, acc_sc, oc_sc, os_sc, keep_sc):
    i = pl.program_id(1)
    qb = q_ref.shape[0]
    t_len = ks_ref.shape[0]
    nc = kc_ref.shape[0]
    n_cmp = (t_len - CMP_BLOCK) // CMP_STRIDE + 1
    n_slc = t_len // SEL_BLOCK
    k_top = min(N_SEL, n_slc)
    gsz = NSA_GROUP
    scale = HEAD_DIM ** -0.5
    keyi = lax.broadcasted_iota(jnp.int32, (qb, qb), 0)
    qi = lax.broadcasted_iota(jnp.int32, (qb, qb), 1)
    causal = keyi <= qi
    wedge = keyi >= qi
    first_half = keyi < SEL_BLOCK
    pos_row = i * qb + lax.broadcasted_iota(jnp.int32, (1, qb), 1)
    sig_t = jax.nn.sigmoid(g_ref[...]).T
    osb = lax.broadcasted_iota(jnp.int32, (n_slc, nc), 0)
    on = lax.broadcasted_iota(jnp.int32, (n_slc, nc), 1)
    c_start = on * CMP_STRIDE
    overlap_t = jnp.where((c_start < osb * SEL_BLOCK + SEL_BLOCK) & (c_start + CMP_BLOCK - 1 >= osb * SEL_BLOCK)
                          & (on < n_cmp), 1.0, 0.0).astype(BF16)
    nrow = lax.broadcasted_iota(jnp.int32, (nc, qb), 0)
    cmask = ((nrow * CMP_STRIDE + CMP_BLOCK - 1) <= pos_row) & (nrow < n_cmp)
    srow = lax.broadcasted_iota(jnp.int32, (n_slc, qb), 0)
    cur = pos_row >> SEL_SHIFT
    forced = (srow == 0) | (srow == cur) | (srow == cur - 1)
    future = srow > cur

    for h in range(NSA_KV_HEADS):
        hs = slice(h * HEAD_DIM, (h + 1) * HEAD_DIM)
        g0 = h * gsz
        q_all = jnp.concatenate(
            [q_ref[:, (g0 + g) * HEAD_DIM:(g0 + g + 1) * HEAD_DIM] for g in range(gsz)], axis=0).astype(BF16)

        s_all = _nt(kc_ref[:, hs].astype(BF16), q_all) * scale
        vc_t = vc_ref[:, hs].T.astype(BF16)
        pcs = jnp.zeros((nc, qb), F32)
        pc_parts = []
        for g in range(gsz):
            s = s_all[:, g * qb:(g + 1) * qb] + tabc_ref[g0 + g]
            s = jnp.where(cmask, s, NEG_INF)
            s = s - jnp.max(s, axis=0, keepdims=True)
            e = jnp.exp(s)
            pc = jnp.where(cmask, e / jnp.sum(e, axis=0, keepdims=True), 0.0)
            pcs = pcs + pc
            pc_parts.append(pc.astype(BF16))
        oc_sc[...] = jnp.dot(vc_t, jnp.concatenate(pc_parts, axis=1), preferred_element_type=F32)
        imp = sum(jnp.dot(overlap_t, piece, preferred_element_type=F32) for piece in _split3(pcs))
        score = jnp.where(future, NEG_INF, jnp.where(forced, BIG, imp))
        keep_sc[...] = _topk_keep(score, k_top)

        def sel_mask(c, nk=1):
            parts = []
            for u in range(nk):
                first = keep_sc[pl.ds(2 * (c + u), 1), :]
                second = keep_sc[pl.ds(2 * (c + u) + 1, 1), :]
                parts.append(jnp.where(first_half, first, second))
            return jnp.concatenate(parts, axis=0) > 0.5

        def chunk(k_ref, v_ref, c, ksl, vsl, bias_fn, mask, nk=1):
            off = pl.multiple_of(c * qb, qb)
            _flash_chunk(q_all, k_ref[pl.ds(off, nk * qb), ksl], v_ref[pl.ds(off, nk * qb), vsl], bias_fn, mask,
                         m_sc, l_sc, acc_sc)

        tab0 = lambda g: tabd_ref[g0 + g, 0]
        tab1 = lambda g: tabd_ref[g0 + g, 1]
        far = lambda g: b31_ref[g0 + g]

        _flash_init(m_sc, l_sc, acc_sc)
        chunk(ks_ref, vs_ref, i, hs, hs, tab0, sel_mask(i) & causal)

        @pl.when(i >= 1)
        def _():
            chunk(ks_ref, vs_ref, i - 1, hs, hs, tab1, sel_mask(i - 1))

        n_far = jnp.maximum(i - 1, 0)

        def far_body(pair, carry):
            chunk(ks_ref, vs_ref, 2 * pair, hs, hs, far, sel_mask(2 * pair, 2), 2)
            return carry

        lax.fori_loop(0, n_far // 2, far_body, 0)

        @pl.when(n_far % 2 == 1)
        def _():
            chunk(ks_ref, vs_ref, n_far - 1, hs, hs, far, sel_mask(n_far - 1))

        os_sc[...] = acc_sc[...] / l_sc[...]

        kw_sl = hs
        vw_sl = slice(NSA_KV_HEADS * HEAD_DIM + h * HEAD_DIM, NSA_KV_HEADS * HEAD_DIM + (h + 1) * HEAD_DIM)
        _flash_init(m_sc, l_sc, acc_sc)
        chunk(kvw_ref, kvw_ref, i, kw_sl, vw_sl, tab0, causal)

        @pl.when(i >= 1)
        def _():
            chunk(kvw_ref, kvw_ref, i - 1, kw_sl, vw_sl, tab1, None)

        @pl.when(i >= 3)
        def _():
            chunk(kvw_ref, kvw_ref, i - 3, kw_sl, vw_sl, far, None, 2)

        @pl.when(i == 2)
        def _():
            chunk(kvw_ref, kvw_ref, 0, kw_sl, vw_sl, far, None)

        @pl.when(i >= 4)
        def _():
            chunk(kvw_ref, kvw_ref, i - 4, kw_sl, vw_sl, far, wedge)

        ow = acc_sc[...] / l_sc[...]

        def gate(j):
            return jnp.concatenate([sig_t[(g0 + g) * 3 + j:(g0 + g) * 3 + j + 1, :] for g in range(gsz)], axis=1)

        y_t = gate(0) * oc_sc[...] + gate(1) * os_sc[...] + gate(2) * ow
        for g in range(gsz):
            o_ref[:, (g0 + g) * HEAD_DIM:(g0 + g + 1) * HEAD_DIM] = y_t[:, g * qb:(g + 1) * qb].T.astype(o_ref.dtype)


def _nsa_prompt(h, kcmp, vcmp, tabc, tabd, b31, qb=128):
    b, t, _ = h.shape
    assert qb == HEAD_DIM == 2 * SEL_BLOCK and WINDOW == 4 * qb and t % qb == 0
    qw = NSA_HEADS * HEAD_DIM
    kvw = NSA_KV_HEADS * HEAD_DIM
    nchunk = kcmp.shape[1]
    gq = NSA_GROUP * qb
    return pl.pallas_call(
        _nsa_kernel,
        grid=(b, t // qb),
        in_specs=[pl.BlockSpec(memory_space=pltpu.SMEM),
                  pl.BlockSpec((None, qb, qw), lambda bi, i: (bi, i, E_Q // qw)),
                  pl.BlockSpec((None, qb, LANES), lambda bi, i: (bi, i, E_G // LANES)),
                  pl.BlockSpec((None, nchunk, kvw), lambda bi, i: (bi, 0, 0)),
                  pl.BlockSpec((None, nchunk, kvw), lambda bi, i: (bi, 0, 0)),
                  pl.BlockSpec((None, t, kvw), lambda bi, i: (bi, 0, E_KS // kvw)),
                  pl.BlockSpec((None, t, kvw), lambda bi, i: (bi, 0, E_VS // kvw)),
                  pl.BlockSpec((None, t, 2 * kvw), lambda bi, i: (bi, 0, E_KVW // (2 * kvw))),
                  pl.BlockSpec((None, NSA_HEADS, nchunk, qb), lambda bi, i: (i, 0, 0, 0)),
                  pl.BlockSpec((NSA_HEADS, 2, qb, qb), lambda bi, i: (0, 0, 0, 0))],
        out_specs=pl.BlockSpec((None, qb, qw), lambda bi, i: (bi, i, 0)),
        out_shape=jax.ShapeDtypeStruct((b, t, qw), BF16),
        scratch_shapes=[pltpu.VMEM((1, gq), F32), pltpu.VMEM((1, gq), F32),
                        pltpu.VMEM((HEAD_DIM, gq), F32), pltpu.VMEM((HEAD_DIM, gq), F32),
                        pltpu.VMEM((HEAD_DIM, gq), F32), pltpu.VMEM((t // SEL_BLOCK, qb), F32)],
        compiler_params=_cp("parallel", "parallel"),
        name="nsa_prompt",
    )(b31, h, h, kcmp, vcmp, h, h, h, tabc, tabd)


def _cmp_dec_kernel(pt_ref, *refs, n_pg):
    k_refs = refs[:n_pg]
    v_refs = refs[n_pg:2 * n_pg]
    wpos_ref = refs[2 * n_pg]
    ak_ref, bk_ref, av_ref, bv_ref = refs[2 * n_pg + 1:]
    per = PAGE_SIZE // CMP_STRIDE
    for idx, (srcs, a_ref, b_ref) in enumerate(((k_refs, ak_ref, bk_ref), (v_refs, av_ref, bv_ref))):
        for r in range(n_pg):
            for h in range(NSA_KV_HEADS):
                sl = slice(h * HEAD_DIM, (h + 1) * HEAD_DIM)
                x = srcs[r][pl.ds(h, PAGE_SIZE, stride=NSA_KV_HEADS), :]
                a, bsum = _chunk_sums(x, wpos_ref[idx, 0:CMP_STRIDE, sl], wpos_ref[idx, CMP_STRIDE:, sl])
                a_ref[r * per:(r + 1) * per, sl] = a
                b_ref[r * per:(r + 1) * per, sl] = bsum


def _cmp_dec(k_pool, v_pool, layer, page_table, wpos, n_pg=8):
    b, n_pages = page_table.shape
    kvw = NSA_KV_HEADS * HEAD_DIM
    per = PAGE_SIZE // CMP_STRIDE
    rows = PAGE_SIZE * NSA_KV_HEADS
    k_pool = k_pool.reshape(k_pool.shape[:2] + (rows, HEAD_DIM))
    v_pool = v_pool.reshape(v_pool.shape[:2] + (rows, HEAD_DIM))

    def page(r):
        return pl.BlockSpec((None, None, rows, HEAD_DIM), lambda bi, c, pt: (layer, pt[bi, c * n_pg + r], 0, 0))

    ospec = pl.BlockSpec((None, n_pg * per, kvw), lambda bi, c, pt: (bi, c, 0))
    out = jax.ShapeDtypeStruct((b, n_pages * per, kvw), F32)
    grid_spec = pltpu.PrefetchScalarGridSpec(
        num_scalar_prefetch=1,
        grid=(b, n_pages // n_pg),
        in_specs=[page(r) for r in range(n_pg)] * 2 + [pl.BlockSpec(wpos.shape, lambda bi, c, pt: (0, 0, 0))],
        out_specs=[ospec] * 4,
    )
    return pl.pallas_call(
        functools.partial(_cmp_dec_kernel, n_pg=n_pg),
        grid_spec=grid_spec,
        out_shape=[out] * 4,
        compiler_params=_cp("parallel", "parallel"),
        name="cmp_decode",
    )(page_table, *([k_pool] * n_pg), *([v_pool] * n_pg), wpos)


def _nsa_dec_cmp_kernel(q_ref, ak_ref, bk_ref, av_ref, bv_ref, wlin_ref, bias_ref, oc_ref, idx_ref,
                        *, pos, n_cmp, n_slc):
    nrow = ak_ref.shape[0]
    sw = idx_ref.shape[-1]
    lane_n = lax.broadcasted_iota(jnp.int32, (1, nrow), 1)
    cmask = ((lane_n * CMP_STRIDE + CMP_BLOCK - 1) <= pos) & (lane_n < n_cmp)
    on = lax.broadcasted_iota(jnp.int32, (nrow, sw), 0)
    osb = lax.broadcasted_iota(jnp.int32, (nrow, sw), 1)
    c_start = on * CMP_STRIDE
    overlap = jnp.where((c_start < osb * SEL_BLOCK + SEL_BLOCK) & (c_start + CMP_BLOCK - 1 >= osb * SEL_BLOCK)
                        & (on < n_cmp) & (osb < n_slc), 1.0, 0.0).astype(BF16)
    pk = ak_ref[...] + pltpu.roll(bk_ref[...], nrow - 1, 0)
    pv = av_ref[...] + pltpu.roll(bv_ref[...], nrow - 1, 0)
    lane_s = lax.broadcasted_iota(jnp.int32, (1, sw), 1)
    ri = lax.broadcasted_iota(jnp.int32, (sw, sw), 0)
    ci = lax.broadcasted_iota(jnp.int32, (sw, sw), 1)
    k_top = min(N_SEL, n_slc)
    cur = pos // SEL_BLOCK
    for h in range(NSA_KV_HEADS):
        sl = slice(h * HEAD_DIM, (h + 1) * HEAD_DIM)
        kc = jnp.dot(pk[:, sl].astype(BF16), wlin_ref[0, h], preferred_element_type=F32).astype(BF16)
        vc = jnp.dot(pv[:, sl].astype(BF16), wlin_ref[1, h], preferred_element_type=F32).astype(BF16)
        q = q_ref[h].astype(BF16)
        s = _nt(q, kc) * (HEAD_DIM ** -0.5) + bias_ref[h * NSA_GROUP:(h + 1) * NSA_GROUP]
        s = jnp.where(cmask, s, NEG_INF)
        s = s - jnp.max(s, axis=-1, keepdims=True)
        e = jnp.exp(s)
        pc = jnp.where(cmask, e / jnp.sum(e, axis=-1, keepdims=True), 0.0)
        oc_ref[h] = jnp.dot(pc.astype(BF16), vc, preferred_element_type=F32)
        pcs = jnp.sum(pc, axis=0, keepdims=True)
        hi, mid, lo = _split3(jnp.broadcast_to(pcs, (8, nrow)))
        imp = (jnp.dot(hi, overlap, preferred_element_type=F32) + jnp.dot(mid, overlap, preferred_element_type=F32)
               + jnp.dot(lo, overlap, preferred_element_type=F32))[0:1]
        forced = (lane_s == 0) | (lane_s == cur) | (lane_s == cur - 1)
        score = jnp.where(lane_s > cur, NEG_INF, jnp.where(forced, BIG, imp))
        score = jnp.where(lane_s < n_slc, score, -jnp.inf)
        rowm = jnp.broadcast_to(score, (sw, sw))
        colm = rowm.T
        beats = jnp.where(colm > rowm, 1, jnp.where(colm == rowm, jnp.where(ri < ci, 1, 0), 0))
        beats = jnp.where(ri < n_slc, beats, 0)
        rank = jnp.sum(beats, axis=0, keepdims=True)
        keep = (rank < k_top) & (score > 0.5 * NEG_INF) & (lane_s < n_slc)
        out = jnp.full((1, sw), -1, jnp.int32)
        for r in range(k_top):
            hit = keep & (rank == r)
            val = jnp.sum(jnp.where(hit, lane_s + 1, 0), axis=1, keepdims=True) - 1
            out = jnp.where(lane_s == r, val, out)
        idx_ref[h] = out


def _nsa_dec_cmp(q, ak, bk, av, bv, wlin, bias_c, pos, n_cmp, n_slc):
    b = q.shape[0]
    nrow = ak.shape[1]
    kvw = ak.shape[2]
    sw = -(-n_slc // LANES) * LANES
    part = pl.BlockSpec((None, nrow, kvw), lambda i: (i, 0, 0))
    kern = functools.partial(_nsa_dec_cmp_kernel, pos=pos, n_cmp=n_cmp, n_slc=n_slc)
    return pl.pallas_call(
        kern,
        name="nsa_dec_cmp",
        grid=(b,),
        in_specs=[pl.BlockSpec((None,) + q.shape[1:], lambda i: (i, 0, 0, 0)), part, part, part, part,
                  pl.BlockSpec(wlin.shape, lambda i: (0, 0, 0, 0)),
                  pl.BlockSpec(bias_c.shape, lambda i: (0, 0))],
        out_specs=[pl.BlockSpec((None,) + q.shape[1:], lambda i: (i, 0, 0, 0)),
                   pl.BlockSpec((None, NSA_KV_HEADS, 1, sw), lambda i: (i, 0, 0, 0))],
        out_shape=[jax.ShapeDtypeStruct(q.shape, F32), jax.ShapeDtypeStruct((b, NSA_KV_HEADS, 1, sw), jnp.int32)],
        compiler_params=_cp("parallel"),
    )(q, ak, bk, av, bv, wlin, bias_c)


def _bucket_bias(dist, rb):
    n = jnp.maximum(dist, 0)
    max_exact = N_BUCKETS // 2
    nf = jnp.maximum(n, 1).astype(F32)
    large = max_exact + (jnp.log(nf / max_exact) / math.log(MAX_DISTANCE / max_exact)
                         * (N_BUCKETS - max_exact)).astype(jnp.int32)
    bucket = jnp.where(n < max_exact, n, jnp.minimum(large, N_BUCKETS - 1))
    out = jnp.zeros((rb.shape[0], dist.shape[1]), F32)
    for k in range(N_BUCKETS):
        out = jnp.where(bucket == k, rb[:, k:k + 1], out)
    return out


def _nsa_dec_sel_kernel(pg_ref, hf_ref, blk_ref, q_ref, kn_ref, vn_ref, rb_ref, k0_ref, k1_ref, v0_ref, v1_ref,
                        o_ref, m_sc, l_sc, acc_sc, *, pos):
    bi = pl.program_id(0)
    j = pl.program_id(1)
    scale = HEAD_DIM ** -0.5
    lane = lax.broadcasted_iota(jnp.int32, (1, SEL_BLOCK), 1)

    @pl.when(j == 0)
    def _():
        for h in range(NSA_KV_HEADS):
            q = q_ref[h]
            rb = rb_ref[h]
            s = jnp.sum(q.astype(BF16).astype(F32) * kn_ref[h:h + 1, :].astype(BF16).astype(F32),
                        axis=-1, keepdims=True) * scale + rb[:, 0:1]
            m_sc[h] = s
            l_sc[h] = jnp.ones_like(s)
            acc_sc[h] = jnp.broadcast_to(vn_ref[h:h + 1, :].astype(BF16).astype(F32), (q.shape[0], HEAD_DIM))

    for h, (k_ref, v_ref) in enumerate(((k0_ref, v0_ref), (k1_ref, v1_ref))):
        blk = blk_ref[bi, h * pl.num_programs(1) + j]

        @pl.when(blk >= 0)
        def _(h=h, k_ref=k_ref, v_ref=v_ref, blk=blk):
            q = q_ref[h].astype(BF16)
            k = k_ref[:, h, :].astype(BF16)
            v = v_ref[:, h, :].astype(BF16)
            tok = blk * SEL_BLOCK + lane
            ok = tok <= pos
            s = _nt(q, k) * scale + _bucket_bias(pos - tok, rb_ref[h])
            s = jnp.where(ok, s, NEG_INF)
            m_prev = m_sc[h]
            m_new = jnp.maximum(m_prev, jnp.max(s, axis=-1, keepdims=True))
            alpha = jnp.exp(m_prev - m_new)
            p = jnp.where(ok, jnp.exp(s - m_new), 0.0)
            l_sc[h] = alpha * l_sc[h] + jnp.sum(p, axis=-1, keepdims=True)
            acc_sc[h] = alpha * acc_sc[h] + jnp.dot(p.astype(BF16), v, preferred_element_type=F32)
            m_sc[h] = m_new

    @pl.when(j == pl.num_programs(1) - 1)
    def _():
        for h in range(NSA_KV_HEADS):
            o_ref[h] = acc_sc[h] / l_sc[h]


def _nsa_dec_sel(q, k_new, v_new, rb, k_pool, v_pool, layer, pages, halves, blocks, pos):
    b = q.shape[0]
    k_top = blocks.shape[-1]
    pages, halves, blocks = (a.reshape(b, NSA_KV_HEADS * k_top) for a in (pages, halves, blocks))

    def blkspec(h):
        return pl.BlockSpec((None, None, SEL_BLOCK, NSA_KV_HEADS, HEAD_DIM),
                            lambda bi, j, pg, hf, bl: (layer, pg[bi, h * k_top + j], hf[bi, h * k_top + j], 0, 0))

    qspec = pl.BlockSpec((None,) + q.shape[1:], lambda bi, j, pg, hf, bl: (bi, 0, 0, 0))
    nspec = pl.BlockSpec((None, NSA_KV_HEADS, HEAD_DIM), lambda bi, j, pg, hf, bl: (bi, 0, 0))
    grid_spec = pltpu.PrefetchScalarGridSpec(
        num_scalar_prefetch=3,
        grid=(b, k_top),
        in_specs=[qspec, nspec, nspec, pl.BlockSpec(rb.shape, lambda bi, j, pg, hf, bl: (0, 0, 0)),
                  blkspec(0), blkspec(1), blkspec(0), blkspec(1)],
        out_specs=qspec,
        scratch_shapes=[pltpu.VMEM((NSA_KV_HEADS, NSA_GROUP, 1), F32), pltpu.VMEM((NSA_KV_HEADS, NSA_GROUP, 1), F32),
                        pltpu.VMEM((NSA_KV_HEADS, NSA_GROUP, HEAD_DIM), F32)],
    )
    return pl.pallas_call(
        functools.partial(_nsa_dec_sel_kernel, pos=pos),
        name="nsa_dec_sel",
        grid_spec=grid_spec,
        out_shape=jax.ShapeDtypeStruct(q.shape, F32),
        compiler_params=_cp("parallel", "arbitrary"),
    )(pages, halves, blocks, q, k_new, v_new, rb, k_pool, k_pool, v_pool, v_pool)


def _nsa_dec_win_kernel(q_ref, g_ref, win_ref, new_ref, bias_ref, oc_ref, os_ref, y_ref, nw_ref):
    scale = HEAD_DIM ** -0.5
    nwin = win_ref.shape[0]
    sig = jax.nn.sigmoid(g_ref[...])
    for h in range(NSA_KV_HEADS):
        q = q_ref[h].astype(BF16)
        k = win_ref[:, 0, h, :].astype(BF16)
        v = win_ref[:, 1, h, :].astype(BF16)
        kn = new_ref[0, h:h + 1, :].astype(BF16)
        vn = new_ref[1, h:h + 1, :].astype(BF16)
        bias = bias_ref[h * NSA_GROUP:(h + 1) * NSA_GROUP]
        s = _nt(q, k) * scale + bias[:, :nwin]
        s_new = jnp.sum(q.astype(F32) * kn.astype(F32), axis=-1, keepdims=True) * scale + bias[:, nwin:nwin + 1]
        m = jnp.maximum(jnp.max(s, axis=-1, keepdims=True), s_new)
        p = jnp.exp(s - m)
        p_new = jnp.exp(s_new - m)
        den = jnp.sum(p, axis=-1, keepdims=True) + p_new
        ow = (jnp.dot(p.astype(BF16), v, preferred_element_type=F32)
              + p_new.astype(BF16).astype(F32) * vn.astype(F32)) / den
        gs = sig[h]
        y_ref[h] = (gs[:, 0:1] * oc_ref[h] + gs[:, 1:2] * os_ref[h] + gs[:, 2:3] * ow).astype(y_ref.dtype)
    nw_ref[pl.ds(0, nwin - 1)] = win_ref[pl.ds(1, nwin - 1)]
    nw_ref[nwin - 1] = new_ref[...]


def _nsa_dec_win(q, gates, win, new_kv, bias_w, o_c, o_s):
    b = q.shape[0]
    qspec = pl.BlockSpec((None,) + q.shape[1:], lambda i: (i, 0, 0, 0))
    wspec = pl.BlockSpec((None,) + win.shape[1:], lambda i: (i, 0, 0, 0, 0))
    bias2 = bias_w[:, 0, :]
    return pl.pallas_call(
        _nsa_dec_win_kernel,
        name="nsa_dec_win",
        grid=(b,),
        in_specs=[qspec, pl.BlockSpec((None,) + gates.shape[1:], lambda i: (i, 0, 0, 0)), wspec,
                  pl.BlockSpec((None,) + new_kv.shape[1:], lambda i: (i, 0, 0, 0)),
                  pl.BlockSpec(bias2.shape, lambda i: (0, 0)), qspec, qspec],
        out_specs=[qspec, wspec],
        out_shape=[jax.ShapeDtypeStruct(q.shape, BF16), jax.ShapeDtypeStruct(win.shape, F32)],
        compiler_params=_cp("parallel"),
    )(q, gates, win, new_kv, bias2, o_c, o_s)


def _tail(x, xb_unused, y_mix, mem_kv, ffn_prev, p, l, bsz, t, alpha):
    d = x.shape[1]
    x1, x1b = _add_ln(x, y_mix, p['ln_g'][l, 0], p['ln_b'][l, 0], alpha)
    tm = 1024
    q = _mm(x1b, p['w_cq'], F32, tm, 512, "mm_cq", layer=l)
    o = _cross(q.reshape(bsz, t, -1), mem_kv, 512).reshape(bsz * t, -1)
    x2, x2b = _mm_add_ln(o, p['w_co'], l, x1, p['ln_g'][l, 1], p['ln_b'][l, 1], alpha)
    if t > 1:
        act, st1, st2 = _ffn_up(x2b, p['w_up'], l, p['ffn_conv'][l], t)
        ffn_new = jnp.concatenate([st1[:, 6:], st2[:, 6:]], axis=-1)
    else:
        act, ffn_new = _ffn_up_dec(x2b, p['w_up'], l, p['ffn_conv'][l], ffn_prev)
    dff = act.shape[1]
    f = _mm_acc(act, p['w_down'], l, F32, 1024, 512, dff // 2)
    x3, x3b = _add_ln(x2, f, p['ln_g'][l, 2], p['ln_b'][l, 2], alpha)
    return x3, x3b, ffn_new


def kernel(x_prompt, x_sample, mem_prompt, state_pool, cache_nsa_cmp_k, cache_nsa_cmp_v, cache_nsa_sel_k, cache_nsa_sel_v, state_nsa_win, state_sc, cache_sb_k, cache_sb_v, state_ffn, cache_mem, page_table, w_in_even, w_pool, pool_scale, w_cmp_pos, w_cmp_lin, rel_bias, w_out_even, w_in_odd, sc_conv, w_out_odd, w_cq, w_ckv, w_co, w_up, ffn_conv, w_down, ln_g, ln_b):
    bp, t, d = x_prompt.shape
    bs = x_sample.shape[0]
    depth = w_cq.shape[0]
    n_pages = page_table.shape[1]
    past = n_pages * PAGE_SIZE
    alpha = (2.0 * depth) ** 0.25
    kvw = NSA_KV_HEADS * HEAD_DIM
    assert x_sample.shape[1] == 1 and state_nsa_win.shape[2] == WINDOW

    we = w_in_even
    w_even = jnp.concatenate([we[:, :, 1024:4096], we[:, :, :1024], we[:, :, 4096:],
                              jnp.zeros(we.shape[:2] + (E_TOT - we.shape[2],), we.dtype)], axis=2).astype(BF16)
    p = {'w_cq': w_cq.astype(BF16), 'w_co': w_co.astype(BF16), 'w_up': w_up,
         'w_down': w_down.astype(BF16), 'ffn_conv': ffn_conv, 'ln_g': ln_g, 'ln_b': ln_b}
    w_odd = w_in_odd.astype(BF16)
    w_oe = w_out_even.astype(BF16)
    w_oo = w_out_odd.astype(BF16)
    w_kv = w_ckv.astype(BF16)
    w_pool_b = w_pool.astype(BF16)
    w_lin_b = w_cmp_lin.astype(BF16)
    w_pos = w_cmp_pos.reshape(w_cmp_pos.shape[0], 2, CMP_BLOCK, kvw)

    qb = 128
    kj = np.arange(qb)[:, None]
    qt = np.arange(qb)[None, :]
    dist_d = np.stack([qt - kj, qb + qt - kj]).reshape(2 * qb, qb).astype(np.int32)
    tabd = _bias_lookup(rel_bias, jnp.asarray(dist_d), 2 * qb).reshape(NSA_HEADS, 2, qb, qb)
    nchunk = t // CMP_STRIDE
    dist_c = (np.arange(t)[None, :] - (np.arange(nchunk)[:, None] * CMP_STRIDE + CMP_BLOCK - 1)).astype(np.int32)
    tabc = _bias_lookup_blocked(rel_bias, jnp.asarray(dist_c), qb)
    b31 = rel_bias[N_BUCKETS - 1]

    mem_b = mem_prompt.reshape(bp * mem_prompt.shape[1], d).astype(BF16)
    n_mem = mem_prompt.shape[1]

    x = x_prompt.reshape(bp * t, d)
    xb = x.astype(BF16)
    outs_p = {k: [] for k in ('pool', 'cmp_k', 'cmp_v', 'sel_k', 'sel_v', 'win', 'sc', 'sb_k', 'sb_v', 'ffn', 'mem')}
    for l in range(depth):
        e = l // 2
        memkv = _mm(mem_b, w_kv, F32, 1024, 512, "mm_memkv", layer=l)
        outs_p['mem'].append(memkv.reshape(bp, n_mem, 2, MEM_HEADS, MEM_HEAD_DIM))
        if l % 2 == 0:
            h = _mm(xb, w_even[e], F32, 1024, 512, "mm_in_even").reshape(bp, t, E_TOT)
            y_pool = _pool_prompt(h, E_U // 1024, w_pool_b[e], pool_scale[e])
            kcmp, vcmp = _compress_prompt(h, w_pos[e], w_lin_b[e])
            y_nsa = _nsa_prompt(h, kcmp, vcmp, tabc, tabd, b31)
            mix = jnp.concatenate([y_pool, y_nsa], axis=-1).reshape(bp * t, d)
            y = _mm(mix, w_oe[e], F32, 1024, 512, "mm_out")
            outs_p['pool'].append(h[:, t - POOL_STATE:, E_U:E_U + 1024])
            for name, off in (('cmp_k', E_KC), ('cmp_v', E_VC), ('sel_k', E_KS), ('sel_v', E_VS)):
                outs_p[name].append(h[:, :, off:off + kvw].reshape(bp, t, NSA_KV_HEADS, HEAD_DIM))
            nw = min(WINDOW, t)
            outs_p['win'].append(h[:, t - nw:, E_KVW:E_KVW + 2 * kvw].reshape(bp, nw, 2, NSA_KV_HEADS, HEAD_DIM))
        else:
            sbw = O_K - O_Q
            hc = _mm(xb, w_odd[e], F32, 1024, 512, "mm_in_odd", O_X, O_Q).reshape(bp, t, O_Q)
            hq = _mm(xb, w_odd[e], BF16, 1024, 512, "mm_in_odd", O_Q, sbw).reshape(bp, t, sbw)
            hk = _mm(xb, w_odd[e], F32, 1024, 512, "mm_in_odd", O_K, sbw).reshape(bp, t, sbw)
            hv = _mm(xb, w_odd[e], F32, 1024, 512, "mm_in_odd", O_V, sbw).reshape(bp, t, sbw)
            y_sc, sc_st = _sconv_prompt(hc, sc_conv[e])
            y_sb = _sb_prompt(hq, hk, hv)
            mix = jnp.concatenate([y_sc, y_sb], axis=-1).reshape(bp * t, d)
            y = _mm(mix, w_oo[e], F32, 1024, 512, "mm_out")
            outs_p['sc'].append(sc_st[:, 6:])
            nh = sbw // HEAD_DIM
            outs_p['sb_k'].append(hk.reshape(bp, t, nh, HEAD_DIM))
            outs_p['sb_v'].append(hv.reshape(bp, t, nh, HEAD_DIM))
        x, xb, ffn_new = _tail(x, xb, y, memkv.reshape(bp, n_mem, -1), None, p, l, bp, t, alpha)
        outs_p['ffn'].append(ffn_new)
    y_prompt = x.reshape(bp, t, d)

    pos = past
    length = past + 1
    n_cmp = (length - CMP_BLOCK) // CMP_STRIDE + 1
    n_slc = -(-length // SEL_BLOCK)
    k_top = min(N_SEL, n_slc)
    ncrow = n_pages * (PAGE_SIZE // CMP_STRIDE)
    dist_cd = (pos - (np.arange(ncrow) * CMP_STRIDE + CMP_BLOCK - 1)).astype(np.int32)
    bias_cd = _bias_lookup(rel_bias, jnp.asarray(np.broadcast_to(dist_cd, (8, ncrow))), 8)[:, 0, :]
    dist_wd = np.maximum(WINDOW - np.arange(WINDOW + LANES), 0).astype(np.int32)
    bias_wd = _bias_lookup(rel_bias, jnp.asarray(np.broadcast_to(dist_wd, (8, WINDOW + LANES))), 8)
    rb_hg = rel_bias.T.reshape(NSA_KV_HEADS, NSA_GROUP, N_BUCKETS)
    mem_s = cache_mem.reshape(depth, bs, cache_mem.shape[2], -1)

    x = x_sample.reshape(bs, d)
    xb = x.astype(BF16)
    outs_s = {k: [] for k in ('pool', 'cmp_k', 'cmp_v', 'sel_k', 'sel_v', 'win', 'sc', 'sb_k', 'sb_v', 'ffn')}
    for l in range(depth):
        e = l // 2
        if l % 2 == 0:
            h = _mm(xb, w_even[e], F32, 8, 512)
            y_pool, pool_new = _pool_dec(state_pool[e], h[:, E_U:E_U + 1024], w_pool_b[e], pool_scale[e], pos)
            q4 = h[:, E_Q:E_Q + NSA_HEADS * HEAD_DIM].reshape(bs, NSA_KV_HEADS, NSA_GROUP, HEAD_DIM)
            gates = h[:, E_G:E_G + NSA_HEADS * 3].reshape(bs, NSA_KV_HEADS, NSA_GROUP, 3)
            ak, bk, av, bv = _cmp_dec(cache_nsa_cmp_k, cache_nsa_cmp_v, e, page_table, w_pos[e])
            o_c, idx = _nsa_dec_cmp(q4, ak, bk, av, bv, w_lin_b[e], bias_cd, pos, n_cmp, n_slc)
            blocks = idx[:, :, 0, :k_top]
            past_blk = jnp.where((blocks >= 0) & (blocks * SEL_BLOCK < past), blocks, -1)
            safe = jnp.maximum(past_blk, 0)
            per_page = PAGE_SIZE // SEL_BLOCK
            pages = jnp.take_along_axis(page_table[:, None, :], safe // per_page, axis=2)
            halves = safe % per_page
            ks_new = h[:, E_KS:E_KS + kvw].reshape(bs, NSA_KV_HEADS, HEAD_DIM)
            vs_new = h[:, E_VS:E_VS + kvw].reshape(bs, NSA_KV_HEADS, HEAD_DIM)
            o_s = _nsa_dec_sel(q4, ks_new, vs_new, rb_hg, cache_nsa_sel_k, cache_nsa_sel_v, e,
                               pages, halves, past_blk, pos)
            new_kv = h[:, E_KVW:E_KVW + 2 * kvw].reshape(bs, 2, NSA_KV_HEADS, HEAD_DIM)
            y_nsa, win_new = _nsa_dec_win(q4, gates, state_nsa_win[e], new_kv, bias_wd, o_c, o_s)
            mix = jnp.concatenate([y_pool, y_nsa.reshape(bs, -1)], axis=-1)
            y = _mm(mix, w_oe[e], F32, 8, 512)
            outs_s['pool'].append(pool_new)
            for name, off in (('cmp_k', E_KC), ('cmp_v', E_VC), ('sel_k', E_KS), ('sel_v', E_VS)):
                outs_s[name].append(h[:, off:off + kvw].reshape(bs, 1, NSA_KV_HEADS, HEAD_DIM))
            outs_s['win'].append(win_new)
        else:
            h = _mm(xb, w_odd[e], F32, 8, 512)
            y_sc, sc_new = _sconv_dec(state_sc[e], h, sc_conv[e])
            nh = (O_K - O_Q) // HEAD_DIM
            q3 = h[:, O_Q:O_K].reshape(bs, nh, HEAD_DIM)
            y_sb = _sb_dec(q3, cache_sb_k, cache_sb_v, e, page_table)
            mix = jnp.concatenate([y_sc, y_sb.reshape(bs, -1)], axis=-1)
            y = _mm(mix, w_oo[e], F32, 8, 512)
            outs_s['sc'].append(sc_new)
            outs_s['sb_k'].append(h[:, O_K:O_V].reshape(bs, 1, nh, HEAD_DIM))
            outs_s['sb_v'].append(h[:, O_V:].reshape(bs, 1, nh, HEAD_DIM))
        x, xb, ffn_new = _tail(x, xb, y, mem_s[l], state_ffn[l], p, l, bs, 1, alpha)
        outs_s['ffn'].append(ffn_new)
    y_sample = x.reshape(bs, 1, d)

    sp = {k: jnp.stack(v) for k, v in outs_p.items()}
    ss = {k: jnp.stack(v) for k, v in outs_s.items()}
    return (y_prompt, y_sample,
            sp['pool'], sp['cmp_k'], sp['cmp_v'], sp['sel_k'], sp['sel_v'], sp['win'],
            sp['sc'], sp['sb_k'], sp['sb_v'], sp['ffn'], sp['mem'],
            ss['pool'], ss['cmp_k'], ss['cmp_v'], ss['sel_k'], ss['sel_v'], ss['win'],
            ss['sc'], ss['sb_k'], ss['sb_v'], ss['ffn'])
```

```python
import functools
import math

import jax
import jax.numpy as jnp
import numpy as np
from jax import lax
from jax.experimental import pallas as pl
from jax.experimental.pallas import tpu as pltpu

F32 = jnp.float32
BF16 = jnp.bfloat16

HEAD_DIM = 128
PAGE_SIZE = 128
POOL_WINDOWS = (2, 4, 8, 16)
POOL_GROUP_DIM = 256
POOL_STATE = 15
NSA_KV_HEADS = 2
NSA_GROUP = 12
NSA_HEADS = 24
CMP_BLOCK = 32
CMP_STRIDE = 16
SEL_BLOCK = 64
SEL_SHIFT = 6
N_SEL = 16
WINDOW = 512
N_BUCKETS = 32
MAX_DISTANCE = 128
MEM_HEADS = 4
MEM_HEAD_DIM = 256
LN_EPS = 1e-5
NEG_INF = -1e30
BIG = 1e30

LANES = 128
VMEM_LIMIT = 56 * 1024 * 1024

E_U, E_Q, E_KC, E_VC, E_KS, E_VS, E_KVW, E_G, E_TOT = 0, 1024, 4096, 4352, 4608, 4864, 5120, 5632, 6144
E_R = E_KC
O_X, O_B, O_C, O_Q, O_K, O_V = 0, 1024, 2048, 3072, 6144, 9216


def _cp(*sem):
    return pltpu.CompilerParams(dimension_semantics=sem, vmem_limit_bytes=VMEM_LIMIT)


def _nt(a, b):
    return lax.dot_general(a, b, (((1,), (1,)), ((), ())), preferred_element_type=F32)


def _split3(x):
    hi = x.astype(BF16)
    r1 = x - hi.astype(F32)
    mid = r1.astype(BF16)
    lo = (r1 - mid.astype(F32)).astype(BF16)
    return hi, mid, lo


def _mm_kernel(x_ref, w_ref, o_ref):
    o_ref[...] = jnp.dot(x_ref[...], w_ref[...], preferred_element_type=F32).astype(o_ref.dtype)


def _mm(x, w, out_dtype, tm, tn, name="mm", col0=0, ncols=None, layer=None):
    m, k = x.shape
    n = w.shape[-1] - col0 if ncols is None else ncols
    tm = min(tm, m)
    c0 = col0 // tn
    assert col0 % tn == 0 and n % tn == 0
    if layer is None:
        wspec = pl.BlockSpec((k, tn), lambda i, j: (0, j + c0))
    else:
        wspec = pl.BlockSpec((None, k, tn), lambda i, j: (layer, 0, j + c0))
    return pl.pallas_call(
        _mm_kernel,
        name=name,
        grid=(m // tm, n // tn),
        in_specs=[pl.BlockSpec((tm, k), lambda i, j: (i, 0)), wspec],
        out_specs=pl.BlockSpec((tm, tn), lambda i, j: (i, j)),
        out_shape=jax.ShapeDtypeStruct((m, n), out_dtype),
        compiler_params=_cp("parallel", "parallel"),
    )(x, w)


def _mm_acc_kernel(x_ref, w_ref, o_ref, acc_ref):
    k = pl.program_id(2)

    @pl.when(k == 0)
    def _():
        acc_ref[...] = jnp.zeros_like(acc_ref)

    acc_ref[...] += jnp.dot(x_ref[...], w_ref[...], preferred_element_type=F32)

    @pl.when(k == pl.num_programs(2) - 1)
    def _():
        o_ref[...] = acc_ref[...].astype(o_ref.dtype)


def _mm_acc(x, w, layer, out_dtype, tm, tn, tk):
    m, k = x.shape
    n = w.shape[2]
    tm = min(tm, m)
    return pl.pallas_call(
        _mm_acc_kernel,
        name="mm_down",
        grid=(m // tm, n // tn, k // tk),
        in_specs=[pl.BlockSpec((tm, tk), lambda i, j, kk: (i, kk)),
                  pl.BlockSpec((None, tk, tn), lambda i, j, kk: (layer, kk, j))],
        out_specs=pl.BlockSpec((tm, tn), lambda i, j, kk: (i, j)),
        out_shape=jax.ShapeDtypeStruct((m, n), out_dtype),
        scratch_shapes=[pltpu.VMEM((tm, tn), F32)],
        compiler_params=_cp("parallel", "parallel", "arbitrary"),
    )(x, w)


def _ln_kernel(x_ref, y_ref, g_ref, b_ref, of_ref, ob_ref, *, alpha):
    z = alpha * x_ref[...] + y_ref[...]
    mu = jnp.mean(z, axis=-1, keepdims=True)
    zc = z - mu
    var = jnp.mean(zc * zc, axis=-1, keepdims=True)
    out = zc * lax.rsqrt(var + LN_EPS) * g_ref[...] + b_ref[...]
    of_ref[...] = out
    ob_ref[...] = out.astype(BF16)


def _add_ln(x, y, g, b, alpha, tm=256):
    m, d = x.shape
    tm = min(tm, m)
    row = pl.BlockSpec((tm, d), lambda i: (i, 0))
    vec = pl.BlockSpec((1, d), lambda i: (0, 0))
    return pl.pallas_call(
        functools.partial(_ln_kernel, alpha=alpha),
        name="add_ln",
        grid=(m // tm,),
        in_specs=[row, row, vec, vec],
        out_specs=[row, row],
        out_shape=[jax.ShapeDtypeStruct((m, d), F32), jax.ShapeDtypeStruct((m, d), BF16)],
        compiler_params=_cp("parallel"),
    )(x, y, g.reshape(1, d), b.reshape(1, d))


def _mm_ln_kernel(x_ref, w_ref, r_ref, g_ref, b_ref, of_ref, ob_ref, *, alpha):
    y = jnp.dot(x_ref[...], w_ref[...], preferred_element_type=F32)
    z = alpha * r_ref[...] + y
    mu = jnp.mean(z, axis=-1, keepdims=True)
    zc = z - mu
    var = jnp.mean(zc * zc, axis=-1, keepdims=True)
    out = zc * lax.rsqrt(var + LN_EPS) * g_ref[...] + b_ref[...]
    of_ref[...] = out
    ob_ref[...] = out.astype(BF16)


def _mm_add_ln(x, w, layer, res, g, b, alpha, tm=256):
    m, k = x.shape
    d = w.shape[2]
    tm = min(tm, m)
    row = pl.BlockSpec((tm, d), lambda i: (i, 0))
    vec = pl.BlockSpec((1, d), lambda i: (0, 0))
    return pl.pallas_call(
        functools.partial(_mm_ln_kernel, alpha=alpha),
        name="mm_add_ln",
        grid=(m // tm,),
        in_specs=[pl.BlockSpec((tm, k), lambda i: (i, 0)),
                  pl.BlockSpec((None, k, d), lambda i: (layer, 0, 0)), row, vec, vec],
        out_specs=[row, row],
        out_shape=[jax.ShapeDtypeStruct((m, d), F32), jax.ShapeDtypeStruct((m, d), BF16)],
        compiler_params=_cp("parallel"),
    )(x, w, res, g.reshape(1, d), b.reshape(1, d))


def _cross_kernel(q_ref, kv_ref, o_ref):
    width = MEM_HEADS * MEM_HEAD_DIM
    scale = MEM_HEAD_DIM ** -0.5
    for h in range(MEM_HEADS):
        sl = slice(h * MEM_HEAD_DIM, (h + 1) * MEM_HEAD_DIM)
        q = q_ref[:, sl].astype(BF16)
        k = kv_ref[:, sl].astype(BF16)
        v = kv_ref[:, width + h * MEM_HEAD_DIM: width + (h + 1) * MEM_HEAD_DIM].astype(BF16)
        s = _nt(q, k) * scale
        s = s - jnp.max(s, axis=-1, keepdims=True)
        e = jnp.exp(s)
        p = e / jnp.sum(e, axis=-1, keepdims=True)
        o_ref[:, sl] = jnp.dot(p.astype(BF16), v, preferred_element_type=F32).astype(o_ref.dtype)


def _cross(q, kv, tq):
    b, t, w = q.shape
    tq = min(tq, t)
    return pl.pallas_call(
        _cross_kernel,
        name="cross_attn",
        grid=(b, t // tq),
        in_specs=[pl.BlockSpec((None, tq, w), lambda i, j: (i, j, 0)),
                  pl.BlockSpec((None, kv.shape[1], kv.shape[2]), lambda i, j: (i, 0, 0))],
        out_specs=pl.BlockSpec((None, tq, w), lambda i, j: (i, j, 0)),
        out_shape=jax.ShapeDtypeStruct((b, t, w), BF16),
        compiler_params=_cp("parallel", "parallel"),
    )(q, kv)


def _shift_rows(h, c_last2, row):
    r1 = pltpu.roll(h, 1, 0)
    r2 = pltpu.roll(h, 2, 0)
    row8 = row[0:8]
    f1 = jnp.where(row8 == 0, c_last2[7:8], r1[0:8])
    f2 = jnp.where(row8 == 0, c_last2[6:7], jnp.where(row8 == 1, c_last2[7:8], r2[0:8]))
    return jnp.concatenate([f1, r1[8:]], axis=0), jnp.concatenate([f2, r2[8:]], axis=0)


def _ffn_up_kernel(x_ref, w1_ref, w2_ref, cw1_ref, cw2_ref, act_ref, st1_ref, st2_ref, *, sub):
    t = x_ref.shape[0]
    tn = w1_ref.shape[1]
    ns = t // sub
    w1 = w1_ref[...].astype(BF16)
    w2 = w2_ref[...].astype(BF16)
    cw1 = cw1_ref[...]
    cw2 = cw2_ref[...]
    row = lax.broadcasted_iota(jnp.int32, (sub, 1), 0)
    zeros8 = jnp.zeros((8, tn), F32)
    h = [None] * ns

    def dots(s):
        xs = x_ref[s * sub:(s + 1) * sub, :]
        return jnp.dot(xs, w1, preferred_element_type=F32), jnp.dot(xs, w2, preferred_element_type=F32)

    def conv(hc, prev8, cw):
        s1, s2 = _shift_rows(hc, prev8, row)
        return s2 * cw[0:1] + s1 * cw[1:2] + hc * cw[2:3]

    def epilogue(s):
        p1, p2 = (zeros8, zeros8) if s == 0 else (h[s - 1][0][sub - 8:], h[s - 1][1][sub - 8:])
        c1 = conv(h[s][0], p1, cw1)
        c2 = conv(h[s][1], p2, cw2)
        act_ref[s * sub:(s + 1) * sub, :] = (c1 * jax.nn.sigmoid(c1) * c2).astype(BF16)

    h[0] = dots(0)
    for s in range(1, ns):
        h[s] = dots(s)
        epilogue(s - 1)
    epilogue(ns - 1)
    st1_ref[...] = h[ns - 1][0][sub - 8:]
    st2_ref[...] = h[ns - 1][1][sub - 8:]


def _ffn_up(x, w_up, layer, conv_w, t, sub=512, tn=256):
    m, d = x.shape
    dff = w_up.shape[2] // 2
    nt = dff // tn
    sub = min(sub, t)
    bsz = m // t
    return pl.pallas_call(
        functools.partial(_ffn_up_kernel, sub=sub),
        grid=(bsz, nt),
        in_specs=[pl.BlockSpec((t, d), lambda i, j: (i, 0), pipeline_mode=pl.Buffered(1)),
                  pl.BlockSpec((None, d, tn), lambda i, j: (layer, 0, j)),
                  pl.BlockSpec((None, d, tn), lambda i, j: (layer, 0, j + nt)),
                  pl.BlockSpec((3, tn), lambda i, j: (0, j)),
                  pl.BlockSpec((3, tn), lambda i, j: (0, j + nt))],
        out_specs=[pl.BlockSpec((t, tn), lambda i, j: (i, j)),
                   pl.BlockSpec((None, 8, tn), lambda i, j: (i, 0, j)),
                   pl.BlockSpec((None, 8, tn), lambda i, j: (i, 0, j))],
        out_shape=[jax.ShapeDtypeStruct((m, dff), BF16),
                   jax.ShapeDtypeStruct((bsz, 8, dff), F32),
                   jax.ShapeDtypeStruct((bsz, 8, dff), F32)],
        compiler_params=_cp("parallel", "parallel"),
        name="ffn_up",
    )(x, w_up, w_up, conv_w, conv_w)


def _ffn_up_dec_kernel(x_ref, w1_ref, w2_ref, cw1_ref, cw2_ref, p1_ref, p2_ref, act_ref, n1_ref, n2_ref):
    x = x_ref[...]
    h1 = jnp.dot(x, w1_ref[...].astype(BF16), preferred_element_type=F32)
    h2 = jnp.dot(x, w2_ref[...].astype(BF16), preferred_element_type=F32)

    def conv(h, p_ref, cw):
        return p_ref[:, 0, :] * cw[0:1] + p_ref[:, 1, :] * cw[1:2] + h * cw[2:3]

    c1 = conv(h1, p1_ref, cw1_ref[...])
    c2 = conv(h2, p2_ref, cw2_ref[...])
    act_ref[...] = (c1 * jax.nn.sigmoid(c1) * c2).astype(BF16)
    n1_ref[:, 0, :] = p1_ref[:, 1, :]
    n1_ref[:, 1, :] = h1
    n2_ref[:, 0, :] = p2_ref[:, 1, :]
    n2_ref[:, 1, :] = h2


def _ffn_up_dec(x, w_up, layer, conv_w, prev, tn=256):
    b, d = x.shape
    dff = w_up.shape[2] // 2
    nt = dff // tn
    pspec1 = pl.BlockSpec((b, 2, tn), lambda j: (0, 0, j))
    pspec2 = pl.BlockSpec((b, 2, tn), lambda j: (0, 0, j + nt))
    act, n1, n2 = pl.pallas_call(
        _ffn_up_dec_kernel,
        name="ffn_up_dec",
        grid=(nt,),
        in_specs=[pl.BlockSpec((b, d), lambda j: (0, 0)),
                  pl.BlockSpec((None, d, tn), lambda j: (layer, 0, j)),
                  pl.BlockSpec((None, d, tn), lambda j: (layer, 0, j + nt)),
                  pl.BlockSpec((3, tn), lambda j: (0, j)),
                  pl.BlockSpec((3, tn), lambda j: (0, j + nt)),
                  pspec1, pspec2],
        out_specs=[pl.BlockSpec((b, tn), lambda j: (0, j)),
                   pl.BlockSpec((b, 2, tn), lambda j: (0, 0, j)),
                   pl.BlockSpec((b, 2, tn), lambda j: (0, 0, j))],
        out_shape=[jax.ShapeDtypeStruct((b, dff), BF16),
                   jax.ShapeDtypeStruct((b, 2, dff), F32),
                   jax.ShapeDtypeStruct((b, 2, dff), F32)],
        compiler_params=_cp("parallel"),
    )(x, w_up, w_up, conv_w, conv_w, prev, prev)
    return act, jnp.concatenate([n1, n2], axis=-1)


def _bias_kernel(rb_ref, dist_ref, o_ref):
    head = pl.program_id(0)
    n = jnp.maximum(dist_ref[...], 0)
    max_exact = N_BUCKETS // 2
    nf = jnp.maximum(n, 1).astype(F32)
    large = max_exact + (jnp.log(nf / max_exact) / math.log(MAX_DISTANCE / max_exact)
                         * (N_BUCKETS - max_exact)).astype(jnp.int32)
    bucket = jnp.where(n < max_exact, n, jnp.minimum(large, N_BUCKETS - 1))
    out = jnp.zeros(n.shape, F32)
    for k in range(N_BUCKETS):
        out = jnp.where(bucket == k, rb_ref[k, head], out)
    o_ref[...] = out


def _bias_lookup(rel_bias, dist, tr):
    r, c = dist.shape
    nh = rel_bias.shape[1]
    tr = min(tr, r)
    return pl.pallas_call(
        _bias_kernel,
        grid=(nh, r // tr),
        in_specs=[pl.BlockSpec(memory_space=pltpu.SMEM),
                  pl.BlockSpec((tr, c), lambda h, i: (i, 0))],
        out_specs=pl.BlockSpec((None, tr, c), lambda h, i: (h, i, 0)),
        out_shape=jax.ShapeDtypeStruct((nh, r, c), F32),
        compiler_params=_cp("parallel", "parallel"),
        name="bias_lookup",
    )(rel_bias, dist)


def _bias_blocked_kernel(rb_ref, dist_ref, o_ref):
    n = jnp.maximum(dist_ref[...], 0)
    max_exact = N_BUCKETS // 2
    nf = jnp.maximum(n, 1).astype(F32)
    large = max_exact + (jnp.log(nf / max_exact) / math.log(MAX_DISTANCE / max_exact)
                         * (N_BUCKETS - max_exact)).astype(jnp.int32)
    bucket = jnp.where(n < max_exact, n, jnp.minimum(large, N_BUCKETS - 1))
    hits = [bucket == k for k in range(N_BUCKETS)]

    def head(hd, carry):
        out = jnp.zeros(n.shape, F32)
        for k in range(N_BUCKETS):
            out = jnp.where(hits[k], rb_ref[k, hd], out)
        o_ref[hd] = out
        return carry

    lax.fori_loop(0, o_ref.shape[0], head, 0)


def _bias_lookup_blocked(rel_bias, dist, tc):
    r, c = dist.shape
    nh = rel_bias.shape[1]
    return pl.pallas_call(
        _bias_blocked_kernel,
        grid=(c // tc,),
        in_specs=[pl.BlockSpec(memory_space=pltpu.SMEM),
                  pl.BlockSpec((r, tc), lambda j: (0, j))],
        out_specs=pl.BlockSpec((None, nh, r, tc), lambda j: (j, 0, 0, 0)),
        out_shape=jax.ShapeDtypeStruct((c // tc, nh, r, tc), F32),
        compiler_params=_cp("parallel"),
        name="bias_lookup_blocked",
    )(rel_bias, dist)


def _pool_groups(sums, cur, pos, wp_ref, sc_ref, y_ref):
    for g, w in enumerate(POOL_WINDOWS):
        sl = slice(g * POOL_GROUP_DIM, (g + 1) * POOL_GROUP_DIM)
        cnt = jnp.minimum(w, pos + 1).astype(F32)
        d = sums[g] / cnt - cur[:, sl]
        y = jnp.dot(d.astype(BF16), wp_ref[g], preferred_element_type=F32) * sc_ref[:, sl]
        y_ref[:, sl] = y.astype(y_ref.dtype)


def _pool_kernel(prev_ref, cur_ref, wp_ref, sc_ref, y_ref):
    t = pl.program_id(1)
    tt = cur_ref.shape[0]
    cur = cur_ref[...]
    prev = jnp.where(t == 0, 0.0, prev_ref[...])
    ext = jnp.concatenate([prev, cur], axis=0)
    gd = POOL_GROUP_DIM
    s2 = ext + pltpu.roll(ext, 1, 0)
    x4 = s2[:, gd:]
    s4 = x4 + pltpu.roll(x4, 2, 0)
    x8 = s4[:, gd:]
    s8 = x8 + pltpu.roll(x8, 4, 0)
    x16 = s8[:, gd:]
    s16 = x16 + pltpu.roll(x16, 8, 0)
    sums = [s2[16:, :gd], s4[16:, :gd], s8[16:, :gd], s16[16:]]
    pos = t * tt + lax.broadcasted_iota(jnp.int32, (tt, 1), 0)
    _pool_groups(sums, cur, pos, wp_ref, sc_ref, y_ref)


def _pool_prompt(h, col_blk, w_pool, pool_scale, tt=256):
    b, t, _ = h.shape
    wdt = w_pool.shape[0] * POOL_GROUP_DIM
    tt = min(tt, t)
    r = tt // 16
    return pl.pallas_call(
        _pool_kernel,
        name="pool_prompt",
        grid=(b, t // tt),
        in_specs=[pl.BlockSpec((None, 16, wdt), lambda i, j: (i, jnp.maximum(j * r - 1, 0), col_blk)),
                  pl.BlockSpec((None, tt, wdt), lambda i, j: (i, j, col_blk)),
                  pl.BlockSpec(w_pool.shape, lambda i, j: (0, 0, 0)),
                  pl.BlockSpec((1, wdt), lambda i, j: (0, 0))],
        out_specs=pl.BlockSpec((None, tt, wdt), lambda i, j: (i, j, 0)),
        out_shape=jax.ShapeDtypeStruct((b, t, wdt), BF16),
        compiler_params=_cp("parallel", "parallel"),
    )(h, h, w_pool, pool_scale.reshape(1, wdt))


def _pool_dec_kernel(sp_ref, u_ref, wp_ref, sc_ref, y_ref, ns_ref, *, pos):
    u = u_ref[...]
    gd = POOL_GROUP_DIM
    acc = u
    sums = []
    back = 1
    for g, w in enumerate(POOL_WINDOWS):
        while back < w:
            acc = acc + sp_ref[:, POOL_STATE - back, :]
            back += 1
        sums.append(acc[:, g * gd:(g + 1) * gd])
    posv = jnp.full((u.shape[0], 1), pos, jnp.int32)
    _pool_groups(sums, u, posv, wp_ref, sc_ref, y_ref)
    for j in range(POOL_STATE - 1):
        ns_ref[:, j, :] = sp_ref[:, j + 1, :]
    ns_ref[:, POOL_STATE - 1, :] = u


def _pool_dec(state, u, w_pool, pool_scale, pos):
    b, wdt = u.shape
    return pl.pallas_call(
        functools.partial(_pool_dec_kernel, pos=pos),
        out_shape=[jax.ShapeDtypeStruct((b, wdt), BF16), jax.ShapeDtypeStruct(state.shape, F32)],
        compiler_params=pltpu.CompilerParams(vmem_limit_bytes=VMEM_LIMIT),
    )(state, u, w_pool, pool_scale.reshape(1, wdt))


def _sconv_kernel(px_ref, pc_ref, x_ref, b_ref, c_ref, cw_ref, y_ref, st_ref):
    t = pl.program_id(1)
    tt = x_ref.shape[0]
    v = c_ref[...] * x_ref[...]
    pv = jnp.where(t == 0, 0.0, pc_ref[...] * px_ref[...])
    row = lax.broadcasted_iota(jnp.int32, (tt, 1), 0)
    s1, s2 = _shift_rows(v, pv, row)
    cw = cw_ref[...]
    conv = s2 * cw[0:1] + s1 * cw[1:2] + v * cw[2:3]
    y_ref[...] = (b_ref[...] * conv).astype(y_ref.dtype)
    st_ref[...] = v[tt - 8:]


def _sconv_prompt(h, conv_w, tt=256):
    b, t, _ = h.shape
    wdt = conv_w.shape[1]
    tt = min(tt, t)
    r = tt // 8

    def prev(blk):
        return pl.BlockSpec((None, 8, wdt), lambda i, j: (i, jnp.maximum(j * r - 1, 0), blk))

    def cur(blk):
        return pl.BlockSpec((None, tt, wdt), lambda i, j: (i, j, blk))

    return pl.pallas_call(
        _sconv_kernel,
        name="sconv_prompt",
        grid=(b, t // tt),
        in_specs=[prev(0), prev(2), cur(0), cur(1), cur(2), pl.BlockSpec((3, wdt), lambda i, j: (0, 0))],
        out_specs=[pl.BlockSpec((None, tt, wdt), lambda i, j: (i, j, 0)),
                   pl.BlockSpec((None, 8, wdt), lambda i, j: (i, 0, 0))],
        out_shape=[jax.ShapeDtypeStruct((b, t, wdt), BF16), jax.ShapeDtypeStruct((b, 8, wdt), F32)],
        compiler_params=_cp("parallel", "arbitrary"),
    )(h, h, h, h, h, conv_w)


def _sconv_dec_kernel(st_ref, x_ref, b_ref, c_ref, cw_ref, y_ref, ns_ref):
    v = c_ref[...] * x_ref[...]
    cw = cw_ref[...]
    conv = st_ref[:, 0, :] * cw[0:1] + st_ref[:, 1, :] * cw[1:2] + v * cw[2:3]
    y_ref[...] = (b_ref[...] * conv).astype(y_ref.dtype)
    ns_ref[:, 0, :] = st_ref[:, 1, :]
    ns_ref[:, 1, :] = v


def _sconv_dec(state, h, conv_w):
    b = h.shape[0]
    wdt = conv_w.shape[1]

    def col(blk):
        return pl.BlockSpec((b, wdt), lambda i: (0, blk))

    return pl.pallas_call(
        _sconv_dec_kernel,
        grid=(1,),
        in_specs=[pl.BlockSpec(state.shape, lambda i: (0, 0, 0)), col(0), col(1), col(2),
                  pl.BlockSpec((3, wdt), lambda i: (0, 0))],
        out_specs=[pl.BlockSpec((b, wdt), lambda i: (0, 0)), pl.BlockSpec(state.shape, lambda i: (0, 0, 0))],
        out_shape=[jax.ShapeDtypeStruct((b, wdt), BF16), jax.ShapeDtypeStruct(state.shape, F32)],
        compiler_params=_cp("arbitrary"),
    )(state, h, h, h, conv_w)


def _softplus(z):
    return jnp.maximum(z, 0.0) + jnp.log(1.0 + jnp.exp(-jnp.abs(z)))


def _suffix_sums(x, upper):
    r = x.shape[0]
    hi = x.astype(BF16)
    lo = (x - hi.astype(F32)).astype(BF16)
    s = jnp.dot(jnp.concatenate([hi, lo], axis=0), upper, preferred_element_type=F32)
    return s[:r] + s[r:]


SB_HEADS_PER_STEP = 4


def _sb_kernel(q_ref, k_ref, v_ref, o_ref, ab_sc, r_sc, acc_sc, *, tq):
    i = pl.program_id(2)
    scale = HEAD_DIM ** -0.5
    nhs = SB_HEADS_PER_STEP
    hsl = [slice(j * HEAD_DIM, (j + 1) * HEAD_DIM) for j in range(nhs)]
    qs = [q_ref[:, sl].astype(BF16) for sl in hsl]
    rj = lax.broadcasted_iota(jnp.int32, (tq, tq), 0)
    cj = lax.broadcasted_iota(jnp.int32, (tq, tq), 1)
    upper = jnp.where(rj > cj, 1.0, 0.0).astype(BF16)

    before = jnp.concatenate([cj < rj] * nhs, axis=0)

    def logits(c):
        off = pl.multiple_of(c * tq, tq)
        return jnp.concatenate([_nt(qs[j], k_ref[pl.ds(off, tq), hsl[j]].astype(BF16)) for j in range(nhs)],
                               axis=0) * scale

    def weights(z, r_run, diag):
        sp = _softplus(z)
        if diag:
            sp = jnp.where(before, sp, 0.0)
        a = jnp.exp(z - sp - _suffix_sums(sp, upper) - r_run)
        if diag:
            a = jnp.where(before, a, 0.0)
        return a.astype(BF16), r_run + jnp.sum(sp, axis=1, keepdims=True)

    def accumulate(c):
        off = pl.multiple_of(c * tq, tq)
        for j in range(nhs):
            acc_sc[j] += jnp.dot(ab_sc[j * tq:(j + 1) * tq, :], v_ref[pl.ds(off, tq), hsl[j]].astype(BF16),
                                 preferred_element_type=F32)

    acc_sc[...] = jnp.zeros_like(acc_sc)
    ab_sc[...], r_sc[...] = weights(logits(i), jnp.zeros((nhs * tq, 1), F32), True)

    def body(s, carry):
        c = i - s
        accumulate(c + 1)
        ab_sc[...], r_sc[...] = weights(logits(c), r_sc[...], False)
        return carry

    lax.fori_loop(1, i + 1, body, 0)
    accumulate(0)
    for j in range(nhs):
        o_ref[:, hsl[j]] = acc_sc[j].astype(o_ref.dtype)


def _sb_prompt(q, k, v, tq=256):
    b, t, width = q.shape
    tq = min(tq, t)
    wdt = SB_HEADS_PER_STEP * HEAD_DIM
    return pl.pallas_call(
        functools.partial(_sb_kernel, tq=tq),
        grid=(b, width // wdt, t // tq),
        in_specs=[pl.BlockSpec((None, tq, wdt), lambda bi, hi, i: (bi, i, hi)),
                  pl.BlockSpec((None, t, wdt), lambda bi, hi, i: (bi, 0, hi)),
                  pl.BlockSpec((None, t, wdt), lambda bi, hi, i: (bi, 0, hi))],
        out_specs=pl.BlockSpec((None, tq, wdt), lambda bi, hi, i: (bi, i, hi)),
        out_shape=jax.ShapeDtypeStruct((b, t, width), BF16),
        scratch_shapes=[pltpu.VMEM((SB_HEADS_PER_STEP * tq, tq), BF16), pltpu.VMEM((SB_HEADS_PER_STEP * tq, 1), F32),
                        pltpu.VMEM((SB_HEADS_PER_STEP, tq, HEAD_DIM), F32)],
        compiler_params=_cp("parallel", "parallel", "parallel"),
        name="sb_prompt",
    )(q, k, v)


SB_PAGES_PER_STEP = 2


def _sb_dec_kernel(pt_ref, q_ref, *refs):
    npg = SB_PAGES_PER_STEP
    k_refs, v_refs = refs[:npg], refs[npg:2 * npg]
    o_ref, z_sc, r_sc, acc_sc = refs[2 * npg:]
    c = pl.program_id(1)
    scale = HEAD_DIM ** -0.5
    nh = q_ref.shape[0]

    @pl.when(c == 0)
    def _():
        z_sc[...] = jnp.zeros_like(z_sc)
        r_sc[...] = jnp.zeros_like(r_sc)
        acc_sc[...] = jnp.zeros_like(acc_sc)

    rj = lax.broadcasted_iota(jnp.int32, (PAGE_SIZE, PAGE_SIZE), 0)
    cj = lax.broadcasted_iota(jnp.int32, (PAGE_SIZE, PAGE_SIZE), 1)
    upper = jnp.where(rj > cj, 1.0, 0.0).astype(BF16)
    hrow = lax.broadcasted_iota(jnp.int32, (nh, HEAD_DIM), 0)

    def heads_on_lanes(ref):
        return jnp.concatenate([ref[pl.ds(j, PAGE_SIZE, stride=nh), :].astype(BF16) for j in range(nh)], axis=1)

    live = c > 0
    r_run = r_sc[...]
    weights = []
    for r in range(npg):
        z = z_sc[r]
        sp = jnp.where(live, _softplus(z), 0.0)
        weights.append(jnp.where(live, jnp.exp(z - sp - _suffix_sums(sp, upper) - r_run), 0.0).astype(BF16))
        r_run = r_run + jnp.sum(sp, axis=1, keepdims=True)
    r_sc[...] = r_run

    q = q_ref[...]
    q_bd = jnp.concatenate([jnp.where(hrow == j, q, 0.0) for j in range(nh)], axis=1).astype(BF16)
    for r in range(npg):
        z_sc[r] = _nt(q_bd, heads_on_lanes(k_refs[r])) * scale

    upd = jnp.zeros((nh, HEAD_DIM), F32)
    for r in range(npg):
        res = jnp.dot(weights[r], heads_on_lanes(v_refs[r]), preferred_element_type=F32)
        for j in range(nh):
            upd = upd + jnp.where(hrow == j, res[:, j * HEAD_DIM:(j + 1) * HEAD_DIM], 0.0)
    acc_sc[...] += upd

    @pl.when(c == pl.num_programs(1) - 1)
    def _():
        o_ref[...] = acc_sc[...].astype(o_ref.dtype)


def _sb_dec(q, k_pool, v_pool, layer, page_table):
    b, nh, _ = q.shape
    n_pages = page_table.shape[1]
    npg = SB_PAGES_PER_STEP
    assert n_pages % npg == 0
    n_steps = n_pages // npg
    rows = PAGE_SIZE * nh
    k_pool = k_pool.reshape(k_pool.shape[:2] + (rows, HEAD_DIM))
    v_pool = v_pool.reshape(v_pool.shape[:2] + (rows, HEAD_DIM))

    def k_spec(r):
        return pl.BlockSpec((None, None, rows, HEAD_DIM), lambda bi, c, pt: (
            layer, pt[bi, n_pages - 1 - (npg * jnp.minimum(c, n_steps - 1) + r)], 0, 0))

    def v_spec(r):
        return pl.BlockSpec((None, None, rows, HEAD_DIM), lambda bi, c, pt: (
            layer, pt[bi, n_pages - 1 - (npg * jnp.maximum(c - 1, 0) + r)], 0, 0))

    grid_spec = pltpu.PrefetchScalarGridSpec(
        num_scalar_prefetch=1,
        grid=(b, n_steps + 1),
        in_specs=[pl.BlockSpec((None, nh, HEAD_DIM), lambda bi, c, pt: (bi, 0, 0))]
        + [k_spec(r) for r in range(npg)] + [v_spec(r) for r in range(npg)],
        out_specs=pl.BlockSpec((None, nh, HEAD_DIM), lambda bi, c, pt: (bi, 0, 0)),
        scratch_shapes=[pltpu.VMEM((npg, nh, PAGE_SIZE), F32), pltpu.VMEM((nh, 1), F32),
                        pltpu.VMEM((nh, HEAD_DIM), F32)],
    )
    return pl.pallas_call(
        _sb_dec_kernel,
        grid_spec=grid_spec,
        out_shape=jax.ShapeDtypeStruct((b, nh, HEAD_DIM), BF16),
        compiler_params=_cp("parallel", "arbitrary"),
        name="sb_decode",
    )(page_table, q, *([k_pool] * npg), *([v_pool] * npg))


def _chunk_sums(x, w0, w1):
    x3 = x.reshape(x.shape[0] // CMP_STRIDE, CMP_STRIDE, x.shape[1])
    return jnp.sum(x3 * w0[None], axis=1), jnp.sum(x3 * w1[None], axis=1)


def _compress_kernel(kc_ref, vc_ref, wpos_ref, wlin_ref, ko_ref, vo_ref):
    nchunk = kc_ref.shape[0] // CMP_STRIDE
    for idx, (src, dst) in enumerate(((kc_ref, ko_ref), (vc_ref, vo_ref))):
        a, bsum = _chunk_sums(src[...], wpos_ref[idx, 0:CMP_STRIDE, :], wpos_ref[idx, CMP_STRIDE:, :])
        pooled = a + pltpu.roll(bsum, nchunk - 1, 0)
        for h in range(NSA_KV_HEADS):
            sl = slice(h * HEAD_DIM, (h + 1) * HEAD_DIM)
            dst[:, sl] = jnp.dot(pooled[:, sl].astype(BF16), wlin_ref[idx, h], preferred_element_type=F32)


def _compress_prompt(h, wpos, wlin):
    b, t, _ = h.shape
    kvw = NSA_KV_HEADS * HEAD_DIM
    nchunk = t // CMP_STRIDE
    out = jax.ShapeDtypeStruct((b, nchunk, kvw), F32)
    ospec = pl.BlockSpec((None, nchunk, kvw), lambda i: (i, 0, 0))
    return pl.pallas_call(
        _compress_kernel,
        name="compress_prompt",
        grid=(b,),
        in_specs=[pl.BlockSpec((None, t, kvw), lambda i: (i, 0, (E_KC - E_R) // kvw)),
                  pl.BlockSpec((None, t, kvw), lambda i: (i, 0, (E_VC - E_R) // kvw)),
                  pl.BlockSpec(wpos.shape, lambda i: (0, 0, 0)),
                  pl.BlockSpec(wlin.shape, lambda i: (0, 0, 0, 0))],
        out_specs=[ospec, ospec],
        out_shape=[out, out],
        compiler_params=_cp("parallel"),
    )(h, h, wpos, wlin)


def _topk_keep(score, k_top):
    srow = lax.broadcasted_iota(jnp.int32, score.shape, 0)
    rank = jnp.zeros(score.shape, jnp.int32)
    for s2 in range(score.shape[0]):
        row = score[s2:s2 + 1, :]
        rank = rank + jnp.where(row > score, 1, jnp.where(row == score, jnp.where(srow > s2, 1, 0), 0))
    return jnp.where(rank < k_top, jnp.where(score > 0.5 * NEG_INF, 1.0, 0.0), 0.0)


def _flash_init(m_sc, l_sc, acc_sc):
    m_sc[...] = jnp.full(m_sc.shape, NEG_INF, F32)
    l_sc[...] = jnp.zeros(l_sc.shape, F32)
    acc_sc[...] = jnp.zeros(acc_sc.shape, F32)


def _flash_chunk(q_all, k, v, bias_fn, mask, m_sc, l_sc, acc_sc):
    ng = NSA_GROUP
    r = q_all.shape[0] // ng
    s_all = _nt(k.astype(BF16), q_all) * (HEAD_DIM ** -0.5)
    v_t = v.T.astype(BF16)
    m_prev = m_sc[...]
    m_parts, p_parts = [], []
    for g in range(ng):
        sl = slice(g * r, (g + 1) * r)
        s = s_all[:, sl] + bias_fn(g)
        if mask is not None:
            s = jnp.where(mask, s, NEG_INF)
        mn = jnp.maximum(m_prev[:, sl], jnp.max(s, axis=0, keepdims=True))
        p_parts.append(jnp.exp(s - mn))
        m_parts.append(mn)
    m_new = jnp.concatenate(m_parts, axis=1)
    p = jnp.concatenate(p_parts, axis=1)
    alpha = jnp.exp(m_prev - m_new)
    l_sc[...] = alpha * l_sc[...] + jnp.sum(p, axis=0, keepdims=True)
    acc_sc[...] = alpha * acc_sc[...] + jnp.dot(v_t, p.astype(BF16), preferred_element_type=F32)
    m_sc[...] = m_new


def _nsa_kernel(b31_ref, q_ref, g_ref, kc_ref, vc_ref, ks_ref, vs_ref, kvw_ref, tabc_ref, tabd_ref, o_ref,
                m_sc, l_sc, acc_sc, oc_sc, os_sc, keep_sc):
    i = pl.program_id(1)
    qb = q_ref.shape[0]
    t_len = ks_ref.shape[0]
    nc = kc_ref.shape[0]
    n_cmp = (t_len - CMP_BLOCK) // CMP_STRIDE + 1
    n_slc = t_len // SEL_BLOCK
    k_top = min(N_SEL, n_slc)
    gsz = NSA_GROUP
    scale = HEAD_DIM ** -0.5
    keyi = lax.broadcasted_iota(jnp.int32, (qb, qb), 0)
    qi = lax.broadcasted_iota(jnp.int32, (qb, qb), 1)
    causal = keyi <= qi
    wedge = keyi >= qi
    first_half = keyi < SEL_BLOCK
    pos_row = i * qb + lax.broadcasted_iota(jnp.int32, (1, qb), 1)
    sig_t = jax.nn.sigmoid(g_ref[...]).T
    osb = lax.broadcasted_iota(jnp.int32, (n_slc, nc), 0)
    on = lax.broadcasted_iota(jnp.int32, (n_slc, nc), 1)
    c_start = on * CMP_STRIDE
    overlap_t = jnp.where((c_start < osb * SEL_BLOCK + SEL_BLOCK) & (c_start + CMP_BLOCK - 1 >= osb * SEL_BLOCK)
                          & (on < n_cmp), 1.0, 0.0).astype(BF16)
    nrow = lax.broadcasted_iota(jnp.int32, (nc, qb), 0)
    cmask = ((nrow * CMP_STRIDE + CMP_BLOCK - 1) <= pos_row) & (nrow < n_cmp)
    srow = lax.broadcasted_iota(jnp.int32, (n_slc, qb), 0)
    cur = pos_row >> SEL_SHIFT
    forced = (srow == 0) | (srow == cur) | (srow == cur - 1)
    future = srow > cur

    for h in range(NSA_KV_HEADS):
        hs = slice(h * HEAD_DIM, (h + 1) * HEAD_DIM)
        g0 = h * gsz
        q_all = jnp.concatenate(
            [q_ref[:, (g0 + g) * HEAD_DIM:(g0 + g + 1) * HEAD_DIM] for g in range(gsz)], axis=0).astype(BF16)

        s_all = _nt(kc_ref[:, hs].astype(BF16), q_all) * scale
        vc_t = vc_ref[:, hs].T.astype(BF16)
        pcs = jnp.zeros((nc, qb), F32)
        pc_parts = []
        for g in range(gsz):
            s = s_all[:, g * qb:(g + 1) * qb] + tabc_ref[g0 + g]
            s = jnp.where(cmask, s, NEG_INF)
            s = s - jnp.max(s, axis=0, keepdims=True)
            e = jnp.exp(s)
            pc = jnp.where(cmask, e / jnp.sum(e, axis=0, keepdims=True), 0.0)
            pcs = pcs + pc
            pc_parts.append(pc.astype(BF16))
        oc_sc[...] = jnp.dot(vc_t, jnp.concatenate(pc_parts, axis=1), preferred_element_type=F32)
        imp = sum(jnp.dot(overlap_t, piece, preferred_element_type=F32) for piece in _split3(pcs))
        score = jnp.where(future, NEG_INF, jnp.where(forced, BIG, imp))
        keep_sc[...] = _topk_keep(score, k_top)

        def sel_mask(c, nk=1):
            parts = []
            for u in range(nk):
                first = keep_sc[pl.ds(2 * (c + u), 1), :]
                second = keep_sc[pl.ds(2 * (c + u) + 1, 1), :]
                parts.append(jnp.where(first_half, first, second))
            return jnp.concatenate(parts, axis=0) > 0.5

        def chunk(k_ref, v_ref, c, ksl, vsl, bias_fn, mask, nk=1):
            off = pl.multiple_of(c * qb, qb)
            _flash_chunk(q_all, k_ref[pl.ds(off, nk * qb), ksl], v_ref[pl.ds(off, nk * qb), vsl], bias_fn, mask,
                         m_sc, l_sc, acc_sc)

        tab0 = lambda g: tabd_ref[g0 + g, 0]
        tab1 = lambda g: tabd_ref[g0 + g, 1]
        far = lambda g: b31_ref[g0 + g]

        _flash_init(m_sc, l_sc, acc_sc)
        chunk(ks_ref, vs_ref, i, hs, hs, tab0, sel_mask(i) & causal)

        @pl.when(i >= 1)
        def _():
            chunk(ks_ref, vs_ref, i - 1, hs, hs, tab1, sel_mask(i - 1))

        n_far = jnp.maximum(i - 1, 0)

        def far_body(pair, carry):
            chunk(ks_ref, vs_ref, 2 * pair, hs, hs, far, sel_mask(2 * pair, 2), 2)
            return carry

        lax.fori_loop(0, n_far // 2, far_body, 0)

        @pl.when(n_far % 2 == 1)
        def _():
            chunk(ks_ref, vs_ref, n_far - 1, hs, hs, far, sel_mask(n_far - 1))

        os_sc[...] = acc_sc[...] / l_sc[...]

        kw_sl = hs
        vw_sl = slice(NSA_KV_HEADS * HEAD_DIM + h * HEAD_DIM, NSA_KV_HEADS * HEAD_DIM + (h + 1) * HEAD_DIM)
        _flash_init(m_sc, l_sc, acc_sc)
        chunk(kvw_ref, kvw_ref, i, kw_sl, vw_sl, tab0, causal)

        @pl.when(i >= 1)
        def _():
            chunk(kvw_ref, kvw_ref, i - 1, kw_sl, vw_sl, tab1, None)

        @pl.when(i >= 3)
        def _():
            chunk(kvw_ref, kvw_ref, i - 3, kw_sl, vw_sl, far, None, 2)

        @pl.when(i == 2)
        def _():
            chunk(kvw_ref, kvw_ref, 0, kw_sl, vw_sl, far, None)

        @pl.when(i >= 4)
        def _():
            chunk(kvw_ref, kvw_ref, i - 4, kw_sl, vw_sl, far, wedge)

        ow = acc_sc[...] / l_sc[...]

        def gate(j):
            return jnp.concatenate([sig_t[(g0 + g) * 3 + j:(g0 + g) * 3 + j + 1, :] for g in range(gsz)], axis=1)

        y_t = gate(0) * oc_sc[...] + gate(1) * os_sc[...] + gate(2) * ow
        for g in range(gsz):
            o_ref[:, (g0 + g) * HEAD_DIM:(g0 + g + 1) * HEAD_DIM] = y_t[:, g * qb:(g + 1) * qb].T.astype(o_ref.dtype)


def _nsa_prompt(hq, h, kcmp, vcmp, tabc, tabd, b31, qb=128):
    b, t, _ = h.shape
    assert qb == HEAD_DIM == 2 * SEL_BLOCK and WINDOW == 4 * qb and t % qb == 0
    qw = NSA_HEADS * HEAD_DIM
    kvw = NSA_KV_HEADS * HEAD_DIM
    nchunk = kcmp.shape[1]
    gq = NSA_GROUP * qb
    return pl.pallas_call(
        _nsa_kernel,
        grid=(b, t // qb),
        in_specs=[pl.BlockSpec(memory_space=pltpu.SMEM),
                  pl.BlockSpec((None, qb, qw), lambda bi, i: (bi, i, 0)),
                  pl.BlockSpec((None, qb, LANES), lambda bi, i: (bi, i, (E_G - E_R) // LANES)),
                  pl.BlockSpec((None, nchunk, kvw), lambda bi, i: (bi, 0, 0)),
                  pl.BlockSpec((None, nchunk, kvw), lambda bi, i: (bi, 0, 0)),
                  pl.BlockSpec((None, t, kvw), lambda bi, i: (bi, 0, (E_KS - E_R) // kvw)),
                  pl.BlockSpec((None, t, kvw), lambda bi, i: (bi, 0, (E_VS - E_R) // kvw)),
                  pl.BlockSpec((None, t, 2 * kvw), lambda bi, i: (bi, 0, (E_KVW - E_R) // (2 * kvw))),
                  pl.BlockSpec((None, NSA_HEADS, nchunk, qb), lambda bi, i: (i, 0, 0, 0)),
                  pl.BlockSpec((NSA_HEADS, 2, qb, qb), lambda bi, i: (0, 0, 0, 0))],
        out_specs=pl.BlockSpec((None, qb, qw), lambda bi, i: (bi, i, 0)),
        out_shape=jax.ShapeDtypeStruct((b, t, qw), BF16),
        scratch_shapes=[pltpu.VMEM((1, gq), F32), pltpu.VMEM((1, gq), F32),
                        pltpu.VMEM((HEAD_DIM, gq), F32), pltpu.VMEM((HEAD_DIM, gq), F32),
                        pltpu.VMEM((HEAD_DIM, gq), F32), pltpu.VMEM((t // SEL_BLOCK, qb), F32)],
        compiler_params=_cp("parallel", "parallel"),
        name="nsa_prompt",
    )(b31, hq, h, kcmp, vcmp, h, h, h, tabc, tabd)


def _cmp_dec_kernel(pt_ref, *refs, n_pg):
    k_refs = refs[:n_pg]
    v_refs = refs[n_pg:2 * n_pg]
    wpos_ref = refs[2 * n_pg]
    ak_ref, bk_ref, av_ref, bv_ref = refs[2 * n_pg + 1:]
    per = PAGE_SIZE // CMP_STRIDE
    for idx, (srcs, a_ref, b_ref) in enumerate(((k_refs, ak_ref, bk_ref), (v_refs, av_ref, bv_ref))):
        for r in range(n_pg):
            for h in range(NSA_KV_HEADS):
                sl = slice(h * HEAD_DIM, (h + 1) * HEAD_DIM)
                x = srcs[r][pl.ds(h, PAGE_SIZE, stride=NSA_KV_HEADS), :]
                a, bsum = _chunk_sums(x, wpos_ref[idx, 0:CMP_STRIDE, sl], wpos_ref[idx, CMP_STRIDE:, sl])
                a_ref[r * per:(r + 1) * per, sl] = a
                b_ref[r * per:(r + 1) * per, sl] = bsum


def _cmp_dec(k_pool, v_pool, layer, page_table, wpos, n_pg=8):
    b, n_pages = page_table.shape
    kvw = NSA_KV_HEADS * HEAD_DIM
    per = PAGE_SIZE // CMP_STRIDE
    rows = PAGE_SIZE * NSA_KV_HEADS
    k_pool = k_pool.reshape(k_pool.shape[:2] + (rows, HEAD_DIM))
    v_pool = v_pool.reshape(v_pool.shape[:2] + (rows, HEAD_DIM))

    def page(r):
        return pl.BlockSpec((None, None, rows, HEAD_DIM), lambda bi, c, pt: (layer, pt[bi, c * n_pg + r], 0, 0))

    ospec = pl.BlockSpec((None, n_pg * per, kvw), lambda bi, c, pt: (bi, c, 0))
    out = jax.ShapeDtypeStruct((b, n_pages * per, kvw), F32)
    grid_spec = pltpu.PrefetchScalarGridSpec(
        num_scalar_prefetch=1,
        grid=(b, n_pages // n_pg),
        in_specs=[page(r) for r in range(n_pg)] * 2 + [pl.BlockSpec(wpos.shape, lambda bi, c, pt: (0, 0, 0))],
        out_specs=[ospec] * 4,
    )
    return pl.pallas_call(
        functools.partial(_cmp_dec_kernel, n_pg=n_pg),
        grid_spec=grid_spec,
        out_shape=[out] * 4,
        compiler_params=_cp("parallel", "parallel"),
        name="cmp_decode",
    )(page_table, *([k_pool] * n_pg), *([v_pool] * n_pg), wpos)


def _nsa_dec_cmp_kernel(q_ref, ak_ref, bk_ref, av_ref, bv_ref, wlin_ref, bias_ref, oc_ref, idx_ref,
                        *, pos, n_cmp, n_slc):
    nrow = ak_ref.shape[0]
    sw = idx_ref.shape[-1]
    lane_n = lax.broadcasted_iota(jnp.int32, (1, nrow), 1)
    cmask = ((lane_n * CMP_STRIDE + CMP_BLOCK - 1) <= pos) & (lane_n < n_cmp)
    on = lax.broadcasted_iota(jnp.int32, (nrow, sw), 0)
    osb = lax.broadcasted_iota(jnp.int32, (nrow, sw), 1)
    c_start = on * CMP_STRIDE
    overlap = jnp.where((c_start < osb * SEL_BLOCK + SEL_BLOCK) & (c_start + CMP_BLOCK - 1 >= osb * SEL_BLOCK)
                        & (on < n_cmp) & (osb < n_slc), 1.0, 0.0).astype(BF16)
    pk = ak_ref[...] + pltpu.roll(bk_ref[...], nrow - 1, 0)
    pv = av_ref[...] + pltpu.roll(bv_ref[...], nrow - 1, 0)
    lane_s = lax.broadcasted_iota(jnp.int32, (1, sw), 1)
    ri = lax.broadcasted_iota(jnp.int32, (sw, sw), 0)
    ci = lax.broadcasted_iota(jnp.int32, (sw, sw), 1)
    k_top = min(N_SEL, n_slc)
    cur = pos // SEL_BLOCK
    for h in range(NSA_KV_HEADS):
        sl = slice(h * HEAD_DIM, (h + 1) * HEAD_DIM)
        kc = jnp.dot(pk[:, sl].astype(BF16), wlin_ref[0, h], preferred_element_type=F32).astype(BF16)
        vc = jnp.dot(pv[:, sl].astype(BF16), wlin_ref[1, h], preferred_element_type=F32).astype(BF16)
        q = q_ref[h].astype(BF16)
        s = _nt(q, kc) * (HEAD_DIM ** -0.5) + bias_ref[h * NSA_GROUP:(h + 1) * NSA_GROUP]
        s = jnp.where(cmask, s, NEG_INF)
        s = s - jnp.max(s, axis=-1, keepdims=True)
        e = jnp.exp(s)
        pc = jnp.where(cmask, e / jnp.sum(e, axis=-1, keepdims=True), 0.0)
        oc_ref[h] = jnp.dot(pc.astype(BF16), vc, preferred_element_type=F32)
        pcs = jnp.sum(pc, axis=0, keepdims=True)
        hi, mid, lo = _split3(jnp.broadcast_to(pcs, (8, nrow)))
        imp = (jnp.dot(hi, overlap, preferred_element_type=F32) + jnp.dot(mid, overlap, preferred_element_type=F32)
               + jnp.dot(lo, overlap, preferred_element_type=F32))[0:1]
        forced = (lane_s == 0) | (lane_s == cur) | (lane_s == cur - 1)
        score = jnp.where(lane_s > cur, NEG_INF, jnp.where(forced, BIG, imp))
        score = jnp.where(lane_s < n_slc, score, -jnp.inf)
        rowm = jnp.broadcast_to(score, (sw, sw))
        colm = rowm.T
        beats = jnp.where(colm > rowm, 1, jnp.where(colm == rowm, jnp.where(ri < ci, 1, 0), 0))
        beats = jnp.where(ri < n_slc, beats, 0)
        rank = jnp.sum(beats, axis=0, keepdims=True)
        keep = (rank < k_top) & (score > 0.5 * NEG_INF) & (lane_s < n_slc)
        out = jnp.full((1, sw), -1, jnp.int32)
        for r in range(k_top):
            hit = keep & (rank == r)
            val = jnp.sum(jnp.where(hit, lane_s + 1, 0), axis=1, keepdims=True) - 1
            out = jnp.where(lane_s == r, val, out)
        idx_ref[h] = out


def _nsa_dec_cmp(q, ak, bk, av, bv, wlin, bias_c, pos, n_cmp, n_slc):
    b = q.shape[0]
    nrow = ak.shape[1]
    kvw = ak.shape[2]
    sw = -(-n_slc // LANES) * LANES
    part = pl.BlockSpec((None, nrow, kvw), lambda i: (i, 0, 0))
    kern = functools.partial(_nsa_dec_cmp_kernel, pos=pos, n_cmp=n_cmp, n_slc=n_slc)
    return pl.pallas_call(
        kern,
        name="nsa_dec_cmp",
        grid=(b,),
        in_specs=[pl.BlockSpec((None,) + q.shape[1:], lambda i: (i, 0, 0, 0)), part, part, part, part,
                  pl.BlockSpec(wlin.shape, lambda i: (0, 0, 0, 0)),
                  pl.BlockSpec(bias_c.shape, lambda i: (0, 0))],
        out_specs=[pl.BlockSpec((None,) + q.shape[1:], lambda i: (i, 0, 0, 0)),
                   pl.BlockSpec((None, NSA_KV_HEADS, 1, sw), lambda i: (i, 0, 0, 0))],
        out_shape=[jax.ShapeDtypeStruct(q.shape, F32), jax.ShapeDtypeStruct((b, NSA_KV_HEADS, 1, sw), jnp.int32)],
        compiler_params=_cp("parallel"),
    )(q, ak, bk, av, bv, wlin, bias_c)


def _bucket_bias(dist, rb):
    n = jnp.maximum(dist, 0)
    max_exact = N_BUCKETS // 2
    nf = jnp.maximum(n, 1).astype(F32)
    large = max_exact + (jnp.log(nf / max_exact) / math.log(MAX_DISTANCE / max_exact)
                         * (N_BUCKETS - max_exact)).astype(jnp.int32)
    bucket = jnp.where(n < max_exact, n, jnp.minimum(large, N_BUCKETS - 1))
    out = jnp.zeros((rb.shape[0], dist.shape[1]), F32)
    for k in range(N_BUCKETS):
        out = jnp.where(bucket == k, rb[:, k:k + 1], out)
    return out


def _nsa_dec_sel_kernel(pg_ref, hf_ref, blk_ref, q_ref, kn_ref, vn_ref, rb_ref, k0_ref, k1_ref, v0_ref, v1_ref,
                        o_ref, m_sc, l_sc, acc_sc, *, pos):
    bi = pl.program_id(0)
    j = pl.program_id(1)
    scale = HEAD_DIM ** -0.5
    lane = lax.broadcasted_iota(jnp.int32, (1, SEL_BLOCK), 1)

    @pl.when(j == 0)
    def _():
        for h in range(NSA_KV_HEADS):
            q = q_ref[h]
            rb = rb_ref[h]
            s = jnp.sum(q.astype(BF16).astype(F32) * kn_ref[h:h + 1, :].astype(BF16).astype(F32),
                        axis=-1, keepdims=True) * scale + rb[:, 0:1]
            m_sc[h] = s
            l_sc[h] = jnp.ones_like(s)
            acc_sc[h] = jnp.broadcast_to(vn_ref[h:h + 1, :].astype(BF16).astype(F32), (q.shape[0], HEAD_DIM))

    for h, (k_ref, v_ref) in enumerate(((k0_ref, v0_ref), (k1_ref, v1_ref))):
        blk = blk_ref[bi, h * pl.num_programs(1) + j]

        @pl.when(blk >= 0)
        def _(h=h, k_ref=k_ref, v_ref=v_ref, blk=blk):
            q = q_ref[h].astype(BF16)
            k = k_ref[:, h, :].astype(BF16)
            v = v_ref[:, h, :].astype(BF16)
            tok = blk * SEL_BLOCK + lane
            ok = tok <= pos
            s = _nt(q, k) * scale + _bucket_bias(pos - tok, rb_ref[h])
            s = jnp.where(ok, s, NEG_INF)
            m_prev = m_sc[h]
            m_new = jnp.maximum(m_prev, jnp.max(s, axis=-1, keepdims=True))
            alpha = jnp.exp(m_prev - m_new)
            p = jnp.where(ok, jnp.exp(s - m_new), 0.0)
            l_sc[h] = alpha * l_sc[h] + jnp.sum(p, axis=-1, keepdims=True)
            acc_sc[h] = alpha * acc_sc[h] + jnp.dot(p.astype(BF16), v, preferred_element_type=F32)
            m_sc[h] = m_new

    @pl.when(j == pl.num_programs(1) - 1)
    def _():
        for h in range(NSA_KV_HEADS):
            o_ref[h] = acc_sc[h] / l_sc[h]


def _nsa_dec_sel(q, k_new, v_new, rb, k_pool, v_pool, layer, pages, halves, blocks, pos):
    b = q.shape[0]
    k_top = blocks.shape[-1]
    pages, halves, blocks = (a.reshape(b, NSA_KV_HEADS * k_top) for a in (pages, halves, blocks))

    def blkspec(h):
        return pl.BlockSpec((None, None, SEL_BLOCK, NSA_KV_HEADS, HEAD_DIM),
                            lambda bi, j, pg, hf, bl: (layer, pg[bi, h * k_top + j], hf[bi, h * k_top + j], 0, 0))

    qspec = pl.BlockSpec((None,) + q.shape[1:], lambda bi, j, pg, hf, bl: (bi, 0, 0, 0))
    nspec = pl.BlockSpec((None, NSA_KV_HEADS, HEAD_DIM), lambda bi, j, pg, hf, bl: (bi, 0, 0))
    grid_spec = pltpu.PrefetchScalarGridSpec(
        num_scalar_prefetch=3,
        grid=(b, k_top),
        in_specs=[qspec, nspec, nspec, pl.BlockSpec(rb.shape, lambda bi, j, pg, hf, bl: (0, 0, 0)),
                  blkspec(0), blkspec(1), blkspec(0), blkspec(1)],
        out_specs=qspec,
        scratch_shapes=[pltpu.VMEM((NSA_KV_HEADS, NSA_GROUP, 1), F32), pltpu.VMEM((NSA_KV_HEADS, NSA_GROUP, 1), F32),
                        pltpu.VMEM((NSA_KV_HEADS, NSA_GROUP, HEAD_DIM), F32)],
    )
    return pl.pallas_call(
        functools.partial(_nsa_dec_sel_kernel, pos=pos),
        name="nsa_dec_sel",
        grid_spec=grid_spec,
        out_shape=jax.ShapeDtypeStruct(q.shape, F32),
        compiler_params=_cp("parallel", "arbitrary"),
    )(pages, halves, blocks, q, k_new, v_new, rb, k_pool, k_pool, v_pool, v_pool)


def _nsa_dec_win_kernel(q_ref, g_ref, win_ref, new_ref, bias_ref, oc_ref, os_ref, y_ref, nw_ref):
    scale = HEAD_DIM ** -0.5
    nwin = win_ref.shape[0]
    sig = jax.nn.sigmoid(g_ref[...])
    for h in range(NSA_KV_HEADS):
        q = q_ref[h].astype(BF16)
        k = win_ref[:, 0, h, :].astype(BF16)
        v = win_ref[:, 1, h, :].astype(BF16)
        kn = new_ref[0, h:h + 1, :].astype(BF16)
        vn = new_ref[1, h:h + 1, :].astype(BF16)
        bias = bias_ref[h * NSA_GROUP:(h + 1) * NSA_GROUP]
        s = _nt(q, k) * scale + bias[:, :nwin]
        s_new = jnp.sum(q.astype(F32) * kn.astype(F32), axis=-1, keepdims=True) * scale + bias[:, nwin:nwin + 1]
        m = jnp.maximum(jnp.max(s, axis=-1, keepdims=True), s_new)
        p = jnp.exp(s - m)
        p_new = jnp.exp(s_new - m)
        den = jnp.sum(p, axis=-1, keepdims=True) + p_new
        ow = (jnp.dot(p.astype(BF16), v, preferred_element_type=F32)
              + p_new.astype(BF16).astype(F32) * vn.astype(F32)) / den
        gs = sig[h]
        y_ref[h] = (gs[:, 0:1] * oc_ref[h] + gs[:, 1:2] * os_ref[h] + gs[:, 2:3] * ow).astype(y_ref.dtype)
    nw_ref[pl.ds(0, nwin - 1)] = win_ref[pl.ds(1, nwin - 1)]
    nw_ref[nwin - 1] = new_ref[...]


def _nsa_dec_win(q, gates, win, new_kv, bias_w, o_c, o_s):
    b = q.shape[0]
    qspec = pl.BlockSpec((None,) + q.shape[1:], lambda i: (i, 0, 0, 0))
    wspec = pl.BlockSpec((None,) + win.shape[1:], lambda i: (i, 0, 0, 0, 0))
    bias2 = bias_w[:, 0, :]
    return pl.pallas_call(
        _nsa_dec_win_kernel,
        name="nsa_dec_win",
        grid=(b,),
        in_specs=[qspec, pl.BlockSpec((None,) + gates.shape[1:], lambda i: (i, 0, 0, 0)), wspec,
                  pl.BlockSpec((None,) + new_kv.shape[1:], lambda i: (i, 0, 0, 0)),
                  pl.BlockSpec(bias2.shape, lambda i: (0, 0)), qspec, qspec],
        out_specs=[qspec, wspec],
        out_shape=[jax.ShapeDtypeStruct(q.shape, BF16), jax.ShapeDtypeStruct(win.shape, F32)],
        compiler_params=_cp("parallel"),
    )(q, gates, win, new_kv, bias2, o_c, o_s)


def _tail(x, xb_unused, y_mix, mem_kv, ffn_prev, p, l, bsz, t, alpha):
    d = x.shape[1]
    x1, x1b = _add_ln(x, y_mix, p['ln_g'][l, 0], p['ln_b'][l, 0], alpha)
    tm = 1024
    q = _mm(x1b, p['w_cq'], F32, tm, 512, "mm_cq", layer=l)
    o = _cross(q.reshape(bsz, t, -1), mem_kv, 512).reshape(bsz * t, -1)
    x2, x2b = _mm_add_ln(o, p['w_co'], l, x1, p['ln_g'][l, 1], p['ln_b'][l, 1], alpha)
    if t > 1:
        act, st1, st2 = _ffn_up(x2b, p['w_up'], l, p['ffn_conv'][l], t)
        ffn_new = jnp.concatenate([st1[:, 6:], st2[:, 6:]], axis=-1)
    else:
        act, ffn_new = _ffn_up_dec(x2b, p['w_up'], l, p['ffn_conv'][l], ffn_prev)
    dff = act.shape[1]
    f = _mm_acc(act, p['w_down'], l, F32, 1024, 512, dff // 2)
    x3, x3b = _add_ln(x2, f, p['ln_g'][l, 2], p['ln_b'][l, 2], alpha)
    return x3, x3b, ffn_new


def kernel(x_prompt, x_sample, mem_prompt, state_pool, cache_nsa_cmp_k, cache_nsa_cmp_v, cache_nsa_sel_k, cache_nsa_sel_v, state_nsa_win, state_sc, cache_sb_k, cache_sb_v, state_ffn, cache_mem, page_table, w_in_even, w_pool, pool_scale, w_cmp_pos, w_cmp_lin, rel_bias, w_out_even, w_in_odd, sc_conv, w_out_odd, w_cq, w_ckv, w_co, w_up, ffn_conv, w_down, ln_g, ln_b):
    bp, t, d = x_prompt.shape
    bs = x_sample.shape[0]
    depth = w_cq.shape[0]
    n_pages = page_table.shape[1]
    past = n_pages * PAGE_SIZE
    alpha = (2.0 * depth) ** 0.25
    kvw = NSA_KV_HEADS * HEAD_DIM
    assert x_sample.shape[1] == 1 and state_nsa_win.shape[2] == WINDOW

    w_even = jnp.pad(w_in_even, ((0, 0), (0, 0), (0, E_TOT - w_in_even.shape[2]))).astype(BF16)
    p = {'w_cq': w_cq.astype(BF16), 'w_co': w_co.astype(BF16), 'w_up': w_up,
         'w_down': w_down.astype(BF16), 'ffn_conv': ffn_conv, 'ln_g': ln_g, 'ln_b': ln_b}
    w_odd = w_in_odd.astype(BF16)
    w_oe = w_out_even.astype(BF16)
    w_oo = w_out_odd.astype(BF16)
    w_kv = w_ckv.astype(BF16)
    w_pool_b = w_pool.astype(BF16)
    w_lin_b = w_cmp_lin.astype(BF16)
    w_pos = w_cmp_pos.reshape(w_cmp_pos.shape[0], 2, CMP_BLOCK, kvw)

    qb = 128
    kj = np.arange(qb)[:, None]
    qt = np.arange(qb)[None, :]
    dist_d = np.stack([qt - kj, qb + qt - kj]).reshape(2 * qb, qb).astype(np.int32)
    tabd = _bias_lookup(rel_bias, jnp.asarray(dist_d), 2 * qb).reshape(NSA_HEADS, 2, qb, qb)
    nchunk = t // CMP_STRIDE
    dist_c = (np.arange(t)[None, :] - (np.arange(nchunk)[:, None] * CMP_STRIDE + CMP_BLOCK - 1)).astype(np.int32)
    tabc = _bias_lookup_blocked(rel_bias, jnp.asarray(dist_c), qb)
    b31 = rel_bias[N_BUCKETS - 1]

    mem_b = mem_prompt.reshape(bp * mem_prompt.shape[1], d).astype(BF16)
    n_mem = mem_prompt.shape[1]

    x = x_prompt.reshape(bp * t, d)
    xb = x.astype(BF16)
    outs_p = {k: [] for k in ('pool', 'cmp_k', 'cmp_v', 'sel_k', 'sel_v', 'win', 'sc', 'sb_k', 'sb_v', 'ffn', 'mem')}
    for l in range(depth):
        e = l // 2
        memkv = _mm(mem_b, w_kv, F32, 1024, 512, "mm_memkv", layer=l)
        outs_p['mem'].append(memkv.reshape(bp, n_mem, 2, MEM_HEADS, MEM_HEAD_DIM))
        if l % 2 == 0:
            qw = E_KC - E_Q
            hu = _mm(xb, w_even[e], F32, 1024, 512, "mm_in_even", E_U, E_Q - E_U).reshape(bp, t, E_Q - E_U)
            hq = _mm(xb, w_even[e], BF16, 1024, 512, "mm_in_even", E_Q, qw).reshape(bp, t, qw)
            h = _mm(xb, w_even[e], F32, 1024, 512, "mm_in_even", E_R, E_TOT - E_R).reshape(bp, t, E_TOT - E_R)
            y_pool = _pool_prompt(hu, 0, w_pool_b[e], pool_scale[e])
            kcmp, vcmp = _compress_prompt(h, w_pos[e], w_lin_b[e])
            y_nsa = _nsa_prompt(hq, h, kcmp, vcmp, tabc, tabd, b31)
            mix = jnp.concatenate([y_pool, y_nsa], axis=-1).reshape(bp * t, d)
            y = _mm(mix, w_oe[e], F32, 1024, 512, "mm_out")
            outs_p['pool'].append(hu[:, t - POOL_STATE:])
            for name, off in (('cmp_k', E_KC), ('cmp_v', E_VC), ('sel_k', E_KS), ('sel_v', E_VS)):
                outs_p[name].append(h[:, :, off - E_R:off - E_R + kvw].reshape(bp, t, NSA_KV_HEADS, HEAD_DIM))
            nw = min(WINDOW, t)
            outs_p['win'].append(h[:, t - nw:, E_KVW - E_R:E_KVW - E_R + 2 * kvw]
                                 .reshape(bp, nw, 2, NSA_KV_HEADS, HEAD_DIM))
        else:
            sbw = O_K - O_Q
            hc = _mm(xb, w_odd[e], F32, 1024, 512, "mm_in_odd", O_X, O_Q).reshape(bp, t, O_Q)
            hq = _mm(xb, w_odd[e], BF16, 1024, 512, "mm_in_odd", O_Q, sbw).reshape(bp, t, sbw)
            hk = _mm(xb, w_odd[e], F32, 1024, 512, "mm_in_odd", O_K, sbw).reshape(bp, t, sbw)
            hv = _mm(xb, w_odd[e], F32, 1024, 512, "mm_in_odd", O_V, sbw).reshape(bp, t, sbw)
            y_sc, sc_st = _sconv_prompt(hc, sc_conv[e])
            y_sb = _sb_prompt(hq, hk, hv)
            mix = jnp.concatenate([y_sc, y_sb], axis=-1).reshape(bp * t, d)
            y = _mm(mix, w_oo[e], F32, 1024, 512, "mm_out")
            outs_p['sc'].append(sc_st[:, 6:])
            nh = sbw // HEAD_DIM
            outs_p['sb_k'].append(hk.reshape(bp, t, nh, HEAD_DIM))
            outs_p['sb_v'].append(hv.reshape(bp, t, nh, HEAD_DIM))
        x, xb, ffn_new = _tail(x, xb, y, memkv.reshape(bp, n_mem, -1), None, p, l, bp, t, alpha)
        outs_p['ffn'].append(ffn_new)
    y_prompt = x.reshape(bp, t, d)

    pos = past
    length = past + 1
    n_cmp = (length - CMP_BLOCK) // CMP_STRIDE + 1
    n_slc = -(-length // SEL_BLOCK)
    k_top = min(N_SEL, n_slc)
    ncrow = n_pages * (PAGE_SIZE // CMP_STRIDE)
    dist_cd = (pos - (np.arange(ncrow) * CMP_STRIDE + CMP_BLOCK - 1)).astype(np.int32)
    bias_cd = _bias_lookup(rel_bias, jnp.asarray(np.broadcast_to(dist_cd, (8, ncrow))), 8)[:, 0, :]
    dist_wd = np.maximum(WINDOW - np.arange(WINDOW + LANES), 0).astype(np.int32)
    bias_wd = _bias_lookup(rel_bias, jnp.asarray(np.broadcast_to(dist_wd, (8, WINDOW + LANES))), 8)
    rb_hg = rel_bias.T.reshape(NSA_KV_HEADS, NSA_GROUP, N_BUCKETS)
    mem_s = cache_mem.reshape(depth, bs, cache_mem.shape[2], -1)

    x = x_sample.reshape(bs, d)
    xb = x.astype(BF16)
    outs_s = {k: [] for k in ('pool', 'cmp_k', 'cmp_v', 'sel_k', 'sel_v', 'win', 'sc', 'sb_k', 'sb_v', 'ffn')}
    for l in range(depth):
        e = l // 2
        if l % 2 == 0:
            h = _mm(xb, w_even[e], F32, 8, 512)
            y_pool, pool_new = _pool_dec(state_pool[e], h[:, E_U:E_U + 1024], w_pool_b[e], pool_scale[e], pos)
            q4 = h[:, E_Q:E_Q + NSA_HEADS * HEAD_DIM].reshape(bs, NSA_KV_HEADS, NSA_GROUP, HEAD_DIM)
            gates = h[:, E_G:E_G + NSA_HEADS * 3].reshape(bs, NSA_KV_HEADS, NSA_GROUP, 3)
            ak, bk, av, bv = _cmp_dec(cache_nsa_cmp_k, cache_nsa_cmp_v, e, page_table, w_pos[e])
            o_c, idx = _nsa_dec_cmp(q4, ak, bk, av, bv, w_lin_b[e], bias_cd, pos, n_cmp, n_slc)
            blocks = idx[:, :, 0, :k_top]
            past_blk = jnp.where((blocks >= 0) & (blocks * SEL_BLOCK < past), blocks, -1)
            safe = jnp.maximum(past_blk, 0)
            per_page = PAGE_SIZE // SEL_BLOCK
            pages = jnp.take_along_axis(page_table[:, None, :], safe // per_page, axis=2)
            halves = safe % per_page
            ks_new = h[:, E_KS:E_KS + kvw].reshape(bs, NSA_KV_HEADS, HEAD_DIM)
            vs_new = h[:, E_VS:E_VS + kvw].reshape(bs, NSA_KV_HEADS, HEAD_DIM)
            o_s = _nsa_dec_sel(q4, ks_new, vs_new, rb_hg, cache_nsa_sel_k, cache_nsa_sel_v, e,
                               pages, halves, past_blk, pos)
            new_kv = h[:, E_KVW:E_KVW + 2 * kvw].reshape(bs, 2, NSA_KV_HEADS, HEAD_DIM)
            y_nsa, win_new = _nsa_dec_win(q4, gates, state_nsa_win[e], new_kv, bias_wd, o_c, o_s)
            mix = jnp.concatenate([y_pool, y_nsa.reshape(bs, -1)], axis=-1)
            y = _mm(mix, w_oe[e], F32, 8, 512)
            outs_s['pool'].append(pool_new)
            for name, off in (('cmp_k', E_KC), ('cmp_v', E_VC), ('sel_k', E_KS), ('sel_v', E_VS)):
                outs_s[name].append(h[:, off:off + kvw].reshape(bs, 1, NSA_KV_HEADS, HEAD_DIM))
            outs_s['win'].append(win_new)
        else:
            h = _mm(xb, w_odd[e], F32, 8, 512)
            y_sc, sc_new = _sconv_dec(state_sc[e], h, sc_conv[e])
            nh = (O_K - O_Q) // HEAD_DIM
            q3 = h[:, O_Q:O_K].reshape(bs, nh, HEAD_DIM)
            y_sb = _sb_dec(q3, cache_sb_k, cache_sb_v, e, page_table)
            mix = jnp.concatenate([y_sc, y_sb.reshape(bs, -1)], axis=-1)
            y = _mm(mix, w_oo[e], F32, 8, 512)
            outs_s['sc'].append(sc_new)
            outs_s['sb_k'].append(h[:, O_K:O_V].reshape(bs, 1, nh, HEAD_DIM))
            outs_s['sb_v'].append(h[:, O_V:].reshape(bs, 1, nh, HEAD_DIM))
        x, xb, ffn_new = _tail(x, xb, y, mem_s[l], state_ffn[l], p, l, bs, 1, alpha)
        outs_s['ffn'].append(ffn_new)
    y_sample = x.reshape(bs, 1, d)

    sp = {k: jnp.stack(v) for k, v in outs_p.items()}
    ss = {k: jnp.stack(v) for k, v in outs_s.items()}
    return (y_prompt, y_sample,
            sp['pool'], sp['cmp_k'], sp['cmp_v'], sp['sel_k'], sp['sel_v'], sp['win'],
            sp['sc'], sp['sb_k'], sp['sb_v'], sp['ffn'], sp['mem'],
            ss['pool'], ss['cmp_k'], ss['cmp_v'], ss['sel_k'], ss['sel_v'], ss['win'],
            ss['sc'], ss['sb_k'], ss['sb_v'], ss['ffn'])
```

```python
import functools
import math

import jax
import jax.numpy as jnp
import numpy as np
from jax import lax
from jax.experimental import pallas as pl
from jax.experimental.pallas import tpu as pltpu

F32 = jnp.float32
BF16 = jnp.bfloat16

HEAD_DIM = 128
PAGE_SIZE = 128
POOL_WINDOWS = (2, 4, 8, 16)
POOL_GROUP_DIM = 256
POOL_STATE = 15
NSA_KV_HEADS = 2
NSA_GROUP = 12
NSA_HEADS = 24
CMP_BLOCK = 32
CMP_STRIDE = 16
SEL_BLOCK = 64
SEL_SHIFT = 6
N_SEL = 16
WINDOW = 512
N_BUCKETS = 32
MAX_DISTANCE = 128
MEM_HEADS = 4
MEM_HEAD_DIM = 256
LN_EPS = 1e-5
NEG_INF = -1e30
BIG = 1e30

LANES = 128
VMEM_LIMIT = 56 * 1024 * 1024

E_U, E_Q, E_KC, E_VC, E_KS, E_VS, E_KVW, E_G, E_TOT = 0, 1024, 4096, 4352, 4608, 4864, 5120, 5632, 6144
E_R = E_KC
O_X, O_B, O_C, O_Q, O_K, O_V = 0, 1024, 2048, 3072, 6144, 9216


def _cp(*sem):
    return pltpu.CompilerParams(dimension_semantics=sem, vmem_limit_bytes=VMEM_LIMIT)


def _nt(a, b):
    return lax.dot_general(a, b, (((1,), (1,)), ((), ())), preferred_element_type=F32)


def _split3(x):
    hi = x.astype(BF16)
    r1 = x - hi.astype(F32)
    mid = r1.astype(BF16)
    lo = (r1 - mid.astype(F32)).astype(BF16)
    return hi, mid, lo


def _mm_kernel(x_ref, w_ref, o_ref):
    o_ref[...] = jnp.dot(x_ref[...], w_ref[...], preferred_element_type=F32).astype(o_ref.dtype)


def _mm(x, w, out_dtype, tm, tn, name="mm", col0=0, ncols=None, layer=None):
    m, k = x.shape
    n = w.shape[-1] - col0 if ncols is None else ncols
    tm = min(tm, m)
    c0 = col0 // tn
    assert col0 % tn == 0 and n % tn == 0
    if layer is None:
        wspec = pl.BlockSpec((k, tn), lambda i, j: (0, j + c0))
    else:
        wspec = pl.BlockSpec((None, k, tn), lambda i, j: (layer, 0, j + c0))
    return pl.pallas_call(
        _mm_kernel,
        name=name,
        grid=(m // tm, n // tn),
        in_specs=[pl.BlockSpec((tm, k), lambda i, j: (i, 0)), wspec],
        out_specs=pl.BlockSpec((tm, tn), lambda i, j: (i, j)),
        out_shape=jax.ShapeDtypeStruct((m, n), out_dtype),
        compiler_params=_cp("parallel", "parallel"),
    )(x, w)


def _mm_acc_kernel(x_ref, w_ref, o_ref, acc_ref):
    k = pl.program_id(2)

    @pl.when(k == 0)
    def _():
        acc_ref[...] = jnp.zeros_like(acc_ref)

    acc_ref[...] += jnp.dot(x_ref[...], w_ref[...], preferred_element_type=F32)

    @pl.when(k == pl.num_programs(2) - 1)
    def _():
        o_ref[...] = acc_ref[...].astype(o_ref.dtype)


def _mm_acc(x, w, layer, out_dtype, tm, tn, tk):
    m, k = x.shape
    n = w.shape[2]
    tm = min(tm, m)
    return pl.pallas_call(
        _mm_acc_kernel,
        name="mm_down",
        grid=(m // tm, n // tn, k // tk),
        in_specs=[pl.BlockSpec((tm, tk), lambda i, j, kk: (i, kk)),
                  pl.BlockSpec((None, tk, tn), lambda i, j, kk: (layer, kk, j))],
        out_specs=pl.BlockSpec((tm, tn), lambda i, j, kk: (i, j)),
        out_shape=jax.ShapeDtypeStruct((m, n), out_dtype),
        scratch_shapes=[pltpu.VMEM((tm, tn), F32)],
        compiler_params=_cp("parallel", "parallel", "arbitrary"),
    )(x, w)


def _ln_kernel(x_ref, y_ref, g_ref, b_ref, of_ref, ob_ref, *, alpha):
    z = alpha * x_ref[...] + y_ref[...]
    mu = jnp.mean(z, axis=-1, keepdims=True)
    zc = z - mu
    var = jnp.mean(zc * zc, axis=-1, keepdims=True)
    out = zc * lax.rsqrt(var + LN_EPS) * g_ref[...] + b_ref[...]
    of_ref[...] = out
    ob_ref[...] = out.astype(BF16)


def _add_ln(x, y, g, b, alpha, tm=256):
    m, d = x.shape
    tm = min(tm, m)
    row = pl.BlockSpec((tm, d), lambda i: (i, 0))
    vec = pl.BlockSpec((1, d), lambda i: (0, 0))
    return pl.pallas_call(
        functools.partial(_ln_kernel, alpha=alpha),
        name="add_ln",
        grid=(m // tm,),
        in_specs=[row, row, vec, vec],
        out_specs=[row, row],
        out_shape=[jax.ShapeDtypeStruct((m, d), F32), jax.ShapeDtypeStruct((m, d), BF16)],
        compiler_params=_cp("parallel"),
    )(x, y, g.reshape(1, d), b.reshape(1, d))


def _mm_ln_kernel(x_ref, w_ref, r_ref, g_ref, b_ref, of_ref, ob_ref, *, alpha):
    y = jnp.dot(x_ref[...], w_ref[...], preferred_element_type=F32)
    z = alpha * r_ref[...] + y
    mu = jnp.mean(z, axis=-1, keepdims=True)
    zc = z - mu
    var = jnp.mean(zc * zc, axis=-1, keepdims=True)
    out = zc * lax.rsqrt(var + LN_EPS) * g_ref[...] + b_ref[...]
    of_ref[...] = out
    ob_ref[...] = out.astype(BF16)


def _mm_add_ln(x, w, layer, res, g, b, alpha, tm=256):
    m, k = x.shape
    d = w.shape[2]
    tm = min(tm, m)
    row = pl.BlockSpec((tm, d), lambda i: (i, 0))
    vec = pl.BlockSpec((1, d), lambda i: (0, 0))
    return pl.pallas_call(
        functools.partial(_mm_ln_kernel, alpha=alpha),
        name="mm_add_ln",
        grid=(m // tm,),
        in_specs=[pl.BlockSpec((tm, k), lambda i: (i, 0)),
                  pl.BlockSpec((None, k, d), lambda i: (layer, 0, 0)), row, vec, vec],
        out_specs=[row, row],
        out_shape=[jax.ShapeDtypeStruct((m, d), F32), jax.ShapeDtypeStruct((m, d), BF16)],
        compiler_params=_cp("parallel"),
    )(x, w, res, g.reshape(1, d), b.reshape(1, d))


def _cross_kernel(q_ref, kv_ref, o_ref):
    width = MEM_HEADS * MEM_HEAD_DIM
    scale = MEM_HEAD_DIM ** -0.5
    for h in range(MEM_HEADS):
        sl = slice(h * MEM_HEAD_DIM, (h + 1) * MEM_HEAD_DIM)
        q = q_ref[:, sl].astype(BF16)
        k = kv_ref[:, sl].astype(BF16)
        v = kv_ref[:, width + h * MEM_HEAD_DIM: width + (h + 1) * MEM_HEAD_DIM].astype(BF16)
        s = _nt(q, k) * scale
        s = s - jnp.max(s, axis=-1, keepdims=True)
        e = jnp.exp(s)
        p = e / jnp.sum(e, axis=-1, keepdims=True)
        o_ref[:, sl] = jnp.dot(p.astype(BF16), v, preferred_element_type=F32).astype(o_ref.dtype)


def _cross(q, kv, tq):
    b, t, w = q.shape
    tq = min(tq, t)
    return pl.pallas_call(
        _cross_kernel,
        name="cross_attn",
        grid=(b, t // tq),
        in_specs=[pl.BlockSpec((None, tq, w), lambda i, j: (i, j, 0)),
                  pl.BlockSpec((None, kv.shape[1], kv.shape[2]), lambda i, j: (i, 0, 0))],
        out_specs=pl.BlockSpec((None, tq, w), lambda i, j: (i, j, 0)),
        out_shape=jax.ShapeDtypeStruct((b, t, w), BF16),
        compiler_params=_cp("parallel", "parallel"),
    )(q, kv)


def _shift_rows(h, c_last2, row):
    r1 = pltpu.roll(h, 1, 0)
    r2 = pltpu.roll(h, 2, 0)
    row8 = row[0:8]
    f1 = jnp.where(row8 == 0, c_last2[7:8], r1[0:8])
    f2 = jnp.where(row8 == 0, c_last2[6:7], jnp.where(row8 == 1, c_last2[7:8], r2[0:8]))
    return jnp.concatenate([f1, r1[8:]], axis=0), jnp.concatenate([f2, r2[8:]], axis=0)


def _ffn_up_kernel(x_ref, w1_ref, w2_ref, cw1_ref, cw2_ref, act_ref, st1_ref, st2_ref, *, sub):
    t = x_ref.shape[0]
    tn = w1_ref.shape[1]
    ns = t // sub
    w1 = w1_ref[...].astype(BF16)
    w2 = w2_ref[...].astype(BF16)
    cw1 = cw1_ref[...]
    cw2 = cw2_ref[...]
    row = lax.broadcasted_iota(jnp.int32, (sub, 1), 0)
    zeros8 = jnp.zeros((8, tn), F32)
    h = [None] * ns

    def dots(s):
        xs = x_ref[s * sub:(s + 1) * sub, :]
        return jnp.dot(xs, w1, preferred_element_type=F32), jnp.dot(xs, w2, preferred_element_type=F32)

    def conv(hc, prev8, cw):
        s1, s2 = _shift_rows(hc, prev8, row)
        return s2 * cw[0:1] + s1 * cw[1:2] + hc * cw[2:3]

    def epilogue(s):
        p1, p2 = (zeros8, zeros8) if s == 0 else (h[s - 1][0][sub - 8:], h[s - 1][1][sub - 8:])
        c1 = conv(h[s][0], p1, cw1)
        c2 = conv(h[s][1], p2, cw2)
        act_ref[s * sub:(s + 1) * sub, :] = (c1 * jax.nn.sigmoid(c1) * c2).astype(BF16)

    h[0] = dots(0)
    for s in range(1, ns):
        h[s] = dots(s)
        epilogue(s - 1)
    epilogue(ns - 1)
    st1_ref[...] = h[ns - 1][0][sub - 8:]
    st2_ref[...] = h[ns - 1][1][sub - 8:]


def _ffn_up(x, w_up, layer, conv_w, t, sub=512, tn=256):
    m, d = x.shape
    dff = w_up.shape[2] // 2
    nt = dff // tn
    sub = min(sub, t)
    bsz = m // t
    return pl.pallas_call(
        functools.partial(_ffn_up_kernel, sub=sub),
        grid=(bsz, nt),
        in_specs=[pl.BlockSpec((t, d), lambda i, j: (i, 0), pipeline_mode=pl.Buffered(1)),
                  pl.BlockSpec((None, d, tn), lambda i, j: (layer, 0, j)),
                  pl.BlockSpec((None, d, tn), lambda i, j: (layer, 0, j + nt)),
                  pl.BlockSpec((3, tn), lambda i, j: (0, j)),
                  pl.BlockSpec((3, tn), lambda i, j: (0, j + nt))],
        out_specs=[pl.BlockSpec((t, tn), lambda i, j: (i, j)),
                   pl.BlockSpec((None, 8, tn), lambda i, j: (i, 0, j)),
                   pl.BlockSpec((None, 8, tn), lambda i, j: (i, 0, j))],
        out_shape=[jax.ShapeDtypeStruct((m, dff), BF16),
                   jax.ShapeDtypeStruct((bsz, 8, dff), F32),
                   jax.ShapeDtypeStruct((bsz, 8, dff), F32)],
        compiler_params=_cp("parallel", "parallel"),
        name="ffn_up",
    )(x, w_up, w_up, conv_w, conv_w)


def _ffn_up_dec_kernel(x_ref, w1_ref, w2_ref, cw1_ref, cw2_ref, p1_ref, p2_ref, act_ref, n1_ref, n2_ref):
    x = x_ref[...]
    h1 = jnp.dot(x, w1_ref[...].astype(BF16), preferred_element_type=F32)
    h2 = jnp.dot(x, w2_ref[...].astype(BF16), preferred_element_type=F32)

    def conv(h, p_ref, cw):
        return p_ref[:, 0, :] * cw[0:1] + p_ref[:, 1, :] * cw[1:2] + h * cw[2:3]

    c1 = conv(h1, p1_ref, cw1_ref[...])
    c2 = conv(h2, p2_ref, cw2_ref[...])
    act_ref[...] = (c1 * jax.nn.sigmoid(c1) * c2).astype(BF16)
    n1_ref[:, 0, :] = p1_ref[:, 1, :]
    n1_ref[:, 1, :] = h1
    n2_ref[:, 0, :] = p2_ref[:, 1, :]
    n2_ref[:, 1, :] = h2


def _ffn_up_dec(x, w_up, layer, conv_w, prev, tn=256):
    b, d = x.shape
    dff = w_up.shape[2] // 2
    nt = dff // tn
    pspec1 = pl.BlockSpec((b, 2, tn), lambda j: (0, 0, j))
    pspec2 = pl.BlockSpec((b, 2, tn), lambda j: (0, 0, j + nt))
    act, n1, n2 = pl.pallas_call(
        _ffn_up_dec_kernel,
        name="ffn_up_dec",
        grid=(nt,),
        in_specs=[pl.BlockSpec((b, d), lambda j: (0, 0)),
                  pl.BlockSpec((None, d, tn), lambda j: (layer, 0, j)),
                  pl.BlockSpec((None, d, tn), lambda j: (layer, 0, j + nt)),
                  pl.BlockSpec((3, tn), lambda j: (0, j)),
                  pl.BlockSpec((3, tn), lambda j: (0, j + nt)),
                  pspec1, pspec2],
        out_specs=[pl.BlockSpec((b, tn), lambda j: (0, j)),
                   pl.BlockSpec((b, 2, tn), lambda j: (0, 0, j)),
                   pl.BlockSpec((b, 2, tn), lambda j: (0, 0, j))],
        out_shape=[jax.ShapeDtypeStruct((b, dff), BF16),
                   jax.ShapeDtypeStruct((b, 2, dff), F32),
                   jax.ShapeDtypeStruct((b, 2, dff), F32)],
        compiler_params=_cp("parallel"),
    )(x, w_up, w_up, conv_w, conv_w, prev, prev)
    return act, jnp.concatenate([n1, n2], axis=-1)


def _bias_kernel(rb_ref, dist_ref, o_ref):
    head = pl.program_id(0)
    n = jnp.maximum(dist_ref[...], 0)
    max_exact = N_BUCKETS // 2
    nf = jnp.maximum(n, 1).astype(F32)
    large = max_exact + (jnp.log(nf / max_exact) / math.log(MAX_DISTANCE / max_exact)
                         * (N_BUCKETS - max_exact)).astype(jnp.int32)
    bucket = jnp.where(n < max_exact, n, jnp.minimum(large, N_BUCKETS - 1))
    out = jnp.zeros(n.shape, F32)
    for k in range(N_BUCKETS):
        out = jnp.where(bucket == k, rb_ref[k, head], out)
    o_ref[...] = out


def _bias_lookup(rel_bias, dist, tr):
    r, c = dist.shape
    nh = rel_bias.shape[1]
    tr = min(tr, r)
    return pl.pallas_call(
        _bias_kernel,
        grid=(nh, r // tr),
        in_specs=[pl.BlockSpec(memory_space=pltpu.SMEM),
                  pl.BlockSpec((tr, c), lambda h, i: (i, 0))],
        out_specs=pl.BlockSpec((None, tr, c), lambda h, i: (h, i, 0)),
        out_shape=jax.ShapeDtypeStruct((nh, r, c), F32),
        compiler_params=_cp("parallel", "parallel"),
        name="bias_lookup",
    )(rel_bias, dist)


def _bias_blocked_kernel(rb_ref, dist_ref, o_ref):
    n = jnp.maximum(dist_ref[...], 0)
    max_exact = N_BUCKETS // 2
    nf = jnp.maximum(n, 1).astype(F32)
    large = max_exact + (jnp.log(nf / max_exact) / math.log(MAX_DISTANCE / max_exact)
                         * (N_BUCKETS - max_exact)).astype(jnp.int32)
    bucket = jnp.where(n < max_exact, n, jnp.minimum(large, N_BUCKETS - 1))
    hits = [bucket == k for k in range(N_BUCKETS)]

    def head(hd, carry):
        out = jnp.zeros(n.shape, F32)
        for k in range(N_BUCKETS):
            out = jnp.where(hits[k], rb_ref[k, hd], out)
        o_ref[hd] = out
        return carry

    lax.fori_loop(0, o_ref.shape[0], head, 0)


def _bias_lookup_blocked(rel_bias, dist, tc):
    r, c = dist.shape
    nh = rel_bias.shape[1]
    return pl.pallas_call(
        _bias_blocked_kernel,
        grid=(c // tc,),
        in_specs=[pl.BlockSpec(memory_space=pltpu.SMEM),
                  pl.BlockSpec((r, tc), lambda j: (0, j))],
        out_specs=pl.BlockSpec((None, nh, r, tc), lambda j: (j, 0, 0, 0)),
        out_shape=jax.ShapeDtypeStruct((c // tc, nh, r, tc), F32),
        compiler_params=_cp("parallel"),
        name="bias_lookup_blocked",
    )(rel_bias, dist)


def _pool_groups(sums, cur, pos, wp_ref, sc_ref, y_ref):
    for g, w in enumerate(POOL_WINDOWS):
        sl = slice(g * POOL_GROUP_DIM, (g + 1) * POOL_GROUP_DIM)
        cnt = jnp.minimum(w, pos + 1).astype(F32)
        d = sums[g] / cnt - cur[:, sl]
        y = jnp.dot(d.astype(BF16), wp_ref[g], preferred_element_type=F32) * sc_ref[:, sl]
        y_ref[:, sl] = y.astype(y_ref.dtype)


def _pool_kernel(prev_ref, cur_ref, wp_ref, sc_ref, y_ref):
    t = pl.program_id(1)
    tt = cur_ref.shape[0]
    cur = cur_ref[...]
    prev = jnp.where(t == 0, 0.0, prev_ref[...])
    ext = jnp.concatenate([prev, cur], axis=0)
    gd = POOL_GROUP_DIM
    s2 = ext + pltpu.roll(ext, 1, 0)
    x4 = s2[:, gd:]
    s4 = x4 + pltpu.roll(x4, 2, 0)
    x8 = s4[:, gd:]
    s8 = x8 + pltpu.roll(x8, 4, 0)
    x16 = s8[:, gd:]
    s16 = x16 + pltpu.roll(x16, 8, 0)
    sums = [s2[16:, :gd], s4[16:, :gd], s8[16:, :gd], s16[16:]]
    pos = t * tt + lax.broadcasted_iota(jnp.int32, (tt, 1), 0)
    _pool_groups(sums, cur, pos, wp_ref, sc_ref, y_ref)


def _pool_prompt(h, col_blk, w_pool, pool_scale, tt=256):
    b, t, _ = h.shape
    wdt = w_pool.shape[0] * POOL_GROUP_DIM
    tt = min(tt, t)
    r = tt // 16
    return pl.pallas_call(
        _pool_kernel,
        name="pool_prompt",
        grid=(b, t // tt),
        in_specs=[pl.BlockSpec((None, 16, wdt), lambda i, j: (i, jnp.maximum(j * r - 1, 0), col_blk)),
                  pl.BlockSpec((None, tt, wdt), lambda i, j: (i, j, col_blk)),
                  pl.BlockSpec(w_pool.shape, lambda i, j: (0, 0, 0)),
                  pl.BlockSpec((1, wdt), lambda i, j: (0, 0))],
        out_specs=pl.BlockSpec((None, tt, wdt), lambda i, j: (i, j, 0)),
        out_shape=jax.ShapeDtypeStruct((b, t, wdt), BF16),
        compiler_params=_cp("parallel", "parallel"),
    )(h, h, w_pool, pool_scale.reshape(1, wdt))


def _pool_dec_kernel(sp_ref, u_ref, wp_ref, sc_ref, y_ref, ns_ref, *, pos):
    u = u_ref[...]
    gd = POOL_GROUP_DIM
    acc = u
    sums = []
    back = 1
    for g, w in enumerate(POOL_WINDOWS):
        while back < w:
            acc = acc + sp_ref[:, POOL_STATE - back, :]
            back += 1
        sums.append(acc[:, g * gd:(g + 1) * gd])
    posv = jnp.full((u.shape[0], 1), pos, jnp.int32)
    _pool_groups(sums, u, posv, wp_ref, sc_ref, y_ref)
    for j in range(POOL_STATE - 1):
        ns_ref[:, j, :] = sp_ref[:, j + 1, :]
    ns_ref[:, POOL_STATE - 1, :] = u


def _pool_dec(state, u, w_pool, pool_scale, pos):
    b, wdt = u.shape
    return pl.pallas_call(
        functools.partial(_pool_dec_kernel, pos=pos),
        out_shape=[jax.ShapeDtypeStruct((b, wdt), BF16), jax.ShapeDtypeStruct(state.shape, F32)],
        compiler_params=pltpu.CompilerParams(vmem_limit_bytes=VMEM_LIMIT),
    )(state, u, w_pool, pool_scale.reshape(1, wdt))


def _sconv_kernel(px_ref, pc_ref, x_ref, b_ref, c_ref, cw_ref, y_ref, st_ref):
    t = pl.program_id(1)
    tt = x_ref.shape[0]
    v = c_ref[...] * x_ref[...]
    pv = jnp.where(t == 0, 0.0, pc_ref[...] * px_ref[...])
    row = lax.broadcasted_iota(jnp.int32, (tt, 1), 0)
    s1, s2 = _shift_rows(v, pv, row)
    cw = cw_ref[...]
    conv = s2 * cw[0:1] + s1 * cw[1:2] + v * cw[2:3]
    y_ref[...] = (b_ref[...] * conv).astype(y_ref.dtype)
    st_ref[...] = v[tt - 8:]


def _sconv_prompt(h, conv_w, tt=256):
    b, t, _ = h.shape
    wdt = conv_w.shape[1]
    tt = min(tt, t)
    r = tt // 8

    def prev(blk):
        return pl.BlockSpec((None, 8, wdt), lambda i, j: (i, jnp.maximum(j * r - 1, 0), blk))

    def cur(blk):
        return pl.BlockSpec((None, tt, wdt), lambda i, j: (i, j, blk))

    return pl.pallas_call(
        _sconv_kernel,
        name="sconv_prompt",
        grid=(b, t // tt),
        in_specs=[prev(0), prev(2), cur(0), cur(1), cur(2), pl.BlockSpec((3, wdt), lambda i, j: (0, 0))],
        out_specs=[pl.BlockSpec((None, tt, wdt), lambda i, j: (i, j, 0)),
                   pl.BlockSpec((None, 8, wdt), lambda i, j: (i, 0, 0))],
        out_shape=[jax.ShapeDtypeStruct((b, t, wdt), BF16), jax.ShapeDtypeStruct((b, 8, wdt), F32)],
        compiler_params=_cp("parallel", "arbitrary"),
    )(h, h, h, h, h, conv_w)


def _sconv_dec_kernel(st_ref, x_ref, b_ref, c_ref, cw_ref, y_ref, ns_ref):
    v = c_ref[...] * x_ref[...]
    cw = cw_ref[...]
    conv = st_ref[:, 0, :] * cw[0:1] + st_ref[:, 1, :] * cw[1:2] + v * cw[2:3]
    y_ref[...] = (b_ref[...] * conv).astype(y_ref.dtype)
    ns_ref[:, 0, :] = st_ref[:, 1, :]
    ns_ref[:, 1, :] = v


def _sconv_dec(state, h, conv_w):
    b = h.shape[0]
    wdt = conv_w.shape[1]

    def col(blk):
        return pl.BlockSpec((b, wdt), lambda i: (0, blk))

    return pl.pallas_call(
        _sconv_dec_kernel,
        grid=(1,),
        in_specs=[pl.BlockSpec(state.shape, lambda i: (0, 0, 0)), col(0), col(1), col(2),
                  pl.BlockSpec((3, wdt), lambda i: (0, 0))],
        out_specs=[pl.BlockSpec((b, wdt), lambda i: (0, 0)), pl.BlockSpec(state.shape, lambda i: (0, 0, 0))],
        out_shape=[jax.ShapeDtypeStruct((b, wdt), BF16), jax.ShapeDtypeStruct(state.shape, F32)],
        compiler_params=_cp("arbitrary"),
    )(state, h, h, h, conv_w)


def _softplus(z):
    return jnp.maximum(z, 0.0) + jnp.log(1.0 + jnp.exp(-jnp.abs(z)))


def _suffix_sums(x, upper):
    r = x.shape[0]
    hi = x.astype(BF16)
    lo = (x - hi.astype(F32)).astype(BF16)
    s = jnp.dot(jnp.concatenate([hi, lo], axis=0), upper, preferred_element_type=F32)
    return s[:r] + s[r:]


SB_HEADS_PER_STEP = 4


def _sb_kernel(q_ref, k_ref, v_ref, o_ref, ab_sc, r_sc, acc_sc, *, tq):
    i = pl.program_id(2)
    scale = HEAD_DIM ** -0.5
    nhs = SB_HEADS_PER_STEP
    hsl = [slice(j * HEAD_DIM, (j + 1) * HEAD_DIM) for j in range(nhs)]
    qs = [q_ref[:, sl].astype(BF16) for sl in hsl]
    rj = lax.broadcasted_iota(jnp.int32, (tq, tq), 0)
    cj = lax.broadcasted_iota(jnp.int32, (tq, tq), 1)
    upper = jnp.where(rj > cj, 1.0, 0.0).astype(BF16)

    before = jnp.concatenate([cj < rj] * nhs, axis=0)

    def logits(c):
        off = pl.multiple_of(c * tq, tq)
        return jnp.concatenate([_nt(qs[j], k_ref[pl.ds(off, tq), hsl[j]].astype(BF16)) for j in range(nhs)],
                               axis=0) * scale

    def weights(z, r_run, diag):
        sp = _softplus(z)
        if diag:
            sp = jnp.where(before, sp, 0.0)
        a = jnp.exp(z - sp - _suffix_sums(sp, upper) - r_run)
        if diag:
            a = jnp.where(before, a, 0.0)
        return a.astype(BF16), r_run + jnp.sum(sp, axis=1, keepdims=True)

    def accumulate(c):
        off = pl.multiple_of(c * tq, tq)
        for j in range(nhs):
            acc_sc[j] += jnp.dot(ab_sc[j * tq:(j + 1) * tq, :], v_ref[pl.ds(off, tq), hsl[j]].astype(BF16),
                                 preferred_element_type=F32)

    acc_sc[...] = jnp.zeros_like(acc_sc)
    ab_sc[...], r_sc[...] = weights(logits(i), jnp.zeros((nhs * tq, 1), F32), True)

    def body(s, carry):
        c = i - s
        accumulate(c + 1)
        ab_sc[...], r_sc[...] = weights(logits(c), r_sc[...], False)
        return carry

    lax.fori_loop(1, i + 1, body, 0)
    accumulate(0)
    for j in range(nhs):
        o_ref[:, hsl[j]] = acc_sc[j].astype(o_ref.dtype)


def _sb_prompt(q, k, v, tq=256):
    b, t, width = q.shape
    tq = min(tq, t)
    wdt = SB_HEADS_PER_STEP * HEAD_DIM
    return pl.pallas_call(
        functools.partial(_sb_kernel, tq=tq),
        grid=(b, width // wdt, t // tq),
        in_specs=[pl.BlockSpec((None, tq, wdt), lambda bi, hi, i: (bi, i, hi)),
                  pl.BlockSpec((None, t, wdt), lambda bi, hi, i: (bi, 0, hi)),
                  pl.BlockSpec((None, t, wdt), lambda bi, hi, i: (bi, 0, hi))],
        out_specs=pl.BlockSpec((None, tq, wdt), lambda bi, hi, i: (bi, i, hi)),
        out_shape=jax.ShapeDtypeStruct((b, t, width), BF16),
        scratch_shapes=[pltpu.VMEM((SB_HEADS_PER_STEP * tq, tq), BF16), pltpu.VMEM((SB_HEADS_PER_STEP * tq, 1), F32),
                        pltpu.VMEM((SB_HEADS_PER_STEP, tq, HEAD_DIM), F32)],
        compiler_params=_cp("parallel", "parallel", "parallel"),
        name="sb_prompt",
    )(q, k, v)


SB_PAGES_PER_STEP = 2


def _sb_dec_kernel(pt_ref, q_ref, *refs):
    npg = SB_PAGES_PER_STEP
    k_refs, v_refs = refs[:npg], refs[npg:2 * npg]
    o_ref, z_sc, r_sc, acc_sc = refs[2 * npg:]
    c = pl.program_id(1)
    scale = HEAD_DIM ** -0.5
    nh = q_ref.shape[0]

    @pl.when(c == 0)
    def _():
        z_sc[...] = jnp.zeros_like(z_sc)
        r_sc[...] = jnp.zeros_like(r_sc)
        acc_sc[...] = jnp.zeros_like(acc_sc)

    rj = lax.broadcasted_iota(jnp.int32, (PAGE_SIZE, PAGE_SIZE), 0)
    cj = lax.broadcasted_iota(jnp.int32, (PAGE_SIZE, PAGE_SIZE), 1)
    upper = jnp.where(rj > cj, 1.0, 0.0).astype(BF16)
    hrow = lax.broadcasted_iota(jnp.int32, (nh, HEAD_DIM), 0)

    def heads_on_lanes(ref):
        return jnp.concatenate([ref[pl.ds(j, PAGE_SIZE, stride=nh), :].astype(BF16) for j in range(nh)], axis=1)

    live = c > 0
    r_run = r_sc[...]
    weights = []
    for r in range(npg):
        z = z_sc[r]
        sp = jnp.where(live, _softplus(z), 0.0)
        weights.append(jnp.where(live, jnp.exp(z - sp - _suffix_sums(sp, upper) - r_run), 0.0).astype(BF16))
        r_run = r_run + jnp.sum(sp, axis=1, keepdims=True)
    r_sc[...] = r_run

    q = q_ref[...]
    q_bd = jnp.concatenate([jnp.where(hrow == j, q, 0.0) for j in range(nh)], axis=1).astype(BF16)
    for r in range(npg):
        z_sc[r] = _nt(q_bd, heads_on_lanes(k_refs[r])) * scale

    upd = jnp.zeros((nh, HEAD_DIM), F32)
    for r in range(npg):
        res = jnp.dot(weights[r], heads_on_lanes(v_refs[r]), preferred_element_type=F32)
        for j in range(nh):
            upd = upd + jnp.where(hrow == j, res[:, j * HEAD_DIM:(j + 1) * HEAD_DIM], 0.0)
    acc_sc[...] += upd

    @pl.when(c == pl.num_programs(1) - 1)
    def _():
        o_ref[...] = acc_sc[...].astype(o_ref.dtype)


def _sb_dec(q, k_pool, v_pool, layer, page_table):
    b, nh, _ = q.shape
    n_pages = page_table.shape[1]
    npg = SB_PAGES_PER_STEP
    assert n_pages % npg == 0
    n_steps = n_pages // npg
    rows = PAGE_SIZE * nh
    k_pool = k_pool.reshape(k_pool.shape[:2] + (rows, HEAD_DIM))
    v_pool = v_pool.reshape(v_pool.shape[:2] + (rows, HEAD_DIM))

    def k_spec(r):
        return pl.BlockSpec((None, None, rows, HEAD_DIM), lambda bi, c, pt: (
            layer, pt[bi, n_pages - 1 - (npg * jnp.minimum(c, n_steps - 1) + r)], 0, 0))

    def v_spec(r):
        return pl.BlockSpec((None, None, rows, HEAD_DIM), lambda bi, c, pt: (
            layer, pt[bi, n_pages - 1 - (npg * jnp.maximum(c - 1, 0) + r)], 0, 0))

    grid_spec = pltpu.PrefetchScalarGridSpec(
        num_scalar_prefetch=1,
        grid=(b, n_steps + 1),
        in_specs=[pl.BlockSpec((None, nh, HEAD_DIM), lambda bi, c, pt: (bi, 0, 0))]
        + [k_spec(r) for r in range(npg)] + [v_spec(r) for r in range(npg)],
        out_specs=pl.BlockSpec((None, nh, HEAD_DIM), lambda bi, c, pt: (bi, 0, 0)),
        scratch_shapes=[pltpu.VMEM((npg, nh, PAGE_SIZE), F32), pltpu.VMEM((nh, 1), F32),
                        pltpu.VMEM((nh, HEAD_DIM), F32)],
    )
    return pl.pallas_call(
        _sb_dec_kernel,
        grid_spec=grid_spec,
        out_shape=jax.ShapeDtypeStruct((b, nh, HEAD_DIM), BF16),
        compiler_params=_cp("parallel", "arbitrary"),
        name="sb_decode",
    )(page_table, q, *([k_pool] * npg), *([v_pool] * npg))


def _chunk_sums(x, w0, w1):
    x3 = x.reshape(x.shape[0] // CMP_STRIDE, CMP_STRIDE, x.shape[1])
    return jnp.sum(x3 * w0[None], axis=1), jnp.sum(x3 * w1[None], axis=1)


def _compress_kernel(kc_ref, vc_ref, wpos_ref, wlin_ref, ko_ref, vo_ref):
    nchunk = kc_ref.shape[0] // CMP_STRIDE
    for idx, (src, dst) in enumerate(((kc_ref, ko_ref), (vc_ref, vo_ref))):
        a, bsum = _chunk_sums(src[...], wpos_ref[idx, 0:CMP_STRIDE, :], wpos_ref[idx, CMP_STRIDE:, :])
        pooled = a + pltpu.roll(bsum, nchunk - 1, 0)
        for h in range(NSA_KV_HEADS):
            sl = slice(h * HEAD_DIM, (h + 1) * HEAD_DIM)
            dst[:, sl] = jnp.dot(pooled[:, sl].astype(BF16), wlin_ref[idx, h], preferred_element_type=F32)


def _compress_prompt(h, wpos, wlin):
    b, t, _ = h.shape
    kvw = NSA_KV_HEADS * HEAD_DIM
    nchunk = t // CMP_STRIDE
    out = jax.ShapeDtypeStruct((b, nchunk, kvw), F32)
    ospec = pl.BlockSpec((None, nchunk, kvw), lambda i: (i, 0, 0))
    return pl.pallas_call(
        _compress_kernel,
        name="compress_prompt",
        grid=(b,),
        in_specs=[pl.BlockSpec((None, t, kvw), lambda i: (i, 0, (E_KC - E_R) // kvw)),
                  pl.BlockSpec((None, t, kvw), lambda i: (i, 0, (E_VC - E_R) // kvw)),
                  pl.BlockSpec(wpos.shape, lambda i: (0, 0, 0)),
                  pl.BlockSpec(wlin.shape, lambda i: (0, 0, 0, 0))],
        out_specs=[ospec, ospec],
        out_shape=[out, out],
        compiler_params=_cp("parallel"),
    )(h, h, wpos, wlin)


def _topk_keep(score, k_top):
    srow = lax.broadcasted_iota(jnp.int32, score.shape, 0)
    rank = jnp.zeros(score.shape, jnp.int32)
    for s2 in range(score.shape[0]):
        row = score[s2:s2 + 1, :]
        rank = rank + jnp.where(row > score, 1, jnp.where(row == score, jnp.where(srow > s2, 1, 0), 0))
    return jnp.where(rank < k_top, jnp.where(score > 0.5 * NEG_INF, 1.0, 0.0), 0.0)


def _flash_init(m_sc, l_sc, acc_sc):
    m_sc[...] = jnp.full(m_sc.shape, NEG_INF, F32)
    l_sc[...] = jnp.zeros(l_sc.shape, F32)
    acc_sc[...] = jnp.zeros(acc_sc.shape, F32)


def _flash_chunk(q_all, k, v, bias_fn, mask, m_sc, l_sc, acc_sc):
    ng = NSA_GROUP
    r = q_all.shape[0] // ng
    s_all = _nt(k.astype(BF16), q_all) * (HEAD_DIM ** -0.5)
    v_t = v.T.astype(BF16)
    m_prev = m_sc[...]
    m_parts, p_parts = [], []
    for g in range(ng):
        sl = slice(g * r, (g + 1) * r)
        s = s_all[:, sl] + bias_fn(g)
        if mask is not None:
            s = jnp.where(mask, s, NEG_INF)
        mn = jnp.maximum(m_prev[:, sl], jnp.max(s, axis=0, keepdims=True))
        p_parts.append(jnp.exp(s - mn))
        m_parts.append(mn)
    m_new = jnp.concatenate(m_parts, axis=1)
    p = jnp.concatenate(p_parts, axis=1)
    alpha = jnp.exp(m_prev - m_new)
    l_sc[...] = alpha * l_sc[...] + jnp.sum(p, axis=0, keepdims=True)
    acc_sc[...] = alpha * acc_sc[...] + jnp.dot(v_t, p.astype(BF16), preferred_element_type=F32)
    m_sc[...] = m_new


def _nsa_kernel(b31_ref, q_ref, g_ref, kc_ref, vc_ref, ks_ref, vs_ref, kvw_ref, tabc_ref, tabd_ref, o_ref,
                m_sc, l_sc, acc_sc, oc_sc, os_sc, keep_sc):
    i = pl.program_id(1)
    qb = q_ref.shape[0]
    t_len = ks_ref.shape[0]
    nc = kc_ref.shape[0]
    n_cmp = (t_len - CMP_BLOCK) // CMP_STRIDE + 1
    n_slc = t_len // SEL_BLOCK
    k_top = min(N_SEL, n_slc)
    gsz = NSA_GROUP
    scale = HEAD_DIM ** -0.5
    keyi = lax.broadcasted_iota(jnp.int32, (qb, qb), 0)
    qi = lax.broadcasted_iota(jnp.int32, (qb, qb), 1)
    causal = keyi <= qi
    wedge = keyi >= qi
    first_half = keyi < SEL_BLOCK
    pos_row = i * qb + lax.broadcasted_iota(jnp.int32, (1, qb), 1)
    sig_t = jax.nn.sigmoid(g_ref[...]).T
    osb = lax.broadcasted_iota(jnp.int32, (n_slc, nc), 0)
    on = lax.broadcasted_iota(jnp.int32, (n_slc, nc), 1)
    c_start = on * CMP_STRIDE
    overlap_t = jnp.where((c_start < osb * SEL_BLOCK + SEL_BLOCK) & (c_start + CMP_BLOCK - 1 >= osb * SEL_BLOCK)
                          & (on < n_cmp), 1.0, 0.0).astype(BF16)
    nrow = lax.broadcasted_iota(jnp.int32, (nc, qb), 0)
    cmask = ((nrow * CMP_STRIDE + CMP_BLOCK - 1) <= pos_row) & (nrow < n_cmp)
    srow = lax.broadcasted_iota(jnp.int32, (n_slc, qb), 0)
    cur = pos_row >> SEL_SHIFT
    forced = (srow == 0) | (srow == cur) | (srow == cur - 1)
    future = srow > cur

    for h in range(NSA_KV_HEADS):
        hs = slice(h * HEAD_DIM, (h + 1) * HEAD_DIM)
        g0 = h * gsz
        q_all = jnp.concatenate(
            [q_ref[:, (g0 + g) * HEAD_DIM:(g0 + g + 1) * HEAD_DIM] for g in range(gsz)], axis=0).astype(BF16)

        s_all = _nt(kc_ref[:, hs].astype(BF16), q_all) * scale
        vc_t = vc_ref[:, hs].T.astype(BF16)
        pcs = jnp.zeros((nc, qb), F32)
        pc_parts = []
        for g in range(gsz):
            s = s_all[:, g * qb:(g + 1) * qb] + tabc_ref[g0 + g]
            s = jnp.where(cmask, s, NEG_INF)
            s = s - jnp.max(s, axis=0, keepdims=True)
            e = jnp.exp(s)
            pc = jnp.where(cmask, e / jnp.sum(e, axis=0, keepdims=True), 0.0)
            pcs = pcs + pc
            pc_parts.append(pc.astype(BF16))
        oc_sc[...] = jnp.dot(vc_t, jnp.concatenate(pc_parts, axis=1), preferred_element_type=F32)
        imp = sum(jnp.dot(overlap_t, piece, preferred_element_type=F32) for piece in _split3(pcs))
        score = jnp.where(future, NEG_INF, jnp.where(forced, BIG, imp))
        keep_sc[...] = _topk_keep(score, k_top)

        def sel_mask(c, nk=1):
            parts = []
            for u in range(nk):
                first = keep_sc[pl.ds(2 * (c + u), 1), :]
                second = keep_sc[pl.ds(2 * (c + u) + 1, 1), :]
                parts.append(jnp.where(first_half, first, second))
            return jnp.concatenate(parts, axis=0) > 0.5

        def chunk(k_ref, v_ref, c, ksl, vsl, bias_fn, mask, nk=1):
            off = pl.multiple_of(c * qb, qb)
            _flash_chunk(q_all, k_ref[pl.ds(off, nk * qb), ksl], v_ref[pl.ds(off, nk * qb), vsl], bias_fn, mask,
                         m_sc, l_sc, acc_sc)

        tab0 = lambda g: tabd_ref[g0 + g, 0]
        tab1 = lambda g: tabd_ref[g0 + g, 1]
        far = lambda g: b31_ref[g0 + g]

        _flash_init(m_sc, l_sc, acc_sc)
        chunk(ks_ref, vs_ref, i, hs, hs, tab0, sel_mask(i) & causal)

        @pl.when(i >= 1)
        def _():
            chunk(ks_ref, vs_ref, i - 1, hs, hs, tab1, sel_mask(i - 1))

        n_far = jnp.maximum(i - 1, 0)

        def far_body(pair, carry):
            chunk(ks_ref, vs_ref, 2 * pair, hs, hs, far, sel_mask(2 * pair, 2), 2)
            return carry

        lax.fori_loop(0, n_far // 2, far_body, 0)

        @pl.when(n_far % 2 == 1)
        def _():
            chunk(ks_ref, vs_ref, n_far - 1, hs, hs, far, sel_mask(n_far - 1))

        os_sc[...] = acc_sc[...] / l_sc[...]

        kw_sl = hs
        vw_sl = slice(NSA_KV_HEADS * HEAD_DIM + h * HEAD_DIM, NSA_KV_HEADS * HEAD_DIM + (h + 1) * HEAD_DIM)
        _flash_init(m_sc, l_sc, acc_sc)
        chunk(kvw_ref, kvw_ref, i, kw_sl, vw_sl, tab0, causal)

        @pl.when(i >= 1)
        def _():
            chunk(kvw_ref, kvw_ref, i - 1, kw_sl, vw_sl, tab1, None)

        @pl.when(i >= 3)
        def _():
            chunk(kvw_ref, kvw_ref, i - 3, kw_sl, vw_sl, far, None, 2)

        @pl.when(i == 2)
        def _():
            chunk(kvw_ref, kvw_ref, 0, kw_sl, vw_sl, far, None)

        @pl.when(i >= 4)
        def _():
            chunk(kvw_ref, kvw_ref, i - 4, kw_sl, vw_sl, far, wedge)

        ow = acc_sc[...] / l_sc[...]

        def gate(j):
            return jnp.concatenate([sig_t[(g0 + g) * 3 + j:(g0 + g) * 3 + j + 1, :] for g in range(gsz)], axis=1)

        y_t = gate(0) * oc_sc[...] + gate(1) * os_sc[...] + gate(2) * ow
        for g in range(gsz):
            o_ref[:, (g0 + g) * HEAD_DIM:(g0 + g + 1) * HEAD_DIM] = y_t[:, g * qb:(g + 1) * qb].T.astype(o_ref.dtype)


def _nsa_prompt(hq, h, kcmp, vcmp, tabc, tabd, b31, qb=128):
    b, t, _ = h.shape
    assert qb == HEAD_DIM == 2 * SEL_BLOCK and WINDOW == 4 * qb and t % qb == 0
    qw = NSA_HEADS * HEAD_DIM
    kvw = NSA_KV_HEADS * HEAD_DIM
    nchunk = kcmp.shape[1]
    gq = NSA_GROUP * qb
    return pl.pallas_call(
        _nsa_kernel,
        grid=(b, t // qb),
        in_specs=[pl.BlockSpec(memory_space=pltpu.SMEM),
                  pl.BlockSpec((None, qb, qw), lambda bi, i: (bi, i, 0)),
                  pl.BlockSpec((None, qb, LANES), lambda bi, i: (bi, i, (E_G - E_R) // LANES)),
                  pl.BlockSpec((None, nchunk, kvw), lambda bi, i: (bi, 0, 0)),
                  pl.BlockSpec((None, nchunk, kvw), lambda bi, i: (bi, 0, 0)),
                  pl.BlockSpec((None, t, kvw), lambda bi, i: (bi, 0, (E_KS - E_R) // kvw)),
                  pl.BlockSpec((None, t, kvw), lambda bi, i: (bi, 0, (E_VS - E_R) // kvw)),
                  pl.BlockSpec((None, t, 2 * kvw), lambda bi, i: (bi, 0, (E_KVW - E_R) // (2 * kvw))),
                  pl.BlockSpec((None, NSA_HEADS, nchunk, qb), lambda bi, i: (i, 0, 0, 0)),
                  pl.BlockSpec((NSA_HEADS, 2, qb, qb), lambda bi, i: (0, 0, 0, 0))],
        out_specs=pl.BlockSpec((None, qb, qw), lambda bi, i: (bi, i, 0)),
        out_shape=jax.ShapeDtypeStruct((b, t, qw), BF16),
        scratch_shapes=[pltpu.VMEM((1, gq), F32), pltpu.VMEM((1, gq), F32),
                        pltpu.VMEM((HEAD_DIM, gq), F32), pltpu.VMEM((HEAD_DIM, gq), F32),
                        pltpu.VMEM((HEAD_DIM, gq), F32), pltpu.VMEM((t // SEL_BLOCK, qb), F32)],
        compiler_params=_cp("parallel", "parallel"),
        name="nsa_prompt",
    )(b31, hq, h, kcmp, vcmp, h, h, h, tabc, tabd)


def _cmp_dec_kernel(pt_ref, *refs, n_pg):
    k_refs = refs[:n_pg]
    v_refs = refs[n_pg:2 * n_pg]
    wpos_ref = refs[2 * n_pg]
    ak_ref, bk_ref, av_ref, bv_ref = refs[2 * n_pg + 1:]
    per = PAGE_SIZE // CMP_STRIDE
    for idx, (srcs, a_ref, b_ref) in enumerate(((k_refs, ak_ref, bk_ref), (v_refs, av_ref, bv_ref))):
        for r in range(n_pg):
            for h in range(NSA_KV_HEADS):
                sl = slice(h * HEAD_DIM, (h + 1) * HEAD_DIM)
                x = srcs[r][pl.ds(h, PAGE_SIZE, stride=NSA_KV_HEADS), :]
                a, bsum = _chunk_sums(x, wpos_ref[idx, 0:CMP_STRIDE, sl], wpos_ref[idx, CMP_STRIDE:, sl])
                a_ref[r * per:(r + 1) * per, sl] = a
                b_ref[r * per:(r + 1) * per, sl] = bsum


def _cmp_dec(k_pool, v_pool, layer, page_table, wpos, n_pg=8):
    b, n_pages = page_table.shape
    kvw = NSA_KV_HEADS * HEAD_DIM
    per = PAGE_SIZE // CMP_STRIDE
    rows = PAGE_SIZE * NSA_KV_HEADS
    k_pool = k_pool.reshape(k_pool.shape[:2] + (rows, HEAD_DIM))
    v_pool = v_pool.reshape(v_pool.shape[:2] + (rows, HEAD_DIM))

    def page(r):
        return pl.BlockSpec((None, None, rows, HEAD_DIM), lambda bi, c, pt: (layer, pt[bi, c * n_pg + r], 0, 0))

    ospec = pl.BlockSpec((None, n_pg * per, kvw), lambda bi, c, pt: (bi, c, 0))
    out = jax.ShapeDtypeStruct((b, n_pages * per, kvw), F32)
    grid_spec = pltpu.PrefetchScalarGridSpec(
        num_scalar_prefetch=1,
        grid=(b, n_pages // n_pg),
        in_specs=[page(r) for r in range(n_pg)] * 2 + [pl.BlockSpec(wpos.shape, lambda bi, c, pt: (0, 0, 0))],
        out_specs=[ospec] * 4,
    )
    return pl.pallas_call(
        functools.partial(_cmp_dec_kernel, n_pg=n_pg),
        grid_spec=grid_spec,
        out_shape=[out] * 4,
        compiler_params=_cp("parallel", "parallel"),
        name="cmp_decode",
    )(page_table, *([k_pool] * n_pg), *([v_pool] * n_pg), wpos)


def _nsa_dec_cmp_kernel(q_ref, ak_ref, bk_ref, av_ref, bv_ref, wlin_ref, bias_ref, oc_ref, idx_ref,
                        *, pos, n_cmp, n_slc):
    nrow = ak_ref.shape[0]
    sw = idx_ref.shape[-1]
    lane_n = lax.broadcasted_iota(jnp.int32, (1, nrow), 1)
    cmask = ((lane_n * CMP_STRIDE + CMP_BLOCK - 1) <= pos) & (lane_n < n_cmp)
    on = lax.broadcasted_iota(jnp.int32, (nrow, sw), 0)
    osb = lax.broadcasted_iota(jnp.int32, (nrow, sw), 1)
    c_start = on * CMP_STRIDE
    overlap = jnp.where((c_start < osb * SEL_BLOCK + SEL_BLOCK) & (c_start + CMP_BLOCK - 1 >= osb * SEL_BLOCK)
                        & (on < n_cmp) & (osb < n_slc), 1.0, 0.0).astype(BF16)
    pk = ak_ref[...] + pltpu.roll(bk_ref[...], nrow - 1, 0)
    pv = av_ref[...] + pltpu.roll(bv_ref[...], nrow - 1, 0)
    lane_s = lax.broadcasted_iota(jnp.int32, (1, sw), 1)
    ri = lax.broadcasted_iota(jnp.int32, (sw, sw), 0)
    ci = lax.broadcasted_iota(jnp.int32, (sw, sw), 1)
    k_top = min(N_SEL, n_slc)
    cur = pos // SEL_BLOCK
    for h in range(NSA_KV_HEADS):
        sl = slice(h * HEAD_DIM, (h + 1) * HEAD_DIM)
        kc = jnp.dot(pk[:, sl].astype(BF16), wlin_ref[0, h], preferred_element_type=F32).astype(BF16)
        vc = jnp.dot(pv[:, sl].astype(BF16), wlin_ref[1, h], preferred_element_type=F32).astype(BF16)
        q = q_ref[h].astype(BF16)
        s = _nt(q, kc) * (HEAD_DIM ** -0.5) + bias_ref[h * NSA_GROUP:(h + 1) * NSA_GROUP]
        s = jnp.where(cmask, s, NEG_INF)
        s = s - jnp.max(s, axis=-1, keepdims=True)
        e = jnp.exp(s)
        pc = jnp.where(cmask, e / jnp.sum(e, axis=-1, keepdims=True), 0.0)
        oc_ref[h] = jnp.dot(pc.astype(BF16), vc, preferred_element_type=F32)
        pcs = jnp.sum(pc, axis=0, keepdims=True)
        hi, mid, lo = _split3(jnp.broadcast_to(pcs, (8, nrow)))
        imp = (jnp.dot(hi, overlap, preferred_element_type=F32) + jnp.dot(mid, overlap, preferred_element_type=F32)
               + jnp.dot(lo, overlap, preferred_element_type=F32))[0:1]
        forced = (lane_s == 0) | (lane_s == cur) | (lane_s == cur - 1)
        score = jnp.where(lane_s > cur, NEG_INF, jnp.where(forced, BIG, imp))
        score = jnp.where(lane_s < n_slc, score, -jnp.inf)
        rowm = jnp.broadcast_to(score, (sw, sw))
        colm = rowm.T
        beats = jnp.where(colm > rowm, 1, jnp.where(colm == rowm, jnp.where(ri < ci, 1, 0), 0))
        beats = jnp.where(ri < n_slc, beats, 0)
        rank = jnp.sum(beats, axis=0, keepdims=True)
        keep = (rank < k_top) & (score > 0.5 * NEG_INF) & (lane_s < n_slc)
        out = jnp.full((1, sw), -1, jnp.int32)
        for r in range(k_top):
            hit = keep & (rank == r)
            val = jnp.sum(jnp.where(hit, lane_s + 1, 0), axis=1, keepdims=True) - 1
            out = jnp.where(lane_s == r, val, out)
        idx_ref[h] = out


def _nsa_dec_cmp(q, ak, bk, av, bv, wlin, bias_c, pos, n_cmp, n_slc):
    b = q.shape[0]
    nrow = ak.shape[1]
    kvw = ak.shape[2]
    sw = -(-n_slc // LANES) * LANES
    part = pl.BlockSpec((None, nrow, kvw), lambda i: (i, 0, 0))
    kern = functools.partial(_nsa_dec_cmp_kernel, pos=pos, n_cmp=n_cmp, n_slc=n_slc)
    return pl.pallas_call(
        kern,
        name="nsa_dec_cmp",
        grid=(b,),
        in_specs=[pl.BlockSpec((None,) + q.shape[1:], lambda i: (i, 0, 0, 0)), part, part, part, part,
                  pl.BlockSpec(wlin.shape, lambda i: (0, 0, 0, 0)),
                  pl.BlockSpec(bias_c.shape, lambda i: (0, 0))],
        out_specs=[pl.BlockSpec((None,) + q.shape[1:], lambda i: (i, 0, 0, 0)),
                   pl.BlockSpec((None, NSA_KV_HEADS, 1, sw), lambda i: (i, 0, 0, 0))],
        out_shape=[jax.ShapeDtypeStruct(q.shape, F32), jax.ShapeDtypeStruct((b, NSA_KV_HEADS, 1, sw), jnp.int32)],
        compiler_params=_cp("parallel"),
    )(q, ak, bk, av, bv, wlin, bias_c)


def _bucket_bias(dist, rb):
    n = jnp.maximum(dist, 0)
    max_exact = N_BUCKETS // 2
    nf = jnp.maximum(n, 1).astype(F32)
    large = max_exact + (jnp.log(nf / max_exact) / math.log(MAX_DISTANCE / max_exact)
                         * (N_BUCKETS - max_exact)).astype(jnp.int32)
    bucket = jnp.where(n < max_exact, n, jnp.minimum(large, N_BUCKETS - 1))
    out = jnp.zeros((rb.shape[0], dist.shape[1]), F32)
    for k in range(N_BUCKETS):
        out = jnp.where(bucket == k, rb[:, k:k + 1], out)
    return out


SEL_BLOCKS_PER_STEP = 4


def _nsa_dec_sel_kernel(pg_ref, hf_ref, blk_ref, q_ref, kn_ref, vn_ref, rb_ref, *refs, pos):
    nb = SEL_BLOCKS_PER_STEP
    kv_refs = refs[:4 * nb]
    o_ref, m_sc, l_sc, acc_sc = refs[4 * nb:]
    bi = pl.program_id(0)
    j = pl.program_id(1)
    k_top = pl.num_programs(1) * nb
    scale = HEAD_DIM ** -0.5
    lane = lax.broadcasted_iota(jnp.int32, (1, SEL_BLOCK), 1)

    @pl.when(j == 0)
    def _():
        for h in range(NSA_KV_HEADS):
            q = q_ref[h]
            rb = rb_ref[h]
            s = jnp.sum(q.astype(BF16).astype(F32) * kn_ref[h:h + 1, :].astype(BF16).astype(F32),
                        axis=-1, keepdims=True) * scale + rb[:, 0:1]
            m_sc[h] = s
            l_sc[h] = jnp.ones_like(s)
            acc_sc[h] = jnp.broadcast_to(vn_ref[h:h + 1, :].astype(BF16).astype(F32), (q.shape[0], HEAD_DIM))

    for r, h in [(r, h) for r in range(nb) for h in range(NSA_KV_HEADS)]:
        k_ref, v_ref = kv_refs[4 * r + h], kv_refs[4 * r + 2 + h]
        blk = blk_ref[bi, h * k_top + j * nb + r]

        @pl.when(blk >= 0)
        def _(h=h, k_ref=k_ref, v_ref=v_ref, blk=blk):
            q = q_ref[h].astype(BF16)
            k = k_ref[:, h, :].astype(BF16)
            v = v_ref[:, h, :].astype(BF16)
            tok = blk * SEL_BLOCK + lane
            ok = tok <= pos
            s = _nt(q, k) * scale + _bucket_bias(pos - tok, rb_ref[h])
            s = jnp.where(ok, s, NEG_INF)
            m_prev = m_sc[h]
            m_new = jnp.maximum(m_prev, jnp.max(s, axis=-1, keepdims=True))
            alpha = jnp.exp(m_prev - m_new)
            p = jnp.where(ok, jnp.exp(s - m_new), 0.0)
            l_sc[h] = alpha * l_sc[h] + jnp.sum(p, axis=-1, keepdims=True)
            acc_sc[h] = alpha * acc_sc[h] + jnp.dot(p.astype(BF16), v, preferred_element_type=F32)
            m_sc[h] = m_new

    @pl.when(j == pl.num_programs(1) - 1)
    def _():
        for h in range(NSA_KV_HEADS):
            o_ref[h] = acc_sc[h] / l_sc[h]


def _nsa_dec_sel(q, k_new, v_new, rb, k_pool, v_pool, layer, pages, halves, blocks, pos):
    b = q.shape[0]
    k_top = blocks.shape[-1]
    nb = SEL_BLOCKS_PER_STEP
    assert k_top % nb == 0
    pages, halves, blocks = (a.reshape(b, NSA_KV_HEADS * k_top) for a in (pages, halves, blocks))

    def blkspec(r, h):
        return pl.BlockSpec((None, None, SEL_BLOCK, NSA_KV_HEADS, HEAD_DIM), lambda bi, j, pg, hf, bl: (
            layer, pg[bi, h * k_top + j * nb + r], hf[bi, h * k_top + j * nb + r], 0, 0))

    qspec = pl.BlockSpec((None,) + q.shape[1:], lambda bi, j, pg, hf, bl: (bi, 0, 0, 0))
    nspec = pl.BlockSpec((None, NSA_KV_HEADS, HEAD_DIM), lambda bi, j, pg, hf, bl: (bi, 0, 0))
    kv_specs = [blkspec(r, h) for r in range(nb) for _ in range(2) for h in range(NSA_KV_HEADS)]
    kv_args = [pool for _ in range(nb) for pool in (k_pool, k_pool, v_pool, v_pool)]
    grid_spec = pltpu.PrefetchScalarGridSpec(
        num_scalar_prefetch=3,
        grid=(b, k_top // nb),
        in_specs=[qspec, nspec, nspec, pl.BlockSpec(rb.shape, lambda bi, j, pg, hf, bl: (0, 0, 0))] + kv_specs,
        out_specs=qspec,
        scratch_shapes=[pltpu.VMEM((NSA_KV_HEADS, NSA_GROUP, 1), F32), pltpu.VMEM((NSA_KV_HEADS, NSA_GROUP, 1), F32),
                        pltpu.VMEM((NSA_KV_HEADS, NSA_GROUP, HEAD_DIM), F32)],
    )
    return pl.pallas_call(
        functools.partial(_nsa_dec_sel_kernel, pos=pos),
        name="nsa_dec_sel",
        grid_spec=grid_spec,
        out_shape=jax.ShapeDtypeStruct(q.shape, F32),
        compiler_params=_cp("parallel", "arbitrary"),
    )(pages, halves, blocks, q, k_new, v_new, rb, *kv_args)


def _nsa_dec_win_kernel(q_ref, g_ref, win_ref, new_ref, bias_ref, oc_ref, os_ref, y_ref, nw_ref):
    scale = HEAD_DIM ** -0.5
    nwin = win_ref.shape[0]
    sig = jax.nn.sigmoid(g_ref[...])
    for h in range(NSA_KV_HEADS):
        q = q_ref[h].astype(BF16)
        k = win_ref[:, 0, h, :].astype(BF16)
        v = win_ref[:, 1, h, :].astype(BF16)
        kn = new_ref[0, h:h + 1, :].astype(BF16)
        vn = new_ref[1, h:h + 1, :].astype(BF16)
        bias = bias_ref[h * NSA_GROUP:(h + 1) * NSA_GROUP]
        s = _nt(q, k) * scale + bias[:, :nwin]
        s_new = jnp.sum(q.astype(F32) * kn.astype(F32), axis=-1, keepdims=True) * scale + bias[:, nwin:nwin + 1]
        m = jnp.maximum(jnp.max(s, axis=-1, keepdims=True), s_new)
        p = jnp.exp(s - m)
        p_new = jnp.exp(s_new - m)
        den = jnp.sum(p, axis=-1, keepdims=True) + p_new
        ow = (jnp.dot(p.astype(BF16), v, preferred_element_type=F32)
              + p_new.astype(BF16).astype(F32) * vn.astype(F32)) / den
        gs = sig[h]
        y_ref[h] = (gs[:, 0:1] * oc_ref[h] + gs[:, 1:2] * os_ref[h] + gs[:, 2:3] * ow).astype(y_ref.dtype)
    nw_ref[pl.ds(0, nwin - 1)] = win_ref[pl.ds(1, nwin - 1)]
    nw_ref[nwin - 1] = new_ref[...]


def _nsa_dec_win(q, gates, win, new_kv, bias_w, o_c, o_s):
    b = q.shape[0]
    qspec = pl.BlockSpec((None,) + q.shape[1:], lambda i: (i, 0, 0, 0))
    wspec = pl.BlockSpec((None,) + win.shape[1:], lambda i: (i, 0, 0, 0, 0))
    bias2 = bias_w[:, 0, :]
    return pl.pallas_call(
        _nsa_dec_win_kernel,
        name="nsa_dec_win",
        grid=(b,),
        in_specs=[qspec, pl.BlockSpec((None,) + gates.shape[1:], lambda i: (i, 0, 0, 0)), wspec,
                  pl.BlockSpec((None,) + new_kv.shape[1:], lambda i: (i, 0, 0, 0)),
                  pl.BlockSpec(bias2.shape, lambda i: (0, 0)), qspec, qspec],
        out_specs=[qspec, wspec],
        out_shape=[jax.ShapeDtypeStruct(q.shape, BF16), jax.ShapeDtypeStruct(win.shape, F32)],
        compiler_params=_cp("parallel"),
    )(q, gates, win, new_kv, bias2, o_c, o_s)


def _tail(x, xb_unused, y_mix, mem_kv, ffn_prev, p, l, bsz, t, alpha):
    d = x.shape[1]
    x1, x1b = _add_ln(x, y_mix, p['ln_g'][l, 0], p['ln_b'][l, 0], alpha)
    tm = 1024
    q = _mm(x1b, p['w_cq'], F32, tm, 512, "mm_cq", layer=l)
    o = _cross(q.reshape(bsz, t, -1), mem_kv, 512).reshape(bsz * t, -1)
    x2, x2b = _mm_add_ln(o, p['w_co'], l, x1, p['ln_g'][l, 1], p['ln_b'][l, 1], alpha)
    if t > 1:
        act, st1, st2 = _ffn_up(x2b, p['w_up'], l, p['ffn_conv'][l], t)
        ffn_new = jnp.concatenate([st1[:, 6:], st2[:, 6:]], axis=-1)
    else:
        act, ffn_new = _ffn_up_dec(x2b, p['w_up'], l, p['ffn_conv'][l], ffn_prev)
    dff = act.shape[1]
    f = _mm(act, p['w_down'], F32, 512, 512, "mm_down", layer=l)
    x3, x3b = _add_ln(x2, f, p['ln_g'][l, 2], p['ln_b'][l, 2], alpha)
    return x3, x3b, ffn_new


def kernel(x_prompt, x_sample, mem_prompt, state_pool, cache_nsa_cmp_k, cache_nsa_cmp_v, cache_nsa_sel_k, cache_nsa_sel_v, state_nsa_win, state_sc, cache_sb_k, cache_sb_v, state_ffn, cache_mem, page_table, w_in_even, w_pool, pool_scale, w_cmp_pos, w_cmp_lin, rel_bias, w_out_even, w_in_odd, sc_conv, w_out_odd, w_cq, w_ckv, w_co, w_up, ffn_conv, w_down, ln_g, ln_b):
    bp, t, d = x_prompt.shape
    bs = x_sample.shape[0]
    depth = w_cq.shape[0]
    n_pages = page_table.shape[1]
    past = n_pages * PAGE_SIZE
    alpha = (2.0 * depth) ** 0.25
    kvw = NSA_KV_HEADS * HEAD_DIM
    assert x_sample.shape[1] == 1 and state_nsa_win.shape[2] == WINDOW

    w_even = jnp.pad(w_in_even, ((0, 0), (0, 0), (0, E_TOT - w_in_even.shape[2]))).astype(BF16)
    p = {'w_cq': w_cq.astype(BF16), 'w_co': w_co.astype(BF16), 'w_up': w_up,
         'w_down': w_down.astype(BF16), 'ffn_conv': ffn_conv, 'ln_g': ln_g, 'ln_b': ln_b}
    w_odd = w_in_odd.astype(BF16)
    w_oe = w_out_even.astype(BF16)
    w_oo = w_out_odd.astype(BF16)
    w_kv = w_ckv.astype(BF16)
    w_pool_b = w_pool.astype(BF16)
    w_lin_b = w_cmp_lin.astype(BF16)
    w_pos = w_cmp_pos.reshape(w_cmp_pos.shape[0], 2, CMP_BLOCK, kvw)

    qb = 128
    kj = np.arange(qb)[:, None]
    qt = np.arange(qb)[None, :]
    dist_d = np.stack([qt - kj, qb + qt - kj]).reshape(2 * qb, qb).astype(np.int32)
    tabd = _bias_lookup(rel_bias, jnp.asarray(dist_d), 2 * qb).reshape(NSA_HEADS, 2, qb, qb)
    nchunk = t // CMP_STRIDE
    dist_c = (np.arange(t)[None, :] - (np.arange(nchunk)[:, None] * CMP_STRIDE + CMP_BLOCK - 1)).astype(np.int32)
    tabc = _bias_lookup_blocked(rel_bias, jnp.asarray(dist_c), qb)
    b31 = rel_bias[N_BUCKETS - 1]

    mem_b = mem_prompt.reshape(bp * mem_prompt.shape[1], d).astype(BF16)
    n_mem = mem_prompt.shape[1]

    x = x_prompt.reshape(bp * t, d)
    xb = x.astype(BF16)
    outs_p = {k: [] for k in ('pool', 'cmp_k', 'cmp_v', 'sel_k', 'sel_v', 'win', 'sc', 'sb_k', 'sb_v', 'ffn', 'mem')}
    for l in range(depth):
        e = l // 2
        memkv = _mm(mem_b, w_kv, F32, 1024, 512, "mm_memkv", layer=l)
        outs_p['mem'].append(memkv.reshape(bp, n_mem, 2, MEM_HEADS, MEM_HEAD_DIM))
        if l % 2 == 0:
            qw = E_KC - E_Q
            hu = _mm(xb, w_even[e], F32, 1024, 512, "mm_in_even", E_U, E_Q - E_U).reshape(bp, t, E_Q - E_U)
            hq = _mm(xb, w_even[e], BF16, 1024, 512, "mm_in_even", E_Q, qw).reshape(bp, t, qw)
            h = _mm(xb, w_even[e], F32, 1024, 512, "mm_in_even", E_R, E_TOT - E_R).reshape(bp, t, E_TOT - E_R)
            y_pool = _pool_prompt(hu, 0, w_pool_b[e], pool_scale[e])
            kcmp, vcmp = _compress_prompt(h, w_pos[e], w_lin_b[e])
            y_nsa = _nsa_prompt(hq, h, kcmp, vcmp, tabc, tabd, b31)
            mix = jnp.concatenate([y_pool, y_nsa], axis=-1).reshape(bp * t, d)
            y = _mm(mix, w_oe[e], F32, 1024, 512, "mm_out")
            outs_p['pool'].append(hu[:, t - POOL_STATE:])
            for name, off in (('cmp_k', E_KC), ('cmp_v', E_VC), ('sel_k', E_KS), ('sel_v', E_VS)):
                outs_p[name].append(h[:, :, off - E_R:off - E_R + kvw].reshape(bp, t, NSA_KV_HEADS, HEAD_DIM))
            nw = min(WINDOW, t)
            outs_p['win'].append(h[:, t - nw:, E_KVW - E_R:E_KVW - E_R + 2 * kvw]
                                 .reshape(bp, nw, 2, NSA_KV_HEADS, HEAD_DIM))
        else:
            sbw = O_K - O_Q
            hc = _mm(xb, w_odd[e], F32, 1024, 512, "mm_in_odd", O_X, O_Q).reshape(bp, t, O_Q)
            hq = _mm(xb, w_odd[e], BF16, 1024, 512, "mm_in_odd", O_Q, sbw).reshape(bp, t, sbw)
            hk = _mm(xb, w_odd[e], F32, 1024, 512, "mm_in_odd", O_K, sbw).reshape(bp, t, sbw)
            hv = _mm(xb, w_odd[e], F32, 1024, 512, "mm_in_odd", O_V, sbw).reshape(bp, t, sbw)
            y_sc, sc_st = _sconv_prompt(hc, sc_conv[e])
            y_sb = _sb_prompt(hq, hk, hv)
            mix = jnp.concatenate([y_sc, y_sb], axis=-1).reshape(bp * t, d)
            y = _mm(mix, w_oo[e], F32, 1024, 512, "mm_out")
            outs_p['sc'].append(sc_st[:, 6:])
            nh = sbw // HEAD_DIM
            outs_p['sb_k'].append(hk.reshape(bp, t, nh, HEAD_DIM))
            outs_p['sb_v'].append(hv.reshape(bp, t, nh, HEAD_DIM))
        x, xb, ffn_new = _tail(x, xb, y, memkv.reshape(bp, n_mem, -1), None, p, l, bp, t, alpha)
        outs_p['ffn'].append(ffn_new)
    y_prompt = x.reshape(bp, t, d)

    pos = past
    length = past + 1
    n_cmp = (length - CMP_BLOCK) // CMP_STRIDE + 1
    n_slc = -(-length // SEL_BLOCK)
    k_top = min(N_SEL, n_slc)
    ncrow = n_pages * (PAGE_SIZE // CMP_STRIDE)
    dist_cd = (pos - (np.arange(ncrow) * CMP_STRIDE + CMP_BLOCK - 1)).astype(np.int32)
    bias_cd = _bias_lookup(rel_bias, jnp.asarray(np.broadcast_to(dist_cd, (8, ncrow))), 8)[:, 0, :]
    dist_wd = np.maximum(WINDOW - np.arange(WINDOW + LANES), 0).astype(np.int32)
    bias_wd = _bias_lookup(rel_bias, jnp.asarray(np.broadcast_to(dist_wd, (8, WINDOW + LANES))), 8)
    rb_hg = rel_bias.T.reshape(NSA_KV_HEADS, NSA_GROUP, N_BUCKETS)
    mem_s = cache_mem.reshape(depth, bs, cache_mem.shape[2], -1)

    x = x_sample.reshape(bs, d)
    xb = x.astype(BF16)
    outs_s = {k: [] for k in ('pool', 'cmp_k', 'cmp_v', 'sel_k', 'sel_v', 'win', 'sc', 'sb_k', 'sb_v', 'ffn')}
    for l in range(depth):
        e = l // 2
        if l % 2 == 0:
            h = _mm(xb, w_even[e], F32, 8, 512)
            y_pool, pool_new = _pool_dec(state_pool[e], h[:, E_U:E_U + 1024], w_pool_b[e], pool_scale[e], pos)
            q4 = h[:, E_Q:E_Q + NSA_HEADS * HEAD_DIM].reshape(bs, NSA_KV_HEADS, NSA_GROUP, HEAD_DIM)
            gates = h[:, E_G:E_G + NSA_HEADS * 3].reshape(bs, NSA_KV_HEADS, NSA_GROUP, 3)
            ak, bk, av, bv = _cmp_dec(cache_nsa_cmp_k, cache_nsa_cmp_v, e, page_table, w_pos[e])
            o_c, idx = _nsa_dec_cmp(q4, ak, bk, av, bv, w_lin_b[e], bias_cd, pos, n_cmp, n_slc)
            blocks = idx[:, :, 0, :k_top]
            past_blk = jnp.where((blocks >= 0) & (blocks * SEL_BLOCK < past), blocks, -1)
            safe = jnp.maximum(past_blk, 0)
            per_page = PAGE_SIZE // SEL_BLOCK
            pages = jnp.take_along_axis(page_table[:, None, :], safe // per_page, axis=2)
            halves = safe % per_page
            ks_new = h[:, E_KS:E_KS + kvw].reshape(bs, NSA_KV_HEADS, HEAD_DIM)
            vs_new = h[:, E_VS:E_VS + kvw].reshape(bs, NSA_KV_HEADS, HEAD_DIM)
            o_s = _nsa_dec_sel(q4, ks_new, vs_new, rb_hg, cache_nsa_sel_k, cache_nsa_sel_v, e,
                               pages, halves, past_blk, pos)
            new_kv = h[:, E_KVW:E_KVW + 2 * kvw].reshape(bs, 2, NSA_KV_HEADS, HEAD_DIM)
            y_nsa, win_new = _nsa_dec_win(q4, gates, state_nsa_win[e], new_kv, bias_wd, o_c, o_s)
            mix = jnp.concatenate([y_pool, y_nsa.reshape(bs, -1)], axis=-1)
            y = _mm(mix, w_oe[e], F32, 8, 512)
            outs_s['pool'].append(pool_new)
            for name, off in (('cmp_k', E_KC), ('cmp_v', E_VC), ('sel_k', E_KS), ('sel_v', E_VS)):
                outs_s[name].append(h[:, off:off + kvw].reshape(bs, 1, NSA_KV_HEADS, HEAD_DIM))
            outs_s['win'].append(win_new)
        else:
            h = _mm(xb, w_odd[e], F32, 8, 512)
            y_sc, sc_new = _sconv_dec(state_sc[e], h, sc_conv[e])
            nh = (O_K - O_Q) // HEAD_DIM
            q3 = h[:, O_Q:O_K].reshape(bs, nh, HEAD_DIM)
            y_sb = _sb_dec(q3, cache_sb_k, cache_sb_v, e, page_table)
            mix = jnp.concatenate([y_sc, y_sb.reshape(bs, -1)], axis=-1)
            y = _mm(mix, w_oo[e], F32, 8, 512)
            outs_s['sc'].append(sc_new)
            outs_s['sb_k'].append(h[:, O_K:O_V].reshape(bs, 1, nh, HEAD_DIM))
            outs_s['sb_v'].append(h[:, O_V:].reshape(bs, 1, nh, HEAD_DIM))
        x, xb, ffn_new = _tail(x, xb, y, mem_s[l], state_ffn[l], p, l, bs, 1, alpha)
        outs_s['ffn'].append(ffn_new)
    y_sample = x.reshape(bs, 1, d)

    sp = {k: jnp.stack(v) for k, v in outs_p.items()}
    ss = {k: jnp.stack(v) for k, v in outs_s.items()}
    return (y_prompt, y_sample,
            sp['pool'], sp['cmp_k'], sp['cmp_v'], sp['sel_k'], sp['sel_v'], sp['win'],
            sp['sc'], sp['sb_k'], sp['sb_v'], sp['ffn'], sp['mem'],
            ss['pool'], ss['cmp_k'], ss['cmp_v'], ss['sel_k'], ss['sel_v'], ss['win'],
            ss['sc'], ss['sb_k'], ss['sb_v'], ss['ffn'])
```

```python
import functools
import math

import jax
import jax.numpy as jnp
import numpy as np
from jax import lax
from jax.experimental import pallas as pl
from jax.experimental.pallas import tpu as pltpu

F32 = jnp.float32
BF16 = jnp.bfloat16

HEAD_DIM = 128
PAGE_SIZE = 128
POOL_WINDOWS = (2, 4, 8, 16)
POOL_GROUP_DIM = 256
POOL_STATE = 15
NSA_KV_HEADS = 2
NSA_GROUP = 12
NSA_HEADS = 24
CMP_BLOCK = 32
CMP_STRIDE = 16
SEL_BLOCK = 64
SEL_SHIFT = 6
N_SEL = 16
WINDOW = 512
N_BUCKETS = 32
MAX_DISTANCE = 128
MEM_HEADS = 4
MEM_HEAD_DIM = 256
LN_EPS = 1e-5
NEG_INF = -1e30
BIG = 1e30

LANES = 128
VMEM_LIMIT = 56 * 1024 * 1024

E_U, E_Q, E_KC, E_VC, E_KS, E_VS, E_KVW, E_G, E_TOT = 0, 1024, 4096, 4352, 4608, 4864, 5120, 5632, 6144
E_R = E_KC
O_X, O_B, O_C, O_Q, O_K, O_V = 0, 1024, 2048, 3072, 6144, 9216


def _cp(*sem):
    return pltpu.CompilerParams(dimension_semantics=sem, vmem_limit_bytes=VMEM_LIMIT)


def _nt(a, b):
    return lax.dot_general(a, b, (((1,), (1,)), ((), ())), preferred_element_type=F32)


def _split3(x):
    hi = x.astype(BF16)
    r1 = x - hi.astype(F32)
    mid = r1.astype(BF16)
    lo = (r1 - mid.astype(F32)).astype(BF16)
    return hi, mid, lo


def _mm_kernel(x_ref, w_ref, o_ref):
    o_ref[...] = jnp.dot(x_ref[...], w_ref[...], preferred_element_type=F32).astype(o_ref.dtype)


def _mm(x, w, out_dtype, tm, tn, name="mm", col0=0, ncols=None, layer=None):
    m, k = x.shape
    n = w.shape[-1] - col0 if ncols is None else ncols
    tm = min(tm, m)
    c0 = col0 // tn
    assert col0 % tn == 0 and n % tn == 0
    if layer is None:
        wspec = pl.BlockSpec((k, tn), lambda i, j: (0, j + c0))
    else:
        wspec = pl.BlockSpec((None, k, tn), lambda i, j: (layer, 0, j + c0))
    return pl.pallas_call(
        _mm_kernel,
        name=name,
        grid=(m // tm, n // tn),
        in_specs=[pl.BlockSpec((tm, k), lambda i, j: (i, 0)), wspec],
        out_specs=pl.BlockSpec((tm, tn), lambda i, j: (i, j)),
        out_shape=jax.ShapeDtypeStruct((m, n), out_dtype),
        compiler_params=_cp("parallel", "parallel"),
    )(x, w)


def _mm_cat_kernel(*refs):
    nblk = (len(refs) - 1) // 2
    acc = None
    for i in range(nblk):
        part = jnp.dot(refs[i][...], refs[nblk + i][...], preferred_element_type=F32)
        acc = part if acc is None else acc + part
    refs[-1][...] = acc.astype(refs[-1].dtype)


def _mm_cat(xa, xb, w, out_dtype, tm, tn, name="mm_out"):
    m, kb = xa.shape
    assert xb.shape[1] % kb == 0 and w.shape[0] == kb + xb.shape[1]
    nb = xb.shape[1] // kb
    n = w.shape[1]
    tm = min(tm, m)
    x_specs = [pl.BlockSpec((tm, kb), lambda i, j: (i, 0))]
    x_specs += [pl.BlockSpec((tm, kb), lambda i, j, c=c: (i, c)) for c in range(nb)]
    w_specs = [pl.BlockSpec((kb, tn), lambda i, j, r=r: (r, j)) for r in range(nb + 1)]
    return pl.pallas_call(
        _mm_cat_kernel,
        name=name,
        grid=(m // tm, n // tn),
        in_specs=x_specs + w_specs,
        out_specs=pl.BlockSpec((tm, tn), lambda i, j: (i, j)),
        out_shape=jax.ShapeDtypeStruct((m, n), out_dtype),
        compiler_params=_cp("parallel", "parallel"),
    )(xa, *([xb] * nb), *([w] * (nb + 1)))


def _mm_acc_kernel(x_ref, w_ref, o_ref, acc_ref):
    k = pl.program_id(2)

    @pl.when(k == 0)
    def _():
        acc_ref[...] = jnp.zeros_like(acc_ref)

    acc_ref[...] += jnp.dot(x_ref[...], w_ref[...], preferred_element_type=F32)

    @pl.when(k == pl.num_programs(2) - 1)
    def _():
        o_ref[...] = acc_ref[...].astype(o_ref.dtype)


def _mm_acc(x, w, layer, out_dtype, tm, tn, tk):
    m, k = x.shape
    n = w.shape[2]
    tm = min(tm, m)
    return pl.pallas_call(
        _mm_acc_kernel,
        name="mm_down",
        grid=(m // tm, n // tn, k // tk),
        in_specs=[pl.BlockSpec((tm, tk), lambda i, j, kk: (i, kk)),
                  pl.BlockSpec((None, tk, tn), lambda i, j, kk: (layer, kk, j))],
        out_specs=pl.BlockSpec((tm, tn), lambda i, j, kk: (i, j)),
        out_shape=jax.ShapeDtypeStruct((m, n), out_dtype),
        scratch_shapes=[pltpu.VMEM((tm, tn), F32)],
        compiler_params=_cp("parallel", "parallel", "arbitrary"),
    )(x, w)


def _ln_kernel(x_ref, y_ref, g_ref, b_ref, of_ref, ob_ref, *, alpha):
    z = alpha * x_ref[...] + y_ref[...]
    mu = jnp.mean(z, axis=-1, keepdims=True)
    zc = z - mu
    var = jnp.mean(zc * zc, axis=-1, keepdims=True)
    out = zc * lax.rsqrt(var + LN_EPS) * g_ref[...] + b_ref[...]
    of_ref[...] = out
    ob_ref[...] = out.astype(BF16)


def _add_ln(x, y, g, b, alpha, tm=256):
    m, d = x.shape
    tm = min(tm, m)
    row = pl.BlockSpec((tm, d), lambda i: (i, 0))
    vec = pl.BlockSpec((1, d), lambda i: (0, 0))
    return pl.pallas_call(
        functools.partial(_ln_kernel, alpha=alpha),
        name="add_ln",
        grid=(m // tm,),
        in_specs=[row, row, vec, vec],
        out_specs=[row, row],
        out_shape=[jax.ShapeDtypeStruct((m, d), F32), jax.ShapeDtypeStruct((m, d), BF16)],
        compiler_params=_cp("parallel"),
    )(x, y, g.reshape(1, d), b.reshape(1, d))


def _mm_ln_kernel(x_ref, w_ref, r_ref, g_ref, b_ref, of_ref, ob_ref, *, alpha):
    y = jnp.dot(x_ref[...], w_ref[...], preferred_element_type=F32)
    z = alpha * r_ref[...] + y
    mu = jnp.mean(z, axis=-1, keepdims=True)
    zc = z - mu
    var = jnp.mean(zc * zc, axis=-1, keepdims=True)
    out = zc * lax.rsqrt(var + LN_EPS) * g_ref[...] + b_ref[...]
    of_ref[...] = out
    ob_ref[...] = out.astype(BF16)


def _mm_add_ln(x, w, layer, res, g, b, alpha, tm=256):
    m, k = x.shape
    d = w.shape[2]
    tm = min(tm, m)
    row = pl.BlockSpec((tm, d), lambda i: (i, 0))
    vec = pl.BlockSpec((1, d), lambda i: (0, 0))
    return pl.pallas_call(
        functools.partial(_mm_ln_kernel, alpha=alpha),
        name="mm_add_ln",
        grid=(m // tm,),
        in_specs=[pl.BlockSpec((tm, k), lambda i: (i, 0)),
                  pl.BlockSpec((None, k, d), lambda i: (layer, 0, 0)), row, vec, vec],
        out_specs=[row, row],
        out_shape=[jax.ShapeDtypeStruct((m, d), F32), jax.ShapeDtypeStruct((m, d), BF16)],
        compiler_params=_cp("parallel"),
    )(x, w, res, g.reshape(1, d), b.reshape(1, d))


def _cross_kernel(q_ref, kv_ref, o_ref):
    width = MEM_HEADS * MEM_HEAD_DIM
    scale = MEM_HEAD_DIM ** -0.5
    for h in range(MEM_HEADS):
        sl = slice(h * MEM_HEAD_DIM, (h + 1) * MEM_HEAD_DIM)
        q = q_ref[:, sl].astype(BF16)
        k = kv_ref[:, sl].astype(BF16)
        v = kv_ref[:, width + h * MEM_HEAD_DIM: width + (h + 1) * MEM_HEAD_DIM].astype(BF16)
        s = _nt(q, k) * scale
        s = s - jnp.max(s, axis=-1, keepdims=True)
        e = jnp.exp(s)
        p = e / jnp.sum(e, axis=-1, keepdims=True)
        o_ref[:, sl] = jnp.dot(p.astype(BF16), v, preferred_element_type=F32).astype(o_ref.dtype)


def _cross(q, kv, tq):
    b, t, w = q.shape
    tq = min(tq, t)
    return pl.pallas_call(
        _cross_kernel,
        name="cross_attn",
        grid=(b, t // tq),
        in_specs=[pl.BlockSpec((None, tq, w), lambda i, j: (i, j, 0)),
                  pl.BlockSpec((None, kv.shape[1], kv.shape[2]), lambda i, j: (i, 0, 0))],
        out_specs=pl.BlockSpec((None, tq, w), lambda i, j: (i, j, 0)),
        out_shape=jax.ShapeDtypeStruct((b, t, w), BF16),
        compiler_params=_cp("parallel", "parallel"),
    )(q, kv)


def _shift_rows(h, c_last2, row):
    r1 = pltpu.roll(h, 1, 0)
    r2 = pltpu.roll(h, 2, 0)
    row8 = row[0:8]
    f1 = jnp.where(row8 == 0, c_last2[7:8], r1[0:8])
    f2 = jnp.where(row8 == 0, c_last2[6:7], jnp.where(row8 == 1, c_last2[7:8], r2[0:8]))
    return jnp.concatenate([f1, r1[8:]], axis=0), jnp.concatenate([f2, r2[8:]], axis=0)


def _ffn_up_kernel(x_ref, w1_ref, w2_ref, cw1_ref, cw2_ref, act_ref, st1_ref, st2_ref, *, sub):
    t = x_ref.shape[0]
    tn = w1_ref.shape[1]
    ns = t // sub
    w1 = w1_ref[...].astype(BF16)
    w2 = w2_ref[...].astype(BF16)
    cw1 = cw1_ref[...]
    cw2 = cw2_ref[...]
    row = lax.broadcasted_iota(jnp.int32, (sub, 1), 0)
    zeros8 = jnp.zeros((8, tn), F32)
    h = [None] * ns

    def dots(s):
        xs = x_ref[s * sub:(s + 1) * sub, :]
        return jnp.dot(xs, w1, preferred_element_type=F32), jnp.dot(xs, w2, preferred_element_type=F32)

    def conv(hc, prev8, cw):
        s1, s2 = _shift_rows(hc, prev8, row)
        return s2 * cw[0:1] + s1 * cw[1:2] + hc * cw[2:3]

    def epilogue(s):
        p1, p2 = (zeros8, zeros8) if s == 0 else (h[s - 1][0][sub - 8:], h[s - 1][1][sub - 8:])
        c1 = conv(h[s][0], p1, cw1)
        c2 = conv(h[s][1], p2, cw2)
        act_ref[s * sub:(s + 1) * sub, :] = (c1 * jax.nn.sigmoid(c1) * c2).astype(BF16)

    h[0] = dots(0)
    for s in range(1, ns):
        h[s] = dots(s)
        epilogue(s - 1)
    epilogue(ns - 1)
    st1_ref[...] = h[ns - 1][0][sub - 8:]
    st2_ref[...] = h[ns - 1][1][sub - 8:]


def _ffn_up(x, w_up, layer, conv_w, t, sub=512, tn=256):
    m, d = x.shape
    dff = w_up.shape[2] // 2
    nt = dff // tn
    sub = min(sub, t)
    bsz = m // t
    return pl.pallas_call(
        functools.partial(_ffn_up_kernel, sub=sub),
        grid=(bsz, nt),
        in_specs=[pl.BlockSpec((t, d), lambda i, j: (i, 0), pipeline_mode=pl.Buffered(1)),
                  pl.BlockSpec((None, d, tn), lambda i, j: (layer, 0, j)),
                  pl.BlockSpec((None, d, tn), lambda i, j: (layer, 0, j + nt)),
                  pl.BlockSpec((3, tn), lambda i, j: (0, j)),
                  pl.BlockSpec((3, tn), lambda i, j: (0, j + nt))],
        out_specs=[pl.BlockSpec((t, tn), lambda i, j: (i, j)),
                   pl.BlockSpec((None, 8, tn), lambda i, j: (i, 0, j)),
                   pl.BlockSpec((None, 8, tn), lambda i, j: (i, 0, j))],
        out_shape=[jax.ShapeDtypeStruct((m, dff), BF16),
                   jax.ShapeDtypeStruct((bsz, 8, dff), F32),
                   jax.ShapeDtypeStruct((bsz, 8, dff), F32)],
        compiler_params=_cp("parallel", "parallel"),
        name="ffn_up",
    )(x, w_up, w_up, conv_w, conv_w)


def _ffn_up_dec_kernel(x_ref, w1_ref, w2_ref, cw1_ref, cw2_ref, p1_ref, p2_ref, act_ref, n1_ref, n2_ref):
    x = x_ref[...]
    h1 = jnp.dot(x, w1_ref[...].astype(BF16), preferred_element_type=F32)
    h2 = jnp.dot(x, w2_ref[...].astype(BF16), preferred_element_type=F32)

    def conv(h, p_ref, cw):
        return p_ref[:, 0, :] * cw[0:1] + p_ref[:, 1, :] * cw[1:2] + h * cw[2:3]

    c1 = conv(h1, p1_ref, cw1_ref[...])
    c2 = conv(h2, p2_ref, cw2_ref[...])
    act_ref[...] = (c1 * jax.nn.sigmoid(c1) * c2).astype(BF16)
    n1_ref[:, 0, :] = p1_ref[:, 1, :]
    n1_ref[:, 1, :] = h1
    n2_ref[:, 0, :] = p2_ref[:, 1, :]
    n2_ref[:, 1, :] = h2


def _ffn_up_dec(x, w_up, layer, conv_w, prev, tn=256):
    b, d = x.shape
    dff = w_up.shape[2] // 2
    nt = dff // tn
    pspec1 = pl.BlockSpec((b, 2, tn), lambda j: (0, 0, j))
    pspec2 = pl.BlockSpec((b, 2, tn), lambda j: (0, 0, j + nt))
    act, n1, n2 = pl.pallas_call(
        _ffn_up_dec_kernel,
        name="ffn_up_dec",
        grid=(nt,),
        in_specs=[pl.BlockSpec((b, d), lambda j: (0, 0)),
                  pl.BlockSpec((None, d, tn), lambda j: (layer, 0, j)),
                  pl.BlockSpec((None, d, tn), lambda j: (layer, 0, j + nt)),
                  pl.BlockSpec((3, tn), lambda j: (0, j)),
                  pl.BlockSpec((3, tn), lambda j: (0, j + nt)),
                  pspec1, pspec2],
        out_specs=[pl.BlockSpec((b, tn), lambda j: (0, j)),
                   pl.BlockSpec((b, 2, tn), lambda j: (0, 0, j)),
                   pl.BlockSpec((b, 2, tn), lambda j: (0, 0, j))],
        out_shape=[jax.ShapeDtypeStruct((b, dff), BF16),
                   jax.ShapeDtypeStruct((b, 2, dff), F32),
                   jax.ShapeDtypeStruct((b, 2, dff), F32)],
        compiler_params=_cp("parallel"),
    )(x, w_up, w_up, conv_w, conv_w, prev, prev)
    return act, jnp.concatenate([n1, n2], axis=-1)


def _bias_kernel(rb_ref, dist_ref, o_ref):
    head = pl.program_id(0)
    n = jnp.maximum(dist_ref[...], 0)
    max_exact = N_BUCKETS // 2
    nf = jnp.maximum(n, 1).astype(F32)
    large = max_exact + (jnp.log(nf / max_exact) / math.log(MAX_DISTANCE / max_exact)
                         * (N_BUCKETS - max_exact)).astype(jnp.int32)
    bucket = jnp.where(n < max_exact, n, jnp.minimum(large, N_BUCKETS - 1))
    out = jnp.zeros(n.shape, F32)
    for k in range(N_BUCKETS):
        out = jnp.where(bucket == k, rb_ref[k, head], out)
    o_ref[...] = out


def _bias_lookup(rel_bias, dist, tr):
    r, c = dist.shape
    nh = rel_bias.shape[1]
    tr = min(tr, r)
    return pl.pallas_call(
        _bias_kernel,
        grid=(nh, r // tr),
        in_specs=[pl.BlockSpec(memory_space=pltpu.SMEM),
                  pl.BlockSpec((tr, c), lambda h, i: (i, 0))],
        out_specs=pl.BlockSpec((None, tr, c), lambda h, i: (h, i, 0)),
        out_shape=jax.ShapeDtypeStruct((nh, r, c), F32),
        compiler_params=_cp("parallel", "parallel"),
        name="bias_lookup",
    )(rel_bias, dist)


def _bias_blocked_kernel(rb_ref, dist_ref, o_ref):
    n = jnp.maximum(dist_ref[...], 0)
    max_exact = N_BUCKETS // 2
    nf = jnp.maximum(n, 1).astype(F32)
    large = max_exact + (jnp.log(nf / max_exact) / math.log(MAX_DISTANCE / max_exact)
                         * (N_BUCKETS - max_exact)).astype(jnp.int32)
    bucket = jnp.where(n < max_exact, n, jnp.minimum(large, N_BUCKETS - 1))
    hits = [bucket == k for k in range(N_BUCKETS)]

    def head(hd, carry):
        out = jnp.zeros(n.shape, F32)
        for k in range(N_BUCKETS):
            out = jnp.where(hits[k], rb_ref[k, hd], out)
        o_ref[hd] = out
        return carry

    lax.fori_loop(0, o_ref.shape[0], head, 0)


def _bias_lookup_blocked(rel_bias, dist, tc):
    r, c = dist.shape
    nh = rel_bias.shape[1]
    return pl.pallas_call(
        _bias_blocked_kernel,
        grid=(c // tc,),
        in_specs=[pl.BlockSpec(memory_space=pltpu.SMEM),
                  pl.BlockSpec((r, tc), lambda j: (0, j))],
        out_specs=pl.BlockSpec((None, nh, r, tc), lambda j: (j, 0, 0, 0)),
        out_shape=jax.ShapeDtypeStruct((c // tc, nh, r, tc), F32),
        compiler_params=_cp("parallel"),
        name="bias_lookup_blocked",
    )(rel_bias, dist)


def _pool_groups(sums, cur, pos, wp_ref, sc_ref, y_ref):
    for g, w in enumerate(POOL_WINDOWS):
        sl = slice(g * POOL_GROUP_DIM, (g + 1) * POOL_GROUP_DIM)
        cnt = jnp.minimum(w, pos + 1).astype(F32)
        d = sums[g] / cnt - cur[:, sl]
        y = jnp.dot(d.astype(BF16), wp_ref[g], preferred_element_type=F32) * sc_ref[:, sl]
        y_ref[:, sl] = y.astype(y_ref.dtype)


def _pool_kernel(prev_ref, cur_ref, wp_ref, sc_ref, y_ref):
    t = pl.program_id(1)
    tt = cur_ref.shape[0]
    cur = cur_ref[...]
    prev = jnp.where(t == 0, 0.0, prev_ref[...])
    ext = jnp.concatenate([prev, cur], axis=0)
    gd = POOL_GROUP_DIM
    s2 = ext + pltpu.roll(ext, 1, 0)
    x4 = s2[:, gd:]
    s4 = x4 + pltpu.roll(x4, 2, 0)
    x8 = s4[:, gd:]
    s8 = x8 + pltpu.roll(x8, 4, 0)
    x16 = s8[:, gd:]
    s16 = x16 + pltpu.roll(x16, 8, 0)
    sums = [s2[16:, :gd], s4[16:, :gd], s8[16:, :gd], s16[16:]]
    pos = t * tt + lax.broadcasted_iota(jnp.int32, (tt, 1), 0)
    _pool_groups(sums, cur, pos, wp_ref, sc_ref, y_ref)


def _pool_prompt(h, col_blk, w_pool, pool_scale, tt=256):
    b, t, _ = h.shape
    wdt = w_pool.shape[0] * POOL_GROUP_DIM
    tt = min(tt, t)
    r = tt // 16
    return pl.pallas_call(
        _pool_kernel,
        name="pool_prompt",
        grid=(b, t // tt),
        in_specs=[pl.BlockSpec((None, 16, wdt), lambda i, j: (i, jnp.maximum(j * r - 1, 0), col_blk)),
                  pl.BlockSpec((None, tt, wdt), lambda i, j: (i, j, col_blk)),
                  pl.BlockSpec(w_pool.shape, lambda i, j: (0, 0, 0)),
                  pl.BlockSpec((1, wdt), lambda i, j: (0, 0))],
        out_specs=pl.BlockSpec((None, tt, wdt), lambda i, j: (i, j, 0)),
        out_shape=jax.ShapeDtypeStruct((b, t, wdt), BF16),
        compiler_params=_cp("parallel", "parallel"),
    )(h, h, w_pool, pool_scale.reshape(1, wdt))


def _pool_dec_kernel(sp_ref, u_ref, wp_ref, sc_ref, y_ref, ns_ref, *, pos):
    u = u_ref[...]
    gd = POOL_GROUP_DIM
    acc = u
    sums = []
    back = 1
    for g, w in enumerate(POOL_WINDOWS):
        while back < w:
            acc = acc + sp_ref[:, POOL_STATE - back, :]
            back += 1
        sums.append(acc[:, g * gd:(g + 1) * gd])
    posv = jnp.full((u.shape[0], 1), pos, jnp.int32)
    _pool_groups(sums, u, posv, wp_ref, sc_ref, y_ref)
    for j in range(POOL_STATE - 1):
        ns_ref[:, j, :] = sp_ref[:, j + 1, :]
    ns_ref[:, POOL_STATE - 1, :] = u


def _pool_dec(state, u, w_pool, pool_scale, pos):
    b, wdt = u.shape
    return pl.pallas_call(
        functools.partial(_pool_dec_kernel, pos=pos),
        out_shape=[jax.ShapeDtypeStruct((b, wdt), BF16), jax.ShapeDtypeStruct(state.shape, F32)],
        compiler_params=pltpu.CompilerParams(vmem_limit_bytes=VMEM_LIMIT),
    )(state, u, w_pool, pool_scale.reshape(1, wdt))


def _sconv_kernel(px_ref, pc_ref, x_ref, b_ref, c_ref, cw_ref, y_ref, st_ref):
    t = pl.program_id(1)
    tt = x_ref.shape[0]
    v = c_ref[...] * x_ref[...]
    pv = jnp.where(t == 0, 0.0, pc_ref[...] * px_ref[...])
    row = lax.broadcasted_iota(jnp.int32, (tt, 1), 0)
    s1, s2 = _shift_rows(v, pv, row)
    cw = cw_ref[...]
    conv = s2 * cw[0:1] + s1 * cw[1:2] + v * cw[2:3]
    y_ref[...] = (b_ref[...] * conv).astype(y_ref.dtype)
    st_ref[...] = v[tt - 8:]


def _sconv_prompt(h, conv_w, tt=256):
    b, t, _ = h.shape
    wdt = conv_w.shape[1]
    tt = min(tt, t)
    r = tt // 8

    def prev(blk):
        return pl.BlockSpec((None, 8, wdt), lambda i, j: (i, jnp.maximum(j * r - 1, 0), blk))

    def cur(blk):
        return pl.BlockSpec((None, tt, wdt), lambda i, j: (i, j, blk))

    return pl.pallas_call(
        _sconv_kernel,
        name="sconv_prompt",
        grid=(b, t // tt),
        in_specs=[prev(0), prev(2), cur(0), cur(1), cur(2), pl.BlockSpec((3, wdt), lambda i, j: (0, 0))],
        out_specs=[pl.BlockSpec((None, tt, wdt), lambda i, j: (i, j, 0)),
                   pl.BlockSpec((None, 8, wdt), lambda i, j: (i, 0, 0))],
        out_shape=[jax.ShapeDtypeStruct((b, t, wdt), BF16), jax.ShapeDtypeStruct((b, 8, wdt), F32)],
        compiler_params=_cp("parallel", "arbitrary"),
    )(h, h, h, h, h, conv_w)


def _sconv_dec_kernel(st_ref, x_ref, b_ref, c_ref, cw_ref, y_ref, ns_ref):
    v = c_ref[...] * x_ref[...]
    cw = cw_ref[...]
    conv = st_ref[:, 0, :] * cw[0:1] + st_ref[:, 1, :] * cw[1:2] + v * cw[2:3]
    y_ref[...] = (b_ref[...] * conv).astype(y_ref.dtype)
    ns_ref[:, 0, :] = st_ref[:, 1, :]
    ns_ref[:, 1, :] = v


def _sconv_dec(state, h, conv_w):
    b = h.shape[0]
    wdt = conv_w.shape[1]

    def col(blk):
        return pl.BlockSpec((b, wdt), lambda i: (0, blk))

    return pl.pallas_call(
        _sconv_dec_kernel,
        grid=(1,),
        in_specs=[pl.BlockSpec(state.shape, lambda i: (0, 0, 0)), col(0), col(1), col(2),
                  pl.BlockSpec((3, wdt), lambda i: (0, 0))],
        out_specs=[pl.BlockSpec((b, wdt), lambda i: (0, 0)), pl.BlockSpec(state.shape, lambda i: (0, 0, 0))],
        out_shape=[jax.ShapeDtypeStruct((b, wdt), BF16), jax.ShapeDtypeStruct(state.shape, F32)],
        compiler_params=_cp("arbitrary"),
    )(state, h, h, h, conv_w)


def _softplus(z):
    return jnp.maximum(z, 0.0) + jnp.log(1.0 + jnp.exp(-jnp.abs(z)))


def _suffix_sums(x, upper):
    r = x.shape[0]
    hi = x.astype(BF16)
    lo = (x - hi.astype(F32)).astype(BF16)
    s = jnp.dot(jnp.concatenate([hi, lo], axis=0), upper, preferred_element_type=F32)
    return s[:r] + s[r:]


SB_HEADS_PER_STEP = 4


def _sb_kernel(q_ref, k_ref, v_ref, o_ref, ab_sc, r_sc, acc_sc, *, tq):
    i = pl.program_id(2)
    scale = HEAD_DIM ** -0.5
    nhs = SB_HEADS_PER_STEP
    hsl = [slice(j * HEAD_DIM, (j + 1) * HEAD_DIM) for j in range(nhs)]
    qs = [q_ref[:, sl].astype(BF16) for sl in hsl]
    rj = lax.broadcasted_iota(jnp.int32, (tq, tq), 0)
    cj = lax.broadcasted_iota(jnp.int32, (tq, tq), 1)
    upper = jnp.where(rj > cj, 1.0, 0.0).astype(BF16)

    before = jnp.concatenate([cj < rj] * nhs, axis=0)

    def logits(c):
        off = pl.multiple_of(c * tq, tq)
        return jnp.concatenate([_nt(qs[j], k_ref[pl.ds(off, tq), hsl[j]].astype(BF16)) for j in range(nhs)],
                               axis=0) * scale

    def weights(z, r_run, diag):
        sp = _softplus(z)
        if diag:
            sp = jnp.where(before, sp, 0.0)
        a = jnp.exp(z - sp - _suffix_sums(sp, upper) - r_run)
        if diag:
            a = jnp.where(before, a, 0.0)
        return a.astype(BF16), r_run + jnp.sum(sp, axis=1, keepdims=True)

    def accumulate(c):
        off = pl.multiple_of(c * tq, tq)
        for j in range(nhs):
            acc_sc[j] += jnp.dot(ab_sc[j * tq:(j + 1) * tq, :], v_ref[pl.ds(off, tq), hsl[j]].astype(BF16),
                                 preferred_element_type=F32)

    acc_sc[...] = jnp.zeros_like(acc_sc)
    ab_sc[...], r_sc[...] = weights(logits(i), jnp.zeros((nhs * tq, 1), F32), True)

    def body(s, carry):
        c = i - s
        accumulate(c + 1)
        ab_sc[...], r_sc[...] = weights(logits(c), r_sc[...], False)
        return carry

    lax.fori_loop(1, i + 1, body, 0)
    accumulate(0)
    for j in range(nhs):
        o_ref[:, hsl[j]] = acc_sc[j].astype(o_ref.dtype)


def _sb_prompt(q, k, v, tq=256):
    b, t, width = q.shape
    tq = min(tq, t)
    wdt = SB_HEADS_PER_STEP * HEAD_DIM
    return pl.pallas_call(
        functools.partial(_sb_kernel, tq=tq),
        grid=(b, width // wdt, t // tq),
        in_specs=[pl.BlockSpec((None, tq, wdt), lambda bi, hi, i: (bi, i, hi)),
                  pl.BlockSpec((None, t, wdt), lambda bi, hi, i: (bi, 0, hi)),
                  pl.BlockSpec((None, t, wdt), lambda bi, hi, i: (bi, 0, hi))],
        out_specs=pl.BlockSpec((None, tq, wdt), lambda bi, hi, i: (bi, i, hi)),
        out_shape=jax.ShapeDtypeStruct((b, t, width), BF16),
        scratch_shapes=[pltpu.VMEM((SB_HEADS_PER_STEP * tq, tq), BF16), pltpu.VMEM((SB_HEADS_PER_STEP * tq, 1), F32),
                        pltpu.VMEM((SB_HEADS_PER_STEP, tq, HEAD_DIM), F32)],
        compiler_params=_cp("parallel", "parallel", "parallel"),
        name="sb_prompt",
    )(q, k, v)


SB_PAGES_PER_STEP = 2


def _sb_dec_kernel(pt_ref, q_ref, *refs):
    npg = SB_PAGES_PER_STEP
    k_refs, v_refs = refs[:npg], refs[npg:2 * npg]
    o_ref, z_sc, r_sc, acc_sc = refs[2 * npg:]
    c = pl.program_id(1)
    scale = HEAD_DIM ** -0.5
    nh = q_ref.shape[0]

    @pl.when(c == 0)
    def _():
        z_sc[...] = jnp.zeros_like(z_sc)
        r_sc[...] = jnp.zeros_like(r_sc)
        acc_sc[...] = jnp.zeros_like(acc_sc)

    rj = lax.broadcasted_iota(jnp.int32, (PAGE_SIZE, PAGE_SIZE), 0)
    cj = lax.broadcasted_iota(jnp.int32, (PAGE_SIZE, PAGE_SIZE), 1)
    upper = jnp.where(rj > cj, 1.0, 0.0).astype(BF16)
    hrow = lax.broadcasted_iota(jnp.int32, (nh, HEAD_DIM), 0)

    def heads_on_lanes(ref):
        return jnp.concatenate([ref[pl.ds(j, PAGE_SIZE, stride=nh), :].astype(BF16) for j in range(nh)], axis=1)

    live = c > 0
    r_run = r_sc[...]
    weights = []
    for r in range(npg):
        z = z_sc[r]
        sp = jnp.where(live, _softplus(z), 0.0)
        weights.append(jnp.where(live, jnp.exp(z - sp - _suffix_sums(sp, upper) - r_run), 0.0).astype(BF16))
        r_run = r_run + jnp.sum(sp, axis=1, keepdims=True)
    r_sc[...] = r_run

    q = q_ref[...]
    q_bd = jnp.concatenate([jnp.where(hrow == j, q, 0.0) for j in range(nh)], axis=1).astype(BF16)
    for r in range(npg):
        z_sc[r] = _nt(q_bd, heads_on_lanes(k_refs[r])) * scale

    upd = jnp.zeros((nh, HEAD_DIM), F32)
    for r in range(npg):
        res = jnp.dot(weights[r], heads_on_lanes(v_refs[r]), preferred_element_type=F32)
        for j in range(nh):
            upd = upd + jnp.where(hrow == j, res[:, j * HEAD_DIM:(j + 1) * HEAD_DIM], 0.0)
    acc_sc[...] += upd

    @pl.when(c == pl.num_programs(1) - 1)
    def _():
        o_ref[...] = acc_sc[...].astype(o_ref.dtype)


def _sb_dec(q, k_pool, v_pool, layer, page_table):
    b, nh, _ = q.shape
    n_pages = page_table.shape[1]
    npg = SB_PAGES_PER_STEP
    assert n_pages % npg == 0
    n_steps = n_pages // npg
    rows = PAGE_SIZE * nh
    k_pool = k_pool.reshape(k_pool.shape[:2] + (rows, HEAD_DIM))
    v_pool = v_pool.reshape(v_pool.shape[:2] + (rows, HEAD_DIM))

    def k_spec(r):
        return pl.BlockSpec((None, None, rows, HEAD_DIM), lambda bi, c, pt: (
            layer, pt[bi, n_pages - 1 - (npg * jnp.minimum(c, n_steps - 1) + r)], 0, 0))

    def v_spec(r):
        return pl.BlockSpec((None, None, rows, HEAD_DIM), lambda bi, c, pt: (
            layer, pt[bi, n_pages - 1 - (npg * jnp.maximum(c - 1, 0) + r)], 0, 0))

    grid_spec = pltpu.PrefetchScalarGridSpec(
        num_scalar_prefetch=1,
        grid=(b, n_steps + 1),
        in_specs=[pl.BlockSpec((None, nh, HEAD_DIM), lambda bi, c, pt: (bi, 0, 0))]
        + [k_spec(r) for r in range(npg)] + [v_spec(r) for r in range(npg)],
        out_specs=pl.BlockSpec((None, nh, HEAD_DIM), lambda bi, c, pt: (bi, 0, 0)),
        scratch_shapes=[pltpu.VMEM((npg, nh, PAGE_SIZE), F32), pltpu.VMEM((nh, 1), F32),
                        pltpu.VMEM((nh, HEAD_DIM), F32)],
    )
    return pl.pallas_call(
        _sb_dec_kernel,
        grid_spec=grid_spec,
        out_shape=jax.ShapeDtypeStruct((b, nh, HEAD_DIM), BF16),
        compiler_params=_cp("parallel", "arbitrary"),
        name="sb_decode",
    )(page_table, q, *([k_pool] * npg), *([v_pool] * npg))


def _chunk_sums(x, w0, w1):
    x3 = x.reshape(x.shape[0] // CMP_STRIDE, CMP_STRIDE, x.shape[1])
    return jnp.sum(x3 * w0[None], axis=1), jnp.sum(x3 * w1[None], axis=1)


def _compress_kernel(kc_ref, vc_ref, wpos_ref, wlin_ref, ko_ref, vo_ref):
    nchunk = kc_ref.shape[0] // CMP_STRIDE
    for idx, (src, dst) in enumerate(((kc_ref, ko_ref), (vc_ref, vo_ref))):
        a, bsum = _chunk_sums(src[...], wpos_ref[idx, 0:CMP_STRIDE, :], wpos_ref[idx, CMP_STRIDE:, :])
        pooled = a + pltpu.roll(bsum, nchunk - 1, 0)
        for h in range(NSA_KV_HEADS):
            sl = slice(h * HEAD_DIM, (h + 1) * HEAD_DIM)
            dst[:, sl] = jnp.dot(pooled[:, sl].astype(BF16), wlin_ref[idx, h], preferred_element_type=F32)


def _compress_prompt(h, wpos, wlin):
    b, t, _ = h.shape
    kvw = NSA_KV_HEADS * HEAD_DIM
    nchunk = t // CMP_STRIDE
    out = jax.ShapeDtypeStruct((b, nchunk, kvw), F32)
    ospec = pl.BlockSpec((None, nchunk, kvw), lambda i: (i, 0, 0))
    return pl.pallas_call(
        _compress_kernel,
        name="compress_prompt",
        grid=(b,),
        in_specs=[pl.BlockSpec((None, t, kvw), lambda i: (i, 0, (E_KC - E_R) // kvw)),
                  pl.BlockSpec((None, t, kvw), lambda i: (i, 0, (E_VC - E_R) // kvw)),
                  pl.BlockSpec(wpos.shape, lambda i: (0, 0, 0)),
                  pl.BlockSpec(wlin.shape, lambda i: (0, 0, 0, 0))],
        out_specs=[ospec, ospec],
        out_shape=[out, out],
        compiler_params=_cp("parallel"),
    )(h, h, wpos, wlin)


def _topk_keep(score, k_top):
    srow = lax.broadcasted_iota(jnp.int32, score.shape, 0)
    rank = jnp.zeros(score.shape, jnp.int32)
    for s2 in range(score.shape[0]):
        row = score[s2:s2 + 1, :]
        rank = rank + jnp.where(row > score, 1, jnp.where(row == score, jnp.where(srow > s2, 1, 0), 0))
    return jnp.where(rank < k_top, jnp.where(score > 0.5 * NEG_INF, 1.0, 0.0), 0.0)


def _flash_init(m_sc, l_sc, acc_sc):
    m_sc[...] = jnp.full(m_sc.shape, NEG_INF, F32)
    l_sc[...] = jnp.zeros(l_sc.shape, F32)
    acc_sc[...] = jnp.zeros(acc_sc.shape, F32)


def _flash_chunk(q_all, k, v, bias_fn, mask, m_sc, l_sc, acc_sc):
    ng = NSA_GROUP
    r = q_all.shape[0] // ng
    s_all = _nt(k.astype(BF16), q_all) * (HEAD_DIM ** -0.5)
    v_t = v.T.astype(BF16)
    m_prev = m_sc[...]
    m_parts, p_parts = [], []
    for g in range(ng):
        sl = slice(g * r, (g + 1) * r)
        s = s_all[:, sl] + bias_fn(g)
        if mask is not None:
            s = jnp.where(mask, s, NEG_INF)
        mn = jnp.maximum(m_prev[:, sl], jnp.max(s, axis=0, keepdims=True))
        p_parts.append(jnp.exp(s - mn))
        m_parts.append(mn)
    m_new = jnp.concatenate(m_parts, axis=1)
    p = jnp.concatenate(p_parts, axis=1)
    alpha = jnp.exp(m_prev - m_new)
    l_sc[...] = alpha * l_sc[...] + jnp.sum(p, axis=0, keepdims=True)
    acc_sc[...] = alpha * acc_sc[...] + jnp.dot(v_t, p.astype(BF16), preferred_element_type=F32)
    m_sc[...] = m_new


def _nsa_kernel(b31_ref, q_ref, g_ref, kc_ref, vc_ref, ks_ref, vs_ref, kvw_ref, tabc_ref, tabd_ref, o_ref,
                m_sc, l_sc, acc_sc, oc_sc, os_sc, keep_sc):
    i = pl.program_id(1)
    qb = q_ref.shape[0]
    t_len = ks_ref.shape[0]
    nc = kc_ref.shape[0]
    n_cmp = (t_len - CMP_BLOCK) // CMP_STRIDE + 1
    n_slc = t_len // SEL_BLOCK
    k_top = min(N_SEL, n_slc)
    gsz = NSA_GROUP
    scale = HEAD_DIM ** -0.5
    keyi = lax.broadcasted_iota(jnp.int32, (qb, qb), 0)
    qi = lax.broadcasted_iota(jnp.int32, (qb, qb), 1)
    causal = keyi <= qi
    wedge = keyi >= qi
    first_half = keyi < SEL_BLOCK
    pos_row = i * qb + lax.broadcasted_iota(jnp.int32, (1, qb), 1)
    sig_t = jax.nn.sigmoid(g_ref[...]).T
    osb = lax.broadcasted_iota(jnp.int32, (n_slc, nc), 0)
    on = lax.broadcasted_iota(jnp.int32, (n_slc, nc), 1)
    c_start = on * CMP_STRIDE
    overlap_t = jnp.where((c_start < osb * SEL_BLOCK + SEL_BLOCK) & (c_start + CMP_BLOCK - 1 >= osb * SEL_BLOCK)
                          & (on < n_cmp), 1.0, 0.0).astype(BF16)
    nrow = lax.broadcasted_iota(jnp.int32, (nc, qb), 0)
    cmask = ((nrow * CMP_STRIDE + CMP_BLOCK - 1) <= pos_row) & (nrow < n_cmp)
    srow = lax.broadcasted_iota(jnp.int32, (n_slc, qb), 0)
    cur = pos_row >> SEL_SHIFT
    forced = (srow == 0) | (srow == cur) | (srow == cur - 1)
    future = srow > cur

    for h in range(NSA_KV_HEADS):
        hs = slice(h * HEAD_DIM, (h + 1) * HEAD_DIM)
        g0 = h * gsz
        q_all = jnp.concatenate(
            [q_ref[:, (g0 + g) * HEAD_DIM:(g0 + g + 1) * HEAD_DIM] for g in range(gsz)], axis=0).astype(BF16)

        s_all = _nt(kc_ref[:, hs].astype(BF16), q_all) * scale
        vc_t = vc_ref[:, hs].T.astype(BF16)
        pcs = jnp.zeros((nc, qb), F32)
        pc_parts = []
        for g in range(gsz):
            s = s_all[:, g * qb:(g + 1) * qb] + tabc_ref[g0 + g]
            s = jnp.where(cmask, s, NEG_INF)
            s = s - jnp.max(s, axis=0, keepdims=True)
            e = jnp.exp(s)
            pc = jnp.where(cmask, e / jnp.sum(e, axis=0, keepdims=True), 0.0)
            pcs = pcs + pc
            pc_parts.append(pc.astype(BF16))
        oc_sc[...] = jnp.dot(vc_t, jnp.concatenate(pc_parts, axis=1), preferred_element_type=F32)
        imp = sum(jnp.dot(overlap_t, piece, preferred_element_type=F32) for piece in _split3(pcs))
        score = jnp.where(future, NEG_INF, jnp.where(forced, BIG, imp))
        keep_sc[...] = _topk_keep(score, k_top)

        def sel_mask(c, nk=1):
            parts = []
            for u in range(nk):
                first = keep_sc[pl.ds(2 * (c + u), 1), :]
                second = keep_sc[pl.ds(2 * (c + u) + 1, 1), :]
                parts.append(jnp.where(first_half, first, second))
            return jnp.concatenate(parts, axis=0) > 0.5

        def chunk(k_ref, v_ref, c, ksl, vsl, bias_fn, mask, nk=1):
            off = pl.multiple_of(c * qb, qb)
            _flash_chunk(q_all, k_ref[pl.ds(off, nk * qb), ksl], v_ref[pl.ds(off, nk * qb), vsl], bias_fn, mask,
                         m_sc, l_sc, acc_sc)

        tab0 = lambda g: tabd_ref[g0 + g, 0]
        tab1 = lambda g: tabd_ref[g0 + g, 1]
        far = lambda g: b31_ref[g0 + g]

        _flash_init(m_sc, l_sc, acc_sc)
        chunk(ks_ref, vs_ref, i, hs, hs, tab0, sel_mask(i) & causal)

        @pl.when(i >= 1)
        def _():
            chunk(ks_ref, vs_ref, i - 1, hs, hs, tab1, sel_mask(i - 1))

        n_far = jnp.maximum(i - 1, 0)

        def far_body(pair, carry):
            chunk(ks_ref, vs_ref, 2 * pair, hs, hs, far, sel_mask(2 * pair, 2), 2)
            return carry

        lax.fori_loop(0, n_far // 2, far_body, 0)

        @pl.when(n_far % 2 == 1)
        def _():
            chunk(ks_ref, vs_ref, n_far - 1, hs, hs, far, sel_mask(n_far - 1))

        os_sc[...] = acc_sc[...] / l_sc[...]

        kw_sl = hs
        vw_sl = slice(NSA_KV_HEADS * HEAD_DIM + h * HEAD_DIM, NSA_KV_HEADS * HEAD_DIM + (h + 1) * HEAD_DIM)
        _flash_init(m_sc, l_sc, acc_sc)
        chunk(kvw_ref, kvw_ref, i, kw_sl, vw_sl, tab0, causal)

        @pl.when(i >= 1)
        def _():
            chunk(kvw_ref, kvw_ref, i - 1, kw_sl, vw_sl, tab1, None)

        @pl.when(i >= 3)
        def _():
            chunk(kvw_ref, kvw_ref, i - 3, kw_sl, vw_sl, far, None, 2)

        @pl.when(i == 2)
        def _():
            chunk(kvw_ref, kvw_ref, 0, kw_sl, vw_sl, far, None)

        @pl.when(i >= 4)
        def _():
            chunk(kvw_ref, kvw_ref, i - 4, kw_sl, vw_sl, far, wedge)

        ow = acc_sc[...] / l_sc[...]

        def gate(j):
            return jnp.concatenate([sig_t[(g0 + g) * 3 + j:(g0 + g) * 3 + j + 1, :] for g in range(gsz)], axis=1)

        y_t = gate(0) * oc_sc[...] + gate(1) * os_sc[...] + gate(2) * ow
        for g in range(gsz):
            o_ref[:, (g0 + g) * HEAD_DIM:(g0 + g + 1) * HEAD_DIM] = y_t[:, g * qb:(g + 1) * qb].T.astype(o_ref.dtype)


def _nsa_prompt(hq, h, kcmp, vcmp, tabc, tabd, b31, qb=128):
    b, t, _ = h.shape
    assert qb == HEAD_DIM == 2 * SEL_BLOCK and WINDOW == 4 * qb and t % qb == 0
    qw = NSA_HEADS * HEAD_DIM
    kvw = NSA_KV_HEADS * HEAD_DIM
    nchunk = kcmp.shape[1]
    gq = NSA_GROUP * qb
    return pl.pallas_call(
        _nsa_kernel,
        grid=(b, t // qb),
        in_specs=[pl.BlockSpec(memory_space=pltpu.SMEM),
                  pl.BlockSpec((None, qb, qw), lambda bi, i: (bi, i, 0)),
                  pl.BlockSpec((None, qb, LANES), lambda bi, i: (bi, i, (E_G - E_R) // LANES)),
                  pl.BlockSpec((None, nchunk, kvw), lambda bi, i: (bi, 0, 0)),
                  pl.BlockSpec((None, nchunk, kvw), lambda bi, i: (bi, 0, 0)),
                  pl.BlockSpec((None, t, kvw), lambda bi, i: (bi, 0, (E_KS - E_R) // kvw)),
                  pl.BlockSpec((None, t, kvw), lambda bi, i: (bi, 0, (E_VS - E_R) // kvw)),
                  pl.BlockSpec((None, t, 2 * kvw), lambda bi, i: (bi, 0, (E_KVW - E_R) // (2 * kvw))),
                  pl.BlockSpec((None, NSA_HEADS, nchunk, qb), lambda bi, i: (i, 0, 0, 0)),
                  pl.BlockSpec((NSA_HEADS, 2, qb, qb), lambda bi, i: (0, 0, 0, 0))],
        out_specs=pl.BlockSpec((None, qb, qw), lambda bi, i: (bi, i, 0)),
        out_shape=jax.ShapeDtypeStruct((b, t, qw), BF16),
        scratch_shapes=[pltpu.VMEM((1, gq), F32), pltpu.VMEM((1, gq), F32),
                        pltpu.VMEM((HEAD_DIM, gq), F32), pltpu.VMEM((HEAD_DIM, gq), F32),
                        pltpu.VMEM((HEAD_DIM, gq), F32), pltpu.VMEM((t // SEL_BLOCK, qb), F32)],
        compiler_params=_cp("parallel", "parallel"),
        name="nsa_prompt",
    )(b31, hq, h, kcmp, vcmp, h, h, h, tabc, tabd)


def _cmp_dec_kernel(pt_ref, *refs, n_pg):
    k_refs = refs[:n_pg]
    v_refs = refs[n_pg:2 * n_pg]
    wpos_ref = refs[2 * n_pg]
    ak_ref, bk_ref, av_ref, bv_ref = refs[2 * n_pg + 1:]
    per = PAGE_SIZE // CMP_STRIDE
    for idx, (srcs, a_ref, b_ref) in enumerate(((k_refs, ak_ref, bk_ref), (v_refs, av_ref, bv_ref))):
        for r in range(n_pg):
            for h in range(NSA_KV_HEADS):
                sl = slice(h * HEAD_DIM, (h + 1) * HEAD_DIM)
                x = srcs[r][pl.ds(h, PAGE_SIZE, stride=NSA_KV_HEADS), :]
                a, bsum = _chunk_sums(x, wpos_ref[idx, 0:CMP_STRIDE, sl], wpos_ref[idx, CMP_STRIDE:, sl])
                a_ref[r * per:(r + 1) * per, sl] = a
                b_ref[r * per:(r + 1) * per, sl] = bsum


def _cmp_dec(k_pool, v_pool, layer, page_table, wpos, n_pg=8):
    b, n_pages = page_table.shape
    kvw = NSA_KV_HEADS * HEAD_DIM
    per = PAGE_SIZE // CMP_STRIDE
    rows = PAGE_SIZE * NSA_KV_HEADS
    k_pool = k_pool.reshape(k_pool.shape[:2] + (rows, HEAD_DIM))
    v_pool = v_pool.reshape(v_pool.shape[:2] + (rows, HEAD_DIM))

    def page(r):
        return pl.BlockSpec((None, None, rows, HEAD_DIM), lambda bi, c, pt: (layer, pt[bi, c * n_pg + r], 0, 0))

    ospec = pl.BlockSpec((None, n_pg * per, kvw), lambda bi, c, pt: (bi, c, 0))
    out = jax.ShapeDtypeStruct((b, n_pages * per, kvw), F32)
    grid_spec = pltpu.PrefetchScalarGridSpec(
        num_scalar_prefetch=1,
        grid=(b, n_pages // n_pg),
        in_specs=[page(r) for r in range(n_pg)] * 2 + [pl.BlockSpec(wpos.shape, lambda bi, c, pt: (0, 0, 0))],
        out_specs=[ospec] * 4,
    )
    return pl.pallas_call(
        functools.partial(_cmp_dec_kernel, n_pg=n_pg),
        grid_spec=grid_spec,
        out_shape=[out] * 4,
        compiler_params=_cp("parallel", "parallel"),
        name="cmp_decode",
    )(page_table, *([k_pool] * n_pg), *([v_pool] * n_pg), wpos)


def _nsa_dec_cmp_kernel(q_ref, ak_ref, bk_ref, av_ref, bv_ref, wlin_ref, bias_ref, oc_ref, idx_ref,
                        *, pos, n_cmp, n_slc):
    nrow = ak_ref.shape[0]
    sw = idx_ref.shape[-1]
    lane_n = lax.broadcasted_iota(jnp.int32, (1, nrow), 1)
    cmask = ((lane_n * CMP_STRIDE + CMP_BLOCK - 1) <= pos) & (lane_n < n_cmp)
    on = lax.broadcasted_iota(jnp.int32, (nrow, sw), 0)
    osb = lax.broadcasted_iota(jnp.int32, (nrow, sw), 1)
    c_start = on * CMP_STRIDE
    overlap = jnp.where((c_start < osb * SEL_BLOCK + SEL_BLOCK) & (c_start + CMP_BLOCK - 1 >= osb * SEL_BLOCK)
                        & (on < n_cmp) & (osb < n_slc), 1.0, 0.0).astype(BF16)
    pk = ak_ref[...] + pltpu.roll(bk_ref[...], nrow - 1, 0)
    pv = av_ref[...] + pltpu.roll(bv_ref[...], nrow - 1, 0)
    lane_s = lax.broadcasted_iota(jnp.int32, (1, sw), 1)
    ri = lax.broadcasted_iota(jnp.int32, (sw, sw), 0)
    ci = lax.broadcasted_iota(jnp.int32, (sw, sw), 1)
    k_top = min(N_SEL, n_slc)
    cur = pos // SEL_BLOCK
    for h in range(NSA_KV_HEADS):
        sl = slice(h * HEAD_DIM, (h + 1) * HEAD_DIM)
        kc = jnp.dot(pk[:, sl].astype(BF16), wlin_ref[0, h], preferred_element_type=F32).astype(BF16)
        vc = jnp.dot(pv[:, sl].astype(BF16), wlin_ref[1, h], preferred_element_type=F32).astype(BF16)
        q = q_ref[h].astype(BF16)
        s = _nt(q, kc) * (HEAD_DIM ** -0.5) + bias_ref[h * NSA_GROUP:(h + 1) * NSA_GROUP]
        s = jnp.where(cmask, s, NEG_INF)
        s = s - jnp.max(s, axis=-1, keepdims=True)
        e = jnp.exp(s)
        pc = jnp.where(cmask, e / jnp.sum(e, axis=-1, keepdims=True), 0.0)
        oc_ref[h] = jnp.dot(pc.astype(BF16), vc, preferred_element_type=F32)
        pcs = jnp.sum(pc, axis=0, keepdims=True)
        hi, mid, lo = _split3(jnp.broadcast_to(pcs, (8, nrow)))
        imp = (jnp.dot(hi, overlap, preferred_element_type=F32) + jnp.dot(mid, overlap, preferred_element_type=F32)
               + jnp.dot(lo, overlap, preferred_element_type=F32))[0:1]
        forced = (lane_s == 0) | (lane_s == cur) | (lane_s == cur - 1)
        score = jnp.where(lane_s > cur, NEG_INF, jnp.where(forced, BIG, imp))
        score = jnp.where(lane_s < n_slc, score, -jnp.inf)
        rowm = jnp.broadcast_to(score, (sw, sw))
        colm = rowm.T
        beats = jnp.where(colm > rowm, 1, jnp.where(colm == rowm, jnp.where(ri < ci, 1, 0), 0))
        beats = jnp.where(ri < n_slc, beats, 0)
        rank = jnp.sum(beats, axis=0, keepdims=True)
        keep = (rank < k_top) & (score > 0.5 * NEG_INF) & (lane_s < n_slc)
        out = jnp.full((1, sw), -1, jnp.int32)
        for r in range(k_top):
            hit = keep & (rank == r)
            val = jnp.sum(jnp.where(hit, lane_s + 1, 0), axis=1, keepdims=True) - 1
            out = jnp.where(lane_s == r, val, out)
        idx_ref[h] = out


def _nsa_dec_cmp(q, ak, bk, av, bv, wlin, bias_c, pos, n_cmp, n_slc):
    b = q.shape[0]
    nrow = ak.shape[1]
    kvw = ak.shape[2]
    sw = -(-n_slc // LANES) * LANES
    part = pl.BlockSpec((None, nrow, kvw), lambda i: (i, 0, 0))
    kern = functools.partial(_nsa_dec_cmp_kernel, pos=pos, n_cmp=n_cmp, n_slc=n_slc)
    return pl.pallas_call(
        kern,
        name="nsa_dec_cmp",
        grid=(b,),
        in_specs=[pl.BlockSpec((None,) + q.shape[1:], lambda i: (i, 0, 0, 0)), part, part, part, part,
                  pl.BlockSpec(wlin.shape, lambda i: (0, 0, 0, 0)),
                  pl.BlockSpec(bias_c.shape, lambda i: (0, 0))],
        out_specs=[pl.BlockSpec((None,) + q.shape[1:], lambda i: (i, 0, 0, 0)),
                   pl.BlockSpec((None, NSA_KV_HEADS, 1, sw), lambda i: (i, 0, 0, 0))],
        out_shape=[jax.ShapeDtypeStruct(q.shape, F32), jax.ShapeDtypeStruct((b, NSA_KV_HEADS, 1, sw), jnp.int32)],
        compiler_params=_cp("parallel"),
    )(q, ak, bk, av, bv, wlin, bias_c)


def _bucket_bias(dist, rb):
    n = jnp.maximum(dist, 0)
    max_exact = N_BUCKETS // 2
    nf = jnp.maximum(n, 1).astype(F32)
    large = max_exact + (jnp.log(nf / max_exact) / math.log(MAX_DISTANCE / max_exact)
                         * (N_BUCKETS - max_exact)).astype(jnp.int32)
    bucket = jnp.where(n < max_exact, n, jnp.minimum(large, N_BUCKETS - 1))
    out = jnp.zeros((rb.shape[0], dist.shape[1]), F32)
    for k in range(N_BUCKETS):
        out = jnp.where(bucket == k, rb[:, k:k + 1], out)
    return out


SEL_BLOCKS_PER_STEP = 4


def _nsa_dec_sel_kernel(pg_ref, hf_ref, blk_ref, q_ref, kn_ref, vn_ref, rb_ref, *refs, pos):
    nb = SEL_BLOCKS_PER_STEP
    kv_refs = refs[:4 * nb]
    o_ref, m_sc, l_sc, acc_sc = refs[4 * nb:]
    bi = pl.program_id(0)
    j = pl.program_id(1)
    k_top = pl.num_programs(1) * nb
    scale = HEAD_DIM ** -0.5
    lane = lax.broadcasted_iota(jnp.int32, (1, SEL_BLOCK), 1)

    @pl.when(j == 0)
    def _():
        for h in range(NSA_KV_HEADS):
            q = q_ref[h]
            rb = rb_ref[h]
            s = jnp.sum(q.astype(BF16).astype(F32) * kn_ref[h:h + 1, :].astype(BF16).astype(F32),
                        axis=-1, keepdims=True) * scale + rb[:, 0:1]
            m_sc[h] = s
            l_sc[h] = jnp.ones_like(s)
            acc_sc[h] = jnp.broadcast_to(vn_ref[h:h + 1, :].astype(BF16).astype(F32), (q.shape[0], HEAD_DIM))

    for r, h in [(r, h) for r in range(nb) for h in range(NSA_KV_HEADS)]:
        k_ref, v_ref = kv_refs[4 * r + h], kv_refs[4 * r + 2 + h]
        blk = blk_ref[bi, h * k_top + j * nb + r]

        @pl.when(blk >= 0)
        def _(h=h, k_ref=k_ref, v_ref=v_ref, blk=blk):
            q = q_ref[h].astype(BF16)
            k = k_ref[:, h, :].astype(BF16)
            v = v_ref[:, h, :].astype(BF16)
            tok = blk * SEL_BLOCK + lane
            ok = tok <= pos
            s = _nt(q, k) * scale + _bucket_bias(pos - tok, rb_ref[h])
            s = jnp.where(ok, s, NEG_INF)
            m_prev = m_sc[h]
            m_new = jnp.maximum(m_prev, jnp.max(s, axis=-1, keepdims=True))
            alpha = jnp.exp(m_prev - m_new)
            p = jnp.where(ok, jnp.exp(s - m_new), 0.0)
            l_sc[h] = alpha * l_sc[h] + jnp.sum(p, axis=-1, keepdims=True)
            acc_sc[h] = alpha * acc_sc[h] + jnp.dot(p.astype(BF16), v, preferred_element_type=F32)
            m_sc[h] = m_new

    @pl.when(j == pl.num_programs(1) - 1)
    def _():
        for h in range(NSA_KV_HEADS):
            o_ref[h] = acc_sc[h] / l_sc[h]


def _nsa_dec_sel(q, k_new, v_new, rb, k_pool, v_pool, layer, pages, halves, blocks, pos):
    b = q.shape[0]
    k_top = blocks.shape[-1]
    nb = SEL_BLOCKS_PER_STEP
    assert k_top % nb == 0
    pages, halves, blocks = (a.reshape(b, NSA_KV_HEADS * k_top) for a in (pages, halves, blocks))

    def blkspec(r, h):
        return pl.BlockSpec((None, None, SEL_BLOCK, NSA_KV_HEADS, HEAD_DIM), lambda bi, j, pg, hf, bl: (
            layer, pg[bi, h * k_top + j * nb + r], hf[bi, h * k_top + j * nb + r], 0, 0))

    qspec = pl.BlockSpec((None,) + q.shape[1:], lambda bi, j, pg, hf, bl: (bi, 0, 0, 0))
    nspec = pl.BlockSpec((None, NSA_KV_HEADS, HEAD_DIM), lambda bi, j, pg, hf, bl: (bi, 0, 0))
    kv_specs = [blkspec(r, h) for r in range(nb) for _ in range(2) for h in range(NSA_KV_HEADS)]
    kv_args = [pool for _ in range(nb) for pool in (k_pool, k_pool, v_pool, v_pool)]
    grid_spec = pltpu.PrefetchScalarGridSpec(
        num_scalar_prefetch=3,
        grid=(b, k_top // nb),
        in_specs=[qspec, nspec, nspec, pl.BlockSpec(rb.shape, lambda bi, j, pg, hf, bl: (0, 0, 0))] + kv_specs,
        out_specs=qspec,
        scratch_shapes=[pltpu.VMEM((NSA_KV_HEADS, NSA_GROUP, 1), F32), pltpu.VMEM((NSA_KV_HEADS, NSA_GROUP, 1), F32),
                        pltpu.VMEM((NSA_KV_HEADS, NSA_GROUP, HEAD_DIM), F32)],
    )
    return pl.pallas_call(
        functools.partial(_nsa_dec_sel_kernel, pos=pos),
        name="nsa_dec_sel",
        grid_spec=grid_spec,
        out_shape=jax.ShapeDtypeStruct(q.shape, F32),
        compiler_params=_cp("parallel", "arbitrary"),
    )(pages, halves, blocks, q, k_new, v_new, rb, *kv_args)


def _nsa_dec_win_kernel(q_ref, g_ref, win_ref, new_ref, bias_ref, oc_ref, os_ref, y_ref, nw_ref):
    scale = HEAD_DIM ** -0.5
    nwin = win_ref.shape[0]
    sig = jax.nn.sigmoid(g_ref[...])
    for h in range(NSA_KV_HEADS):
        q = q_ref[h].astype(BF16)
        k = win_ref[:, 0, h, :].astype(BF16)
        v = win_ref[:, 1, h, :].astype(BF16)
        kn = new_ref[0, h:h + 1, :].astype(BF16)
        vn = new_ref[1, h:h + 1, :].astype(BF16)
        bias = bias_ref[h * NSA_GROUP:(h + 1) * NSA_GROUP]
        s = _nt(q, k) * scale + bias[:, :nwin]
        s_new = jnp.sum(q.astype(F32) * kn.astype(F32), axis=-1, keepdims=True) * scale + bias[:, nwin:nwin + 1]
        m = jnp.maximum(jnp.max(s, axis=-1, keepdims=True), s_new)
        p = jnp.exp(s - m)
        p_new = jnp.exp(s_new - m)
        den = jnp.sum(p, axis=-1, keepdims=True) + p_new
        ow = (jnp.dot(p.astype(BF16), v, preferred_element_type=F32)
              + p_new.astype(BF16).astype(F32) * vn.astype(F32)) / den
        gs = sig[h]
        y_ref[h] = (gs[:, 0:1] * oc_ref[h] + gs[:, 1:2] * os_ref[h] + gs[:, 2:3] * ow).astype(y_ref.dtype)
    nw_ref[pl.ds(0, nwin - 1)] = win_ref[pl.ds(1, nwin - 1)]
    nw_ref[nwin - 1] = new_ref[...]


def _nsa_dec_win(q, gates, win, new_kv, bias_w, o_c, o_s):
    b = q.shape[0]
    qspec = pl.BlockSpec((None,) + q.shape[1:], lambda i: (i, 0, 0, 0))
    wspec = pl.BlockSpec((None,) + win.shape[1:], lambda i: (i, 0, 0, 0, 0))
    bias2 = bias_w[:, 0, :]
    return pl.pallas_call(
        _nsa_dec_win_kernel,
        name="nsa_dec_win",
        grid=(b,),
        in_specs=[qspec, pl.BlockSpec((None,) + gates.shape[1:], lambda i: (i, 0, 0, 0)), wspec,
                  pl.BlockSpec((None,) + new_kv.shape[1:], lambda i: (i, 0, 0, 0)),
                  pl.BlockSpec(bias2.shape, lambda i: (0, 0)), qspec, qspec],
        out_specs=[qspec, wspec],
        out_shape=[jax.ShapeDtypeStruct(q.shape, BF16), jax.ShapeDtypeStruct(win.shape, F32)],
        compiler_params=_cp("parallel"),
    )(q, gates, win, new_kv, bias2, o_c, o_s)


def _tail(x, xb_unused, y_mix, mem_kv, ffn_prev, p, l, bsz, t, alpha):
    d = x.shape[1]
    x1, x1b = _add_ln(x, y_mix, p['ln_g'][l, 0], p['ln_b'][l, 0], alpha)
    tm = 1024
    q = _mm(x1b, p['w_cq'], F32, tm, 512, "mm_cq", layer=l)
    o = _cross(q.reshape(bsz, t, -1), mem_kv, 512).reshape(bsz * t, -1)
    x2, x2b = _mm_add_ln(o, p['w_co'], l, x1, p['ln_g'][l, 1], p['ln_b'][l, 1], alpha)
    if t > 1:
        act, st1, st2 = _ffn_up(x2b, p['w_up'], l, p['ffn_conv'][l], t)
        ffn_new = jnp.concatenate([st1[:, 6:], st2[:, 6:]], axis=-1)
    else:
        act, ffn_new = _ffn_up_dec(x2b, p['w_up'], l, p['ffn_conv'][l], ffn_prev)
    dff = act.shape[1]
    f = _mm(act, p['w_down'], F32, 512, 512, "mm_down", layer=l)
    x3, x3b = _add_ln(x2, f, p['ln_g'][l, 2], p['ln_b'][l, 2], alpha)
    return x3, x3b, ffn_new


def kernel(x_prompt, x_sample, mem_prompt, state_pool, cache_nsa_cmp_k, cache_nsa_cmp_v, cache_nsa_sel_k, cache_nsa_sel_v, state_nsa_win, state_sc, cache_sb_k, cache_sb_v, state_ffn, cache_mem, page_table, w_in_even, w_pool, pool_scale, w_cmp_pos, w_cmp_lin, rel_bias, w_out_even, w_in_odd, sc_conv, w_out_odd, w_cq, w_ckv, w_co, w_up, ffn_conv, w_down, ln_g, ln_b):
    bp, t, d = x_prompt.shape
    bs = x_sample.shape[0]
    depth = w_cq.shape[0]
    n_pages = page_table.shape[1]
    past = n_pages * PAGE_SIZE
    alpha = (2.0 * depth) ** 0.25
    kvw = NSA_KV_HEADS * HEAD_DIM
    assert x_sample.shape[1] == 1 and state_nsa_win.shape[2] == WINDOW

    w_even = jnp.pad(w_in_even, ((0, 0), (0, 0), (0, E_TOT - w_in_even.shape[2]))).astype(BF16)
    p = {'w_cq': w_cq.astype(BF16), 'w_co': w_co.astype(BF16), 'w_up': w_up,
         'w_down': w_down.astype(BF16), 'ffn_conv': ffn_conv, 'ln_g': ln_g, 'ln_b': ln_b}
    w_odd = w_in_odd.astype(BF16)
    w_oe = w_out_even.astype(BF16)
    w_oo = w_out_odd.astype(BF16)
    w_kv = w_ckv.astype(BF16)
    w_pool_b = w_pool.astype(BF16)
    w_lin_b = w_cmp_lin.astype(BF16)
    w_pos = w_cmp_pos.reshape(w_cmp_pos.shape[0], 2, CMP_BLOCK, kvw)

    qb = 128
    kj = np.arange(qb)[:, None]
    qt = np.arange(qb)[None, :]
    dist_d = np.stack([qt - kj, qb + qt - kj]).reshape(2 * qb, qb).astype(np.int32)
    tabd = _bias_lookup(rel_bias, jnp.asarray(dist_d), 2 * qb).reshape(NSA_HEADS, 2, qb, qb)
    nchunk = t // CMP_STRIDE
    dist_c = (np.arange(t)[None, :] - (np.arange(nchunk)[:, None] * CMP_STRIDE + CMP_BLOCK - 1)).astype(np.int32)
    tabc = _bias_lookup_blocked(rel_bias, jnp.asarray(dist_c), qb)
    b31 = rel_bias[N_BUCKETS - 1]

    mem_b = mem_prompt.reshape(bp * mem_prompt.shape[1], d).astype(BF16)
    n_mem = mem_prompt.shape[1]

    x = x_prompt.reshape(bp * t, d)
    xb = x.astype(BF16)
    outs_p = {k: [] for k in ('pool', 'cmp_k', 'cmp_v', 'sel_k', 'sel_v', 'win', 'sc', 'sb_k', 'sb_v', 'ffn', 'mem')}
    for l in range(depth):
        e = l // 2
        memkv = _mm(mem_b, w_kv, F32, 1024, 512, "mm_memkv", layer=l)
        outs_p['mem'].append(memkv.reshape(bp, n_mem, 2, MEM_HEADS, MEM_HEAD_DIM))
        if l % 2 == 0:
            qw = E_KC - E_Q
            hu = _mm(xb, w_even[e], F32, 1024, 512, "mm_in_even", E_U, E_Q - E_U).reshape(bp, t, E_Q - E_U)
            hq = _mm(xb, w_even[e], BF16, 1024, 512, "mm_in_even", E_Q, qw).reshape(bp, t, qw)
            h = _mm(xb, w_even[e], F32, 1024, 512, "mm_in_even", E_R, E_TOT - E_R).reshape(bp, t, E_TOT - E_R)
            y_pool = _pool_prompt(hu, 0, w_pool_b[e], pool_scale[e])
            kcmp, vcmp = _compress_prompt(h, w_pos[e], w_lin_b[e])
            y_nsa = _nsa_prompt(hq, h, kcmp, vcmp, tabc, tabd, b31)
            y = _mm_cat(y_pool.reshape(bp * t, -1), y_nsa.reshape(bp * t, -1), w_oe[e], F32, 1024, 512)
            outs_p['pool'].append(hu[:, t - POOL_STATE:])
            for name, off in (('cmp_k', E_KC), ('cmp_v', E_VC), ('sel_k', E_KS), ('sel_v', E_VS)):
                outs_p[name].append(h[:, :, off - E_R:off - E_R + kvw].reshape(bp, t, NSA_KV_HEADS, HEAD_DIM))
            nw = min(WINDOW, t)
            outs_p['win'].append(h[:, t - nw:, E_KVW - E_R:E_KVW - E_R + 2 * kvw]
                                 .reshape(bp, nw, 2, NSA_KV_HEADS, HEAD_DIM))
        else:
            sbw = O_K - O_Q
            hc = _mm(xb, w_odd[e], F32, 1024, 512, "mm_in_odd", O_X, O_Q).reshape(bp, t, O_Q)
            hq = _mm(xb, w_odd[e], BF16, 1024, 512, "mm_in_odd", O_Q, sbw).reshape(bp, t, sbw)
            hk = _mm(xb, w_odd[e], F32, 1024, 512, "mm_in_odd", O_K, sbw).reshape(bp, t, sbw)
            hv = _mm(xb, w_odd[e], F32, 1024, 512, "mm_in_odd", O_V, sbw).reshape(bp, t, sbw)
            y_sc, sc_st = _sconv_prompt(hc, sc_conv[e])
            y_sb = _sb_prompt(hq, hk, hv)
            y = _mm_cat(y_sc.reshape(bp * t, -1), y_sb.reshape(bp * t, -1), w_oo[e], F32, 1024, 512)
            outs_p['sc'].append(sc_st[:, 6:])
            nh = sbw // HEAD_DIM
            outs_p['sb_k'].append(hk.reshape(bp, t, nh, HEAD_DIM))
            outs_p['sb_v'].append(hv.reshape(bp, t, nh, HEAD_DIM))
        x, xb, ffn_new = _tail(x, xb, y, memkv.reshape(bp, n_mem, -1), None, p, l, bp, t, alpha)
        outs_p['ffn'].append(ffn_new)
    y_prompt = x.reshape(bp, t, d)

    pos = past
    length = past + 1
    n_cmp = (length - CMP_BLOCK) // CMP_STRIDE + 1
    n_slc = -(-length // SEL_BLOCK)
    k_top = min(N_SEL, n_slc)
    ncrow = n_pages * (PAGE_SIZE // CMP_STRIDE)
    dist_cd = (pos - (np.arange(ncrow) * CMP_STRIDE + CMP_BLOCK - 1)).astype(np.int32)
    bias_cd = _bias_lookup(rel_bias, jnp.asarray(np.broadcast_to(dist_cd, (8, ncrow))), 8)[:, 0, :]
    dist_wd = np.maximum(WINDOW - np.arange(WINDOW + LANES), 0).astype(np.int32)
    bias_wd = _bias_lookup(rel_bias, jnp.asarray(np.broadcast_to(dist_wd, (8, WINDOW + LANES))), 8)
    rb_hg = rel_bias.T.reshape(NSA_KV_HEADS, NSA_GROUP, N_BUCKETS)
    mem_s = cache_mem.reshape(depth, bs, cache_mem.shape[2], -1)

    x = x_sample.reshape(bs, d)
    xb = x.astype(BF16)
    outs_s = {k: [] for k in ('pool', 'cmp_k', 'cmp_v', 'sel_k', 'sel_v', 'win', 'sc', 'sb_k', 'sb_v', 'ffn')}
    for l in range(depth):
        e = l // 2
        if l % 2 == 0:
            h = _mm(xb, w_even[e], F32, 8, 512)
            y_pool, pool_new = _pool_dec(state_pool[e], h[:, E_U:E_U + 1024], w_pool_b[e], pool_scale[e], pos)
            q4 = h[:, E_Q:E_Q + NSA_HEADS * HEAD_DIM].reshape(bs, NSA_KV_HEADS, NSA_GROUP, HEAD_DIM)
            gates = h[:, E_G:E_G + NSA_HEADS * 3].reshape(bs, NSA_KV_HEADS, NSA_GROUP, 3)
            ak, bk, av, bv = _cmp_dec(cache_nsa_cmp_k, cache_nsa_cmp_v, e, page_table, w_pos[e])
            o_c, idx = _nsa_dec_cmp(q4, ak, bk, av, bv, w_lin_b[e], bias_cd, pos, n_cmp, n_slc)
            blocks = idx[:, :, 0, :k_top]
            past_blk = jnp.where((blocks >= 0) & (blocks * SEL_BLOCK < past), blocks, -1)
            safe = jnp.maximum(past_blk, 0)
            per_page = PAGE_SIZE // SEL_BLOCK
            pages = jnp.take_along_axis(page_table[:, None, :], safe // per_page, axis=2)
            halves = safe % per_page
            ks_new = h[:, E_KS:E_KS + kvw].reshape(bs, NSA_KV_HEADS, HEAD_DIM)
            vs_new = h[:, E_VS:E_VS + kvw].reshape(bs, NSA_KV_HEADS, HEAD_DIM)
            o_s = _nsa_dec_sel(q4, ks_new, vs_new, rb_hg, cache_nsa_sel_k, cache_nsa_sel_v, e,
                               pages, halves, past_blk, pos)
            new_kv = h[:, E_KVW:E_KVW + 2 * kvw].reshape(bs, 2, NSA_KV_HEADS, HEAD_DIM)
            y_nsa, win_new = _nsa_dec_win(q4, gates, state_nsa_win[e], new_kv, bias_wd, o_c, o_s)
            mix = jnp.concatenate([y_pool, y_nsa.reshape(bs, -1)], axis=-1)
            y = _mm(mix, w_oe[e], F32, 8, 512)
            outs_s['pool'].append(pool_new)
            for name, off in (('cmp_k', E_KC), ('cmp_v', E_VC), ('sel_k', E_KS), ('sel_v', E_VS)):
                outs_s[name].append(h[:, off:off + kvw].reshape(bs, 1, NSA_KV_HEADS, HEAD_DIM))
            outs_s['win'].append(win_new)
        else:
            h = _mm(xb, w_odd[e], F32, 8, 512)
            y_sc, sc_new = _sconv_dec(state_sc[e], h, sc_conv[e])
            nh = (O_K - O_Q) // HEAD_DIM
            q3 = h[:, O_Q:O_K].reshape(bs, nh, HEAD_DIM)
            y_sb = _sb_dec(q3, cache_sb_k, cache_sb_v, e, page_table)
            mix = jnp.concatenate([y_sc, y_sb.reshape(bs, -1)], axis=-1)
            y = _mm(mix, w_oo[e], F32, 8, 512)
            outs_s['sc'].append(sc_new)
            outs_s['sb_k'].append(h[:, O_K:O_V].reshape(bs, 1, nh, HEAD_DIM))
            outs_s['sb_v'].append(h[:, O_V:].reshape(bs, 1, nh, HEAD_DIM))
        x, xb, ffn_new = _tail(x, xb, y, mem_s[l], state_ffn[l], p, l, bs, 1, alpha)
        outs_s['ffn'].append(ffn_new)
    y_sample = x.reshape(bs, 1, d)

    sp = {k: jnp.stack(v) for k, v in outs_p.items()}
    ss = {k: jnp.stack(v) for k, v in outs_s.items()}
    return (y_prompt, y_sample,
            sp['pool'], sp['cmp_k'], sp['cmp_v'], sp['sel_k'], sp['sel_v'], sp['win'],
            sp['sc'], sp['sb_k'], sp['sb_v'], sp['ffn'], sp['mem'],
            ss['pool'], ss['cmp_k'], ss['cmp_v'], ss['sel_k'], ss['sel_v'], ss['win'],
            ss['sc'], ss['sb_k'], ss['sb_v'], ss['ffn'])
```
